```python
import jax, jax.numpy as jnp
from jax import lax
import numpy as np

D_MODEL = 1024
BATCH = 8
SEQ = 2048
DEPTH = 4
DEC_BATCH = 128
DEC_SEQ = 1
PAST_LEN = 16384
PAGE_SIZE = 128

N_MIXERS = 4
N_LAYERS_A = (DEPTH + 3) // 4
N_LAYERS_B = (DEPTH + 2) // 4
N_LAYERS_C = (DEPTH + 1) // 4
N_LAYERS_D = DEPTH // 4
D_CONV = D_MODEL
CONV_A_WIDTH = 31
POOL_WINDOWS = (2, 4, 8, 16)
N_POOL_GROUPS = 4
POOL_GROUP = D_MODEL // N_POOL_GROUPS
POOL_HIST = max(POOL_WINDOWS) - 1
D_SCONV = D_MODEL
CONV_C_WIDTH = 3
D_GATE = D_MODEL
CHUNK = 128
N_SGU_GROUPS = 4
N_EXPERTS = 32
TOP_K = 4
D_EXPERT = D_MODEL
SWIGLU_LIMIT = 7.0
SWIGLU_ALPHA = 1.702
MOE_BLOCK = 128
ALPHA = (2 * DEPTH) ** 0.25
BETA = (8 * DEPTH) ** -0.25
LN_EPS = 1e-5

kernel_name = "hybrid_conv_pool_sgu_moe_decoder_step"


def layer_norm(x, g, b):
    xf = x.astype(jnp.float32)
    mu = jnp.mean(xf, axis=-1, keepdims=True)
    var = jnp.mean(jnp.square(xf - mu), axis=-1, keepdims=True)
    return ((xf - mu) * lax.rsqrt(var + LN_EPS) * g + b).astype(x.dtype)


def causal_dwconv(ext, w):
    return lax.conv_general_dilated(ext, w[:, None, :].astype(ext.dtype), window_strides=(1,), padding='VALID',
                                    dimension_numbers=('NWC', 'WIO', 'NWC'), feature_group_count=ext.shape[-1])


def conformer_conv(h, hist, w_in, b_in, w_dw, b_dw, ln_g, ln_b, w_out):
    a, g = jnp.split(h @ w_in + b_in, 2, axis=-1)
    u = a * jax.nn.sigmoid(g)
    ext = jnp.concatenate([hist, u], axis=1)
    y = causal_dwconv(ext, w_dw) + b_dw
    y = jax.nn.silu(layer_norm(y, ln_g, ln_b))
    return y @ w_out, ext[:, -(CONV_A_WIDTH - 1):]


def multiscale_pool(h, hist, start_pos, w_grp, scale):
    B, T, _ = h.shape
    ext = jnp.concatenate([hist, h], axis=1)
    extf = ext.astype(jnp.float32)
    cs = jnp.concatenate([jnp.zeros((B, 1, D_MODEL), jnp.float32), jnp.cumsum(extf, axis=1)], axis=1)
    pos = (start_pos + jnp.arange(T)).astype(jnp.float32)
    outs = []
    for gi, w in enumerate(POOL_WINDOWS):
        sl = slice(gi * POOL_GROUP, (gi + 1) * POOL_GROUP)
        s = cs[:, POOL_HIST + 1:POOL_HIST + 1 + T, sl] - cs[:, POOL_HIST + 1 - w:POOL_HIST + 1 - w + T, sl]
        cnt = jnp.minimum(float(w), pos + 1.0)
        outs.append(s / cnt[None, :, None])
    pooled = (jnp.concatenate(outs, axis=-1) - extf[:, POOL_HIST:]).astype(h.dtype)
    mixed = jnp.einsum('btgc,gcd->btgd', pooled.reshape(B, T, N_POOL_GROUPS, POOL_GROUP), w_grp)
    return mixed.reshape(B, T, D_MODEL) * scale, ext[:, -POOL_HIST:]


def short_gated_conv(h, hist, w_in, w_conv, w_out):
    bg, cg, xin = jnp.split(h @ w_in, 3, axis=-1)
    ext = jnp.concatenate([hist, cg * xin], axis=1)
    y = causal_dwconv(ext, w_conv)
    return (bg * y) @ w_out, ext[:, -(CONV_C_WIDTH - 1):]


def chunk_gmlp(h, w_in, b_in, ln_g, ln_b, w_s, b_s, w_out):
    B, T, _ = h.shape
    u, v = jnp.split(jax.nn.gelu(h @ w_in + b_in, approximate=False), 2, axis=-1)
    v = layer_norm(v, ln_g, ln_b)
    Tp = -(-T // CHUNK) * CHUNK
    vp = jnp.pad(v, ((0, 0), (0, Tp - T), (0, 0))).reshape(B, Tp // CHUNK, CHUNK, N_SGU_GROUPS, D_GATE // N_SGU_GROUPS)
    w = jnp.where(jnp.tril(jnp.ones((CHUNK, CHUNK), bool)), w_s, jnp.zeros((), w_s.dtype))
    mixed = jnp.einsum('gts,bnsgc->bntgc', w, vp) + b_s.T[None, None, :, :, None]
    mixed = mixed.reshape(B, Tp, D_GATE)[:, :T]
    last_chunk = ((T - 1) // CHUNK) * CHUNK
    return (u * mixed) @ w_out, v[:, last_chunk:]


def moe(x, w_router, b_router, w_in, b_in, w_out, b_out):
    shp = x.shape
    xt = x.reshape(-1, D_MODEL)
    N = xt.shape[0]
    logits = (xt @ w_router + b_router).astype(jnp.float32)
    top_logit, top_e = lax.top_k(logits, TOP_K)
    gate = jax.nn.softmax(top_logit, axis=-1)
    A = N * TOP_K
    flat_e = top_e.reshape(-1)
    flat_tok = jnp.arange(A, dtype=jnp.int32) // TOP_K
    order = jnp.argsort(flat_e)
    se = flat_e[order]
    counts = jnp.bincount(flat_e, length=N_EXPERTS)
    padded = (counts + MOE_BLOCK - 1) // MOE_BLOCK * MOE_BLOCK
    pad_end = jnp.cumsum(padded)
    pad_start = pad_end - padded
    start = jnp.cumsum(counts) - counts
    dest = pad_start[se] + jnp.arange(A) - start[se]
    n_blocks = (A + N_EXPERTS * (MOE_BLOCK - 1) + MOE_BLOCK - 1) // MOE_BLOCK
    R = n_blocks * MOE_BLOCK
    row_tok = jnp.full((R,), N, jnp.int32).at[dest].set(flat_tok[order])
    row_gate = jnp.zeros((R,), jnp.float32).at[dest].set(gate.reshape(-1)[order])
    block_e = jnp.minimum(jnp.searchsorted(pad_end, jnp.arange(n_blocks) * MOE_BLOCK, side='right'), N_EXPERTS - 1)
    xpad = jnp.concatenate([xt, jnp.zeros((1, D_MODEL), xt.dtype)], axis=0)
    xb = xpad[row_tok].reshape(n_blocks, MOE_BLOCK, D_MODEL)

    def expert_block(args):
        xg, e = args
        g, u = jnp.split(xg @ w_in[e] + b_in[e], 2, axis=-1)
        g = jnp.minimum(g, SWIGLU_LIMIT)
        u = jnp.clip(u, -SWIGLU_LIMIT, SWIGLU_LIMIT)
        return ((u + 1) * (g * jax.nn.sigmoid(SWIGLU_ALPHA * g))) @ w_out[e] + b_out[e]

    yb = lax.map(expert_block, (xb, block_e)).reshape(R, D_MODEL)
    y = jnp.zeros((N + 1, D_MODEL), x.dtype).at[row_tok].add(yb * row_gate[:, None].astype(x.dtype))[:N]
    return y.reshape(shp)


def setup_inputs(seed: int = 0) -> dict:
    key = jax.random.key(seed)
    ks = iter(jax.random.split(key, 64))

    def nrm(shape, scale=1.0):
        return jax.random.normal(next(ks), shape, jnp.float32) * scale

    D = D_MODEL
    return {
        "x_prompt": nrm((BATCH, SEQ, D)),
        "x_sample": nrm((DEC_BATCH, DEC_SEQ, D)),
        "state_conv_a": nrm((N_LAYERS_A, DEC_BATCH, CONV_A_WIDTH - 1, D_CONV)),
        "state_pool_b": nrm((N_LAYERS_B, DEC_BATCH, POOL_HIST, D)),
        "state_conv_c": nrm((N_LAYERS_C, DEC_BATCH, CONV_C_WIDTH - 1, D_SCONV)),
        "c_prompt": nrm((BATCH, D)),
        "c_sample": nrm((DEC_BATCH, D)),
        "w_ada": nrm((DEPTH, D, 6 * D), 0.1 * D ** -0.5),
        "b_ada": nrm((DEPTH, 6 * D), 0.01),
        "ln1_g": 1.0 + nrm((DEPTH, D), 0.01),
        "ln1_b": nrm((DEPTH, D), 0.01),
        "ln2_g": 1.0 + nrm((DEPTH, D), 0.01),
        "ln2_b": nrm((DEPTH, D), 0.01),
        "a_w_in": nrm((N_LAYERS_A, D, 2 * D_CONV), D ** -0.5),
        "a_b_in": nrm((N_LAYERS_A, 2 * D_CONV), 0.01),
        "a_w_dw": nrm((N_LAYERS_A, CONV_A_WIDTH, D_CONV), CONV_A_WIDTH ** -0.5),
        "a_b_dw": nrm((N_LAYERS_A, D_CONV), 0.01),
        "a_ln_g": 1.0 + nrm((N_LAYERS_A, D_CONV), 0.01),
        "a_ln_b": nrm((N_LAYERS_A, D_CONV), 0.01),
        "a_w_out": nrm((N_LAYERS_A, D_CONV, D), BETA * D_CONV ** -0.5),
        "b_w_grp": nrm((N_LAYERS_B, N_POOL_GROUPS, POOL_GROUP, POOL_GROUP), BETA * POOL_GROUP ** -0.5),
        "b_scale": 1.0 + nrm((N_LAYERS_B, D), 0.1),
        "c_w_in": nrm((N_LAYERS_C, D, 3 * D_SCONV), D ** -0.5),
        "c_w_conv": nrm((N_LAYERS_C, CONV_C_WIDTH, D_SCONV), CONV_C_WIDTH ** -0.5),
        "c_w_out": nrm((N_LAYERS_C, D_SCONV, D), BETA * D_SCONV ** -0.5),
        "d_w_in": nrm((N_LAYERS_D, D, 2 * D_GATE), D ** -0.5),
        "d_b_in": nrm((N_LAYERS_D, 2 * D_GATE), 0.01),
        "d_ln_g": 1.0 + nrm((N_LAYERS_D, D_GATE), 0.01),
        "d_ln_b": nrm((N_LAYERS_D, D_GATE), 0.01),
        "d_w_s": nrm((N_LAYERS_D, N_SGU_GROUPS, CHUNK, CHUNK), CHUNK ** -0.5),
        "d_b_s": 1.0 + nrm((N_LAYERS_D, N_SGU_GROUPS, CHUNK), 0.01),
        "d_w_out": nrm((N_LAYERS_D, D_GATE, D), BETA * D_GATE ** -0.5),
        "w_router": nrm((DEPTH, D, N_EXPERTS), D ** -0.5),
        "b_router": nrm((DEPTH, N_EXPERTS), 0.01),
        "w_moe_in": nrm((DEPTH, N_EXPERTS, D, 2 * D_EXPERT), D ** -0.5),
        "b_moe_in": nrm((DEPTH, N_EXPERTS, 2 * D_EXPERT), 0.01),
        "w_moe_out": nrm((DEPTH, N_EXPERTS, D_EXPERT, D), BETA * D_EXPERT ** -0.5),
        "b_moe_out": nrm((DEPTH, N_EXPERTS, D), 0.01),
    }


def reference(x_prompt, x_sample, state_conv_a, state_pool_b, state_conv_c, c_prompt, c_sample,
              w_ada, b_ada, ln1_g, ln1_b, ln2_g, ln2_b,
              a_w_in, a_b_in, a_w_dw, a_b_dw, a_ln_g, a_ln_b, a_w_out,
              b_w_grp, b_scale, c_w_in, c_w_conv, c_w_out,
              d_w_in, d_b_in, d_ln_g, d_ln_b, d_w_s, d_b_s, d_w_out,
              w_router, b_router, w_moe_in, b_moe_in, w_moe_out, b_moe_out):

    def trunk(x, c, hist_a, hist_b, hist_c, start_pos):
        new_a, new_b, new_c, new_d = [], [], [], []
        sc = jax.nn.silu(c)
        for i in range(DEPTH):
            m, j = i % N_MIXERS, i // N_MIXERS
            ada = sc @ w_ada[i] + b_ada[i]
            sh1, sc1, g1, sh2, sc2, g2 = jnp.split(ada[:, None, :], 6, axis=-1)
            h = x * (1 + sc1) + sh1
            if m == 0:
                out, st = conformer_conv(h, hist_a[j], a_w_in[j], a_b_in[j], a_w_dw[j], a_b_dw[j],
                                         a_ln_g[j], a_ln_b[j], a_w_out[j])
                new_a.append(st)
            elif m == 1:
                out, st = multiscale_pool(h, hist_b[j], start_pos, b_w_grp[j], b_scale[j])
                new_b.append(st)
            elif m == 2:
                out, st = short_gated_conv(h, hist_c[j], c_w_in[j], c_w_conv[j], c_w_out[j])
                new_c.append(st)
            else:
                out, st = chunk_gmlp(h, d_w_in[j], d_b_in[j], d_ln_g[j], d_ln_b[j], d_w_s[j], d_b_s[j], d_w_out[j])
                new_d.append(st)
            x = layer_norm(ALPHA * x + (1 + g1) * out, ln1_g[i], ln1_b[i])
            h = x * (1 + sc2) + sh2
            f = moe(h, w_router[i], b_router[i], w_moe_in[i], b_moe_in[i], w_moe_out[i], b_moe_out[i])
            x = layer_norm(ALPHA * x + (1 + g2) * f, ln2_g[i], ln2_b[i])
        return x, jnp.stack(new_a), jnp.stack(new_b), jnp.stack(new_c), jnp.stack(new_d)

    Bp = x_prompt.shape[0]
    dt = x_prompt.dtype
    zero_a = jnp.zeros((N_LAYERS_A, Bp, CONV_A_WIDTH - 1, D_CONV), dt)
    zero_b = jnp.zeros((N_LAYERS_B, Bp, POOL_HIST, D_MODEL), dt)
    zero_c = jnp.zeros((N_LAYERS_C, Bp, CONV_C_WIDTH - 1, D_SCONV), dt)
    y_prompt, a_p, b_p, c_p, d_p = trunk(x_prompt, c_prompt, zero_a, zero_b, zero_c, 0)
    y_sample, a_s, b_s, c_s, d_s = trunk(x_sample, c_sample, state_conv_a, state_pool_b, state_conv_c, PAST_LEN)
    return (y_prompt, y_sample, a_p, a_s, b_p, b_s, c_p, c_s, d_p, d_s)
```

```python
import functools

import jax
import jax.numpy as jnp
from jax import lax
from jax.experimental import pallas as pl
from jax.experimental.pallas import tpu as pltpu

F32 = jnp.float32
BF16 = jnp.bfloat16

D = 1024
DEPTH = 4
N_EXPERTS = 32
TOP_K = 4
CONV_A_WIDTH = 31
POOL_WINDOWS = (2, 4, 8, 16)
POOL_GROUP = D // 4
POOL_HIST = 15
CONV_C_WIDTH = 3
CHUNK = 128
N_SGU_GROUPS = 4
SGU_GROUP = D // N_SGU_GROUPS
SWIGLU_LIMIT = 7.0
SWIGLU_ALPHA = 1.702
ALPHA = (2 * DEPTH) ** 0.25
LN_EPS = 1e-5

LANES = 128
SUBLANES_BF16 = 16
VMEM_LIMIT = 56 * 1024 * 1024

TOK_TILE = 512
ROW_CHUNK = SUBLANES_BF16
BLOCK_CHUNKS = 32
BLOCK_ROWS = BLOCK_CHUNKS * ROW_CHUNK


def _tile_capacity(tt):
    cap = TOP_K * tt + N_EXPERTS * (ROW_CHUNK - 1)
    return -(-cap // 256) * 256


def _cparams(sem=None):
    return pltpu.CompilerParams(dimension_semantics=sem, vmem_limit_bytes=VMEM_LIMIT)


def _dot(a, b):
    return jnp.dot(a, b, preferred_element_type=F32)


def _ln(x, g, b):
    mu = jnp.mean(x, axis=-1, keepdims=True)
    xc = x - mu
    var = jnp.mean(xc * xc, axis=-1, keepdims=True)
    return xc * lax.rsqrt(var + LN_EPS) * g + b


def _split_ada(ada):
    return [ada[:, i * D:(i + 1) * D] for i in range(6)]


def _post_mixer(x, out, ada, ln_g, ln_b, x1_ref, h2_ref):
    _, _, g1, sh2, sc2, _ = _split_ada(ada)
    x1 = _ln(ALPHA * x + (1.0 + g1) * out, ln_g, ln_b)
    x1_ref[...] = x1
    h2_ref[...] = (x1 * (1.0 + sc2) + sh2).astype(BF16)


def _ada_kernel(c_ref, w_ref, b_ref, o_ref):
    c = c_ref[...]
    sc = (c * jax.nn.sigmoid(c)).astype(BF16)
    o_ref[0] = _dot(sc, w_ref[0].astype(BF16)) + b_ref[0]


def _ada(c_all, w_ada, b_ada):
    rows = c_all.shape[0]
    return pl.pallas_call(
        _ada_kernel,
        out_shape=jax.ShapeDtypeStruct((DEPTH, rows, 6 * D), F32),
        grid=(DEPTH, 6),
        in_specs=[
            pl.BlockSpec((rows, D), lambda i, j: (0, 0)),
            pl.BlockSpec((1, D, D), lambda i, j: (i, 0, j)),
            pl.BlockSpec((1, 1, D), lambda i, j: (i, 0, j)),
        ],
        out_specs=pl.BlockSpec((1, rows, D), lambda i, j: (i, 0, j)),
        compiler_params=_cparams(("arbitrary", "arbitrary")),
        name="ada",
    )(c_all, w_ada, b_ada.reshape(DEPTH, 1, 6 * D))


A_HALO = 32
B_HALO = 16
C_HALO = 8


def _mix_a_prompt_kernel(x_ref, ada_ref, w_in_ref, b_in_ref, w_dw_ref, b_dw_ref, lng_ref, lnb_ref,
                         w_out_ref, ln1g_ref, ln1b_ref, x1_ref, h2_ref, st_ref, u_scr):
    t = pl.program_id(1)
    tt = x_ref.shape[0]
    x = x_ref[...]
    ada = ada_ref[0]
    sh1, sc1 = ada[:, 0:D], ada[:, D:2 * D]
    h = (x * (1.0 + sc1) + sh1).astype(BF16)
    z = _dot(h, w_in_ref[...]) + b_in_ref[...]
    u = z[:, :D] * jax.nn.sigmoid(z[:, D:])

    @pl.when(t == 0)
    def _():
        u_scr[0:A_HALO, :] = jnp.zeros((A_HALO, D), F32)

    u_scr[A_HALO:A_HALO + tt, :] = u
    first = A_HALO - (CONV_A_WIDTH - 1)
    y = jnp.zeros((tt, D), F32) + b_dw_ref[...]
    for k in range(CONV_A_WIDTH):
        y = y + w_dw_ref[k:k + 1, :] * u_scr[first + k:first + k + tt, :]
    y = _ln(y, lng_ref[...], lnb_ref[...])
    y = y * jax.nn.sigmoid(y)
    out = _dot(y.astype(BF16), w_out_ref[...])
    _post_mixer(x, out, ada, ln1g_ref[...], ln1b_ref[...], x1_ref, h2_ref)
    tail = u_scr[tt:tt + A_HALO, :]
    u_scr[0:A_HALO, :] = tail

    @pl.when(t == pl.num_programs(1) - 1)
    def _():
        st_ref[0] = tail


def _mix_b_prompt_kernel(x_ref, ada_ref, w_grp_ref, scale_ref, ln1g_ref, ln1b_ref,
                         x1_ref, h2_ref, st_ref, h_scr):
    t = pl.program_id(1)
    tt = x_ref.shape[0]
    x = x_ref[...]
    ada = ada_ref[0]
    sh1, sc1 = ada[:, 0:D], ada[:, D:2 * D]
    h = x * (1.0 + sc1) + sh1

    @pl.when(t == 0)
    def _():
        h_scr[0:B_HALO, :] = jnp.zeros((B_HALO, D), F32)

    h_scr[B_HALO:B_HALO + tt, :] = h
    pos = (t * tt + lax.broadcasted_iota(jnp.int32, (tt, POOL_GROUP), 0)).astype(F32)
    outs = []
    for gi, w in enumerate(POOL_WINDOWS):
        lo = gi * POOL_GROUP
        s = h_scr[B_HALO:B_HALO + tt, lo:lo + POOL_GROUP]
        for j in range(1, w):
            s = s + h_scr[B_HALO - j:B_HALO - j + tt, lo:lo + POOL_GROUP]
        cnt = jnp.minimum(float(w), pos + 1.0)
        pooled = s / cnt - h[:, lo:lo + POOL_GROUP]
        outs.append(_dot(pooled.astype(BF16), w_grp_ref[gi].astype(BF16)))
    out = jnp.concatenate(outs, axis=-1) * scale_ref[...]
    _post_mixer(x, out, ada, ln1g_ref[...], ln1b_ref[...], x1_ref, h2_ref)
    tail = h_scr[tt:tt + B_HALO, :]
    h_scr[0:B_HALO, :] = tail

    @pl.when(t == pl.num_programs(1) - 1)
    def _():
        st_ref[0] = tail


def _mix_c_prompt_kernel(x_ref, ada_ref, w_in_ref, w_conv_ref, w_out_ref, ln1g_ref, ln1b_ref,
                         x1_ref, h2_ref, st_ref, v_scr):
    t = pl.program_id(1)
    tt = x_ref.shape[0]
    x = x_ref[...]
    ada = ada_ref[0]
    sh1, sc1 = ada[:, 0:D], ada[:, D:2 * D]
    h = (x * (1.0 + sc1) + sh1).astype(BF16)
    z = _dot(h, w_in_ref[...])
    bg = z[:, :D]
    v = z[:, D:2 * D] * z[:, 2 * D:]

    @pl.when(t == 0)
    def _():
        v_scr[0:C_HALO, :] = jnp.zeros((C_HALO, D), F32)

    v_scr[C_HALO:C_HALO + tt, :] = v
    first = C_HALO - (CONV_C_WIDTH - 1)
    y = w_conv_ref[CONV_C_WIDTH - 1:CONV_C_WIDTH, :] * v
    for k in range(CONV_C_WIDTH - 1):
        y = y + w_conv_ref[k:k + 1, :] * v_scr[first + k:first + k + tt, :]
    out = _dot((bg * y).astype(BF16), w_out_ref[...])
    _post_mixer(x, out, ada, ln1g_ref[...], ln1b_ref[...], x1_ref, h2_ref)
    tail = v_scr[tt:tt + C_HALO, :]
    v_scr[0:C_HALO, :] = tail

    @pl.when(t == pl.num_programs(1) - 1)
    def _():
        st_ref[0] = tail


def _gelu_exact(x):
    return 0.5 * x * (1.0 + lax.erf(x * (2.0 ** -0.5)))


def _mix_d_prompt_kernel(x_ref, ada_ref, w_in_ref, b_in_ref, lng_ref, lnb_ref, w_s_ref, b_st_ref,
                         w_out_ref, ln1g_ref, ln1b_ref, x1_ref, h2_ref, st_ref):
    t = pl.program_id(1)
    tt = x_ref.shape[0]
    x = x_ref[...]
    ada = ada_ref[0]
    sh1, sc1 = ada[:, 0:D], ada[:, D:2 * D]
    h = (x * (1.0 + sc1) + sh1).astype(BF16)
    z = _gelu_exact(_dot(h, w_in_ref[...]) + b_in_ref[...])
    u = z[:, :D]
    v = _ln(z[:, D:], lng_ref[...], lnb_ref[...])
    vb = v.astype(BF16)
    row = lax.broadcasted_iota(jnp.int32, (CHUNK, CHUNK), 0)
    col = lax.broadcasted_iota(jnp.int32, (CHUNK, CHUNK), 1)
    causal = col <= row
    chunks = []
    for c in range(tt // CHUNK):
        groups = []
        for g in range(N_SGU_GROUPS):
            w = jnp.where(causal, w_s_ref[g], 0.0).astype(BF16)
            m = _dot(w, vb[c * CHUNK:(c + 1) * CHUNK, g * SGU_GROUP:(g + 1) * SGU_GROUP])
            groups.append(m + b_st_ref[:, g:g + 1])
        chunks.append(jnp.concatenate(groups, axis=-1))
    mixed = jnp.concatenate(chunks, axis=0)
    out = _dot((u * mixed).astype(BF16), w_out_ref[...])
    _post_mixer(x, out, ada, ln1g_ref[...], ln1b_ref[...], x1_ref, h2_ref)

    @pl.when(t == pl.num_programs(1) - 1)
    def _():
        st_ref[0] = v[tt - CHUNK:, :]


def _full(shape):
    nd = len(shape)
    return pl.BlockSpec(shape, lambda b, t: (0,) * nd)


def _mix_prompt(kernel_fn, name, x, ada_p, weights, state_rows, scratch, seq):
    n = x.shape[0]
    batch = n // seq
    nt = seq // TOK_TILE
    tok = pl.BlockSpec((TOK_TILE, D), lambda b, t: (b * nt + t, 0))
    return pl.pallas_call(
        kernel_fn,
        out_shape=(jax.ShapeDtypeStruct((n, D), F32), jax.ShapeDtypeStruct((n, D), BF16),
                   jax.ShapeDtypeStruct((batch, state_rows, D), F32)),
        grid=(batch, nt),
        in_specs=[tok, pl.BlockSpec((1, 1, 6 * D), lambda b, t: (b, 0, 0))] + [_full(w.shape) for w in weights],
        out_specs=(tok, tok, pl.BlockSpec((1, state_rows, D), lambda b, t: (b, 0, 0))),
        scratch_shapes=scratch,
        compiler_params=_cparams(("arbitrary", "arbitrary")),
        name=name,
    )(x, ada_p, *weights)


def _mix_a_sample_kernel(x_ref, ada_ref, hist_ref, w_in_ref, b_in_ref, w_dw_ref, b_dw_ref, lng_ref, lnb_ref,
                         w_out_ref, ln1g_ref, ln1b_ref, x1_ref, h2_ref, new_ref):
    x = x_ref[...]
    ada = ada_ref[...]
    sh1, sc1 = ada[:, 0:D], ada[:, D:2 * D]
    h = (x * (1.0 + sc1) + sh1).astype(BF16)
    z = _dot(h, w_in_ref[...]) + b_in_ref[...]
    u = z[:, :D] * jax.nn.sigmoid(z[:, D:])
    y = b_dw_ref[...] + w_dw_ref[CONV_A_WIDTH - 1:CONV_A_WIDTH, :] * u
    for k in range(CONV_A_WIDTH - 1):
        y = y + w_dw_ref[k:k + 1, :] * hist_ref[k]
    y = _ln(y, lng_ref[...], lnb_ref[...])
    y = y * jax.nn.sigmoid(y)
    out = _dot(y.astype(BF16), w_out_ref[...])
    _post_mixer(x, out, ada, ln1g_ref[...], ln1b_ref[...], x1_ref, h2_ref)
    new_ref[...] = u


def _mix_b_sample_kernel(x_ref, ada_ref, hist_ref, w_grp_ref, scale_ref, ln1g_ref, ln1b_ref,
                         x1_ref, h2_ref, new_ref):
    x = x_ref[...]
    ada = ada_ref[...]
    sh1, sc1 = ada[:, 0:D], ada[:, D:2 * D]
    h = x * (1.0 + sc1) + sh1
    outs = []
    for gi, w in enumerate(POOL_WINDOWS):
        lo = gi * POOL_GROUP
        s = h[:, lo:lo + POOL_GROUP]
        for j in range(1, w):
            s = s + hist_ref[POOL_HIST - j][:, lo:lo + POOL_GROUP]
        pooled = s / float(w) - h[:, lo:lo + POOL_GROUP]
        outs.append(_dot(pooled.astype(BF16), w_grp_ref[gi].astype(BF16)))
    out = jnp.concatenate(outs, axis=-1) * scale_ref[...]
    _post_mixer(x, out, ada, ln1g_ref[...], ln1b_ref[...], x1_ref, h2_ref)
    new_ref[...] = h


def _mix_c_sample_kernel(x_ref, ada_ref, hist_ref, w_in_ref, w_conv_ref, w_out_ref, ln1g_ref, ln1b_ref,
                         x1_ref, h2_ref, new_ref):
    x = x_ref[...]
    ada = ada_ref[...]
    sh1, sc1 = ada[:, 0:D], ada[:, D:2 * D]
    h = (x * (1.0 + sc1) + sh1).astype(BF16)
    z = _dot(h, w_in_ref[...])
    bg = z[:, :D]
    v = z[:, D:2 * D] * z[:, 2 * D:]
    y = w_conv_ref[CONV_C_WIDTH - 1:CONV_C_WIDTH, :] * v
    for k in range(CONV_C_WIDTH - 1):
        y = y + w_conv_ref[k:k + 1, :] * hist_ref[k]
    out = _dot((bg * y).astype(BF16), w_out_ref[...])
    _post_mixer(x, out, ada, ln1g_ref[...], ln1b_ref[...], x1_ref, h2_ref)
    new_ref[...] = v


def _mix_d_sample_kernel(x_ref, ada_ref, w_in_ref, b_in_ref, lng_ref, lnb_ref, w_s0_ref, b_s0_ref,
                         w_out_ref, ln1g_ref, ln1b_ref, x1_ref, h2_ref, new_ref):
    x = x_ref[...]
    ada = ada_ref[...]
    sh1, sc1 = ada[:, 0:D], ada[:, D:2 * D]
    h = (x * (1.0 + sc1) + sh1).astype(BF16)
    z = _gelu_exact(_dot(h, w_in_ref[...]) + b_in_ref[...])
    u = z[:, :D]
    v = _ln(z[:, D:], lng_ref[...], lnb_ref[...])
    mixed = w_s0_ref[...] * v + b_s0_ref[...]
    out = _dot((u * mixed).astype(BF16), w_out_ref[...])
    _post_mixer(x, out, ada, ln1g_ref[...], ln1b_ref[...], x1_ref, h2_ref)
    new_ref[...] = v


def _mix_sample(kernel_fn, name, x, ada_s, arrays):
    n = x.shape[0]
    return pl.pallas_call(
        kernel_fn,
        out_shape=(jax.ShapeDtypeStruct((n, D), F32), jax.ShapeDtypeStruct((n, D), BF16),
                   jax.ShapeDtypeStruct((n, D), F32)),
        compiler_params=_cparams(),
        name=name,
    )(x, ada_s, *arrays)


def _route_kernel(h_ref, wrt_ref, br_ref, xs_ref, meta_ref, cnt_ref, *, n_valid):
    i = pl.program_id(0)
    tt = h_ref.shape[0]
    cap = xs_ref.shape[0]
    hb = h_ref[...]
    logits = lax.dot_general(wrt_ref[...], hb, (((1,), (1,)), ((), ())), preferred_element_type=F32) + br_ref[...]
    e_iota = lax.broadcasted_iota(jnp.int32, (N_EXPERTS, tt), 0)
    valid = (i * tt + lax.broadcasted_iota(jnp.int32, (1, tt), 1)) < n_valid
    work = logits
    sel, top = [], []
    for _ in range(TOP_K):
        m = jnp.max(work, axis=0, keepdims=True)
        idx = jnp.min(jnp.where(work == m, e_iota, N_EXPERTS), axis=0, keepdims=True)
        oh = e_iota == idx
        sel.append(oh)
        top.append(m)
        work = jnp.where(oh, -jnp.inf, work)
    ex = [jnp.exp(v - top[0]) for v in top]
    denom = ex[0] + ex[1] + ex[2] + ex[3]
    gates = [e / denom for e in ex]

    member = jnp.where(sel[0] | sel[1] | sel[2] | sel[3], 1.0, 0.0)
    member = jnp.where(valid, member, 0.0)
    r_i = lax.broadcasted_iota(jnp.int32, (tt, tt), 0)
    c_i = lax.broadcasted_iota(jnp.int32, (tt, tt), 1)
    before = jnp.where(r_i < c_i, 1.0, 0.0).astype(BF16)
    rank = _dot(member.astype(BF16), before)
    count = jnp.sum(member, axis=1, keepdims=True)
    chunks = jnp.floor((count + float(ROW_CHUNK - 1)) * (1.0 / ROW_CHUNK))
    chunks_b = jnp.broadcast_to(chunks, (N_EXPERTS, LANES))
    er = lax.broadcasted_iota(jnp.int32, (N_EXPERTS, N_EXPERTS), 0)
    ec = lax.broadcasted_iota(jnp.int32, (N_EXPERTS, N_EXPERTS), 1)
    lower = jnp.where(ec < er, 1.0, 0.0).astype(BF16)
    chunk_off = _dot(lower, chunks_b.astype(BF16))
    base = chunk_off[:, 0:1] * float(ROW_CHUNK)
    slot = base + rank
    pos = [jnp.where(valid, jnp.sum(jnp.where(s, slot, 0.0), axis=0, keepdims=True), -1.0) for s in sel]

    row = lax.broadcasted_iota(jnp.int32, (cap, tt), 0).astype(F32)
    onehot = jnp.where(row == pos[0], 1.0, 0.0)
    for k in range(1, TOP_K):
        onehot = jnp.where(row == pos[k], 1.0, onehot)
    xs_ref[...] = _dot(onehot.astype(BF16), hb).astype(BF16)

    meta = jnp.concatenate(pos + gates + [jnp.zeros((LANES - 2 * TOP_K, tt), F32)], axis=0)
    meta_ref[...] = meta.T
    cnt_ref[0] = chunks_b.astype(jnp.int32)


def _route(h2, w_router_t, b_router_col, n_valid):
    n_pad = h2.shape[0]
    nt = n_pad // TOK_TILE
    cap = _tile_capacity(TOK_TILE)
    return pl.pallas_call(
        functools.partial(_route_kernel, n_valid=n_valid),
        out_shape=(jax.ShapeDtypeStruct((nt * cap, D), BF16),
                   jax.ShapeDtypeStruct((n_pad, LANES), F32),
                   jax.ShapeDtypeStruct((nt, N_EXPERTS, LANES), jnp.int32)),
        grid=(nt,),
        in_specs=[pl.BlockSpec((TOK_TILE, D), lambda i: (i, 0)),
                  pl.BlockSpec((N_EXPERTS, D), lambda i: (0, 0)),
                  pl.BlockSpec((N_EXPERTS, 1), lambda i: (0, 0))],
        out_specs=(pl.BlockSpec((cap, D), lambda i: (i, 0)),
                   pl.BlockSpec((TOK_TILE, LANES), lambda i: (i, 0)),
                   pl.BlockSpec((1, N_EXPERTS, LANES), lambda i: (i, 0, 0))),
        compiler_params=_cparams(("arbitrary",)),
        name="route_dispatch",
    )(h2, w_router_t, b_router_col)


def _chunk_tables(cnt, nt, cap, nb_max):
    cap_chunks = cap // ROW_CHUNK
    off_in_tile = jnp.cumsum(cnt, axis=1) - cnt
    pre_incl = jnp.cumsum(cnt, axis=0)
    pre = pre_incl - cnt
    n_e = pre_incl[-1]
    nblk = (n_e + BLOCK_CHUNKS - 1) // BLOCK_CHUNKS
    blk_end = jnp.cumsum(nblk)
    blk_start = blk_end - nblk
    nb = blk_end[-1]
    b = jnp.arange(nb_max, dtype=jnp.int32)
    block_e = jnp.minimum(jnp.sum((blk_end[None, :] <= b[:, None]).astype(jnp.int32), axis=1), N_EXPERTS - 1)
    dchunk = jnp.arange(nb_max * BLOCK_CHUNKS, dtype=jnp.int32)
    blk = dchunk // BLOCK_CHUNKS
    j = dchunk % BLOCK_CHUNKS
    e_d = block_e[blk]
    local = dchunk - blk_start[e_d] * BLOCK_CHUNKS
    valid = (local < n_e[e_d]) & (blk < nb)
    tile = jnp.minimum(jnp.sum((pre_incl[:, e_d].T <= local[:, None]).astype(jnp.int32), axis=1), nt - 1)
    src = tile * cap_chunks + off_in_tile[tile, e_d] + (local - pre[tile, e_d])
    trash = nt * cap_chunks + (blk % 2) * BLOCK_CHUNKS + j
    src_tab = jnp.where(valid, src, 0).astype(jnp.int32)
    dst_tab = jnp.where(valid, src, trash).astype(jnp.int32)
    return block_e.astype(jnp.int32), nb.reshape(1).astype(jnp.int32), src_tab, dst_tab


def _expert_kernel(be_ref, nb_ref, src_ref, dst_ref, xs_hbm, w_in_ref, b_in_ref, w_out_ref, b_out_ref,
                   ys_in_hbm, ys_hbm, xbuf, ybuf, sem_in, sem_out):
    del be_ref, ys_in_hbm
    b = pl.program_id(0)
    nb = nb_ref[0]
    slot = lax.rem(b, 2)

    def gather(blk, s):
        for j in range(BLOCK_CHUNKS):
            c = src_ref[blk * BLOCK_CHUNKS + j]
            pltpu.make_async_copy(xs_hbm.at[pl.ds(pl.multiple_of(c * ROW_CHUNK, ROW_CHUNK), ROW_CHUNK)],
                                  xbuf.at[s, pl.ds(j * ROW_CHUNK, ROW_CHUNK)], sem_in.at[s]).start()

    def gather_wait(s):
        pltpu.make_async_copy(xs_hbm.at[pl.ds(0, BLOCK_ROWS)], xbuf.at[s], sem_in.at[s]).wait()

    def scatter(blk, s):
        for j in range(BLOCK_CHUNKS):
            c = dst_ref[blk * BLOCK_CHUNKS + j]
            pltpu.make_async_copy(ybuf.at[s, pl.ds(j * ROW_CHUNK, ROW_CHUNK)],
                                  ys_hbm.at[pl.ds(pl.multiple_of(c * ROW_CHUNK, ROW_CHUNK), ROW_CHUNK)],
                                  sem_out.at[s]).start()

    def scatter_wait(s):
        pltpu.make_async_copy(ybuf.at[s], ys_hbm.at[pl.ds(0, BLOCK_ROWS)], sem_out.at[s]).wait()

    @pl.when(b == 0)
    def _():
        gather(0, 0)

    @pl.when(b < nb)
    def _():
        gather_wait(slot)

        @pl.when(b + 1 < nb)
        def _():
            gather(b + 1, 1 - slot)

        @pl.when(b >= 2)
        def _():
            scatter_wait(slot)

        x = xbuf[slot]
        h = _dot(x, w_in_ref[0]) + b_in_ref[0]
        g = jnp.minimum(h[:, :D], SWIGLU_LIMIT)
        u = jnp.clip(h[:, D:], -SWIGLU_LIMIT, SWIGLU_LIMIT)
        a = (u + 1.0) * (g * jax.nn.sigmoid(SWIGLU_ALPHA * g))
        y = _dot(a.astype(BF16), w_out_ref[0]) + b_out_ref[0]
        ybuf[slot] = y.astype(BF16)
        scatter(b, slot)

        @pl.when(b == nb - 1)
        def _():
            scatter_wait(slot)

            @pl.when(b >= 1)
            def _():
                scatter_wait(1 - slot)


def _experts(xs, tables, w_in, b_in, w_out, b_out, nb_max):
    block_e, nb, src_tab, dst_tab = tables
    rows = xs.shape[0] + 2 * BLOCK_ROWS
    ys0 = jnp.zeros((rows, D), BF16)
    wmap = lambda b, be, nbr, s, d: (be[b], 0, 0)
    grid_spec = pltpu.PrefetchScalarGridSpec(
        num_scalar_prefetch=4,
        grid=(nb_max,),
        in_specs=[pl.BlockSpec(memory_space=pl.ANY),
                  pl.BlockSpec((1, D, 2 * D), wmap),
                  pl.BlockSpec((1, 1, 2 * D), wmap),
                  pl.BlockSpec((1, D, D), wmap),
                  pl.BlockSpec((1, 1, D), wmap),
                  pl.BlockSpec(memory_space=pl.ANY)],
        out_specs=pl.BlockSpec(memory_space=pl.ANY),
        scratch_shapes=[pltpu.VMEM((2, BLOCK_ROWS, D), BF16), pltpu.VMEM((2, BLOCK_ROWS, D), BF16),
                        pltpu.SemaphoreType.DMA((2,)), pltpu.SemaphoreType.DMA((2,))],
    )
    return pl.pallas_call(
        _expert_kernel,
        out_shape=jax.ShapeDtypeStruct((rows, D), BF16),
        grid_spec=grid_spec,
        input_output_aliases={9: 0},
        compiler_params=_cparams(("arbitrary",)),
        name="experts",
    )(block_e, nb, src_tab, dst_tab, xs, w_in, b_in, w_out, b_out, ys0)


def _combine_kernel(ys_ref, meta_ref, x1_ref, ada_ref, lng_ref, lnb_ref, o_ref):
    tt = x1_ref.shape[0]
    cap = ys_ref.shape[0]
    meta = meta_ref[...]
    col = lax.broadcasted_iota(jnp.int32, (tt, cap), 1).astype(F32)
    gmat = jnp.zeros((tt, cap), F32)
    for k in range(TOP_K):
        gmat = jnp.where(col == meta[:, k:k + 1], meta[:, TOP_K + k:TOP_K + k + 1], gmat)
    f = _dot(gmat.astype(BF16), ys_ref[...])
    ada = ada_ref[0] if len(ada_ref.shape) == 3 else ada_ref[...]
    g2 = ada[:, 5 * D:6 * D]
    o_ref[...] = _ln(ALPHA * x1_ref[...] + (1.0 + g2) * f, lng_ref[...], lnb_ref[...])


def _combine_prompt(ys, meta, x1, ada_p, ln_g, ln_b, seq):
    n = x1.shape[0]
    cap = _tile_capacity(TOK_TILE)
    per_seq = seq // TOK_TILE
    return pl.pallas_call(
        _combine_kernel,
        out_shape=jax.ShapeDtypeStruct((n, D), F32),
        grid=(n // TOK_TILE,),
        in_specs=[pl.BlockSpec((cap, D), lambda i: (i, 0)),
                  pl.BlockSpec((TOK_TILE, LANES), lambda i: (i, 0)),
                  pl.BlockSpec((TOK_TILE, D), lambda i: (i, 0)),
                  pl.BlockSpec((1, 1, 6 * D), lambda i: (i // per_seq, 0, 0)),
                  pl.BlockSpec((1, D), lambda i: (0, 0)),
                  pl.BlockSpec((1, D), lambda i: (0, 0))],
        out_specs=pl.BlockSpec((TOK_TILE, D), lambda i: (i, 0)),
        compiler_params=_cparams(("arbitrary",)),
        name="combine_prompt",
    )(ys, meta, x1, ada_p, ln_g, ln_b)


def _combine_sample(ys, meta, x1, ada_s, ln_g, ln_b, tile):
    n = x1.shape[0]
    cap = _tile_capacity(TOK_TILE)
    return pl.pallas_call(
        _combine_kernel,
        out_shape=jax.ShapeDtypeStruct((n, D), F32),
        grid=(1,),
        in_specs=[pl.BlockSpec((cap, D), lambda i: (tile, 0)),
                  pl.BlockSpec((n, LANES), lambda i: (tile * (TOK_TILE // n), 0)),
                  pl.BlockSpec((n, D), lambda i: (0, 0)),
                  pl.BlockSpec((n, 6 * D), lambda i: (0, 0)),
                  pl.BlockSpec((1, D), lambda i: (0, 0)),
                  pl.BlockSpec((1, D), lambda i: (0, 0))],
        out_specs=pl.BlockSpec((n, D), lambda i: (0, 0)),
        compiler_params=_cparams(("arbitrary",)),
        name="combine_sample",
    )(ys, meta, x1, ada_s, ln_g, ln_b)


def _moe_and_norm(h2_p, h2_s, x1_p, x1_s, ada_p, ada_s, ln_g, ln_b, w_router, b_router,
                  w_in, b_in, w_out, b_out, seq):
    n_p, n_s = h2_p.shape[0], h2_s.shape[0]
    n_valid = n_p + n_s
    nt = -(-n_valid // TOK_TILE)
    n_pad = nt * TOK_TILE
    cap = _tile_capacity(TOK_TILE)
    h2 = jnp.concatenate([h2_p, h2_s, jnp.zeros((n_pad - n_valid, D), BF16)], axis=0)
    xs, meta, cnt = _route(h2, w_router.T.astype(BF16), b_router.reshape(N_EXPERTS, 1), n_valid)
    total_chunks_max = (TOP_K * n_valid + nt * N_EXPERTS * (ROW_CHUNK - 1)) // ROW_CHUNK
    nb_max = -(-total_chunks_max // BLOCK_CHUNKS) + N_EXPERTS
    tables = _chunk_tables(cnt[:, :, 0], nt, cap, nb_max)
    ys = _experts(xs, tables, w_in.astype(BF16), b_in.reshape(N_EXPERTS, 1, 2 * D),
                  w_out.astype(BF16), b_out.reshape(N_EXPERTS, 1, D), nb_max)
    x2_p = _combine_prompt(ys, meta, x1_p, ada_p, ln_g, ln_b, seq)
    x2_s = _combine_sample(ys, meta, x1_s, ada_s, ln_g, ln_b, n_p // TOK_TILE)
    return x2_p, x2_s


def _row(v):
    return v.reshape(1, -1)


def kernel(x_prompt, x_sample, state_conv_a, state_pool_b, state_conv_c, c_prompt, c_sample,
           w_ada, b_ada, ln1_g, ln1_b, ln2_g, ln2_b,
           a_w_in, a_b_in, a_w_dw, a_b_dw, a_ln_g, a_ln_b, a_w_out,
           b_w_grp, b_scale, c_w_in, c_w_conv, c_w_out,
           d_w_in, d_b_in, d_ln_g, d_ln_b, d_w_s, d_b_s, d_w_out,
           w_router, b_router, w_moe_in, b_moe_in, w_moe_out, b_moe_out):
    bp, seq, _ = x_prompt.shape
    bs = x_sample.shape[0]
    assert seq % TOK_TILE == 0 and (bp * seq) % TOK_TILE == 0 and TOK_TILE % bs == 0
    assert x_sample.shape[1] == 1 and w_ada.shape[0] == DEPTH == 4

    ada = _ada(jnp.concatenate([c_prompt, c_sample], axis=0), w_ada, b_ada)
    xp = x_prompt.reshape(bp * seq, D)
    xs = x_sample.reshape(bs, D)
    states = {}

    for i in range(DEPTH):
        ada_p = ada[i, :bp].reshape(bp, 1, 6 * D)
        ada_s = ada[i, bp:]
        ln1 = [_row(ln1_g[i]), _row(ln1_b[i])]
        if i == 0:
            wts = [a_w_in[0].astype(BF16), _row(a_b_in[0]), a_w_dw[0], _row(a_b_dw[0]), _row(a_ln_g[0]),
                   _row(a_ln_b[0]), a_w_out[0].astype(BF16)] + ln1
            x1_p, h2_p, st = _mix_prompt(_mix_a_prompt_kernel, "mix_a_prompt", xp, ada_p, wts, A_HALO,
                                         [pltpu.VMEM((A_HALO + TOK_TILE, D), F32)], seq)
            states["a_p"] = st[:, A_HALO - (CONV_A_WIDTH - 1):][None]
            hist = jnp.transpose(state_conv_a[0], (1, 0, 2))
            x1_s, h2_s, new = _mix_sample(_mix_a_sample_kernel, "mix_a_sample", xs, ada_s, [hist] + wts)
            states["a_s"] = jnp.concatenate([state_conv_a[0][:, 1:], new[:, None]], axis=1)[None]
        elif i == 1:
            wts = [b_w_grp[0], _row(b_scale[0])] + ln1
            x1_p, h2_p, st = _mix_prompt(_mix_b_prompt_kernel, "mix_b_prompt", xp, ada_p, wts, B_HALO,
                                         [pltpu.VMEM((B_HALO + TOK_TILE, D), F32)], seq)
            states["b_p"] = st[:, B_HALO - POOL_HIST:][None]
            hist = jnp.transpose(state_pool_b[0], (1, 0, 2))
            x1_s, h2_s, new = _mix_sample(_mix_b_sample_kernel, "mix_b_sample", xs, ada_s, [hist] + wts)
            states["b_s"] = jnp.concatenate([state_pool_b[0][:, 1:], new[:, None]], axis=1)[None]
        elif i == 2:
            wts = [c_w_in[0].astype(BF16), c_w_conv[0], c_w_out[0].astype(BF16)] + ln1
            x1_p, h2_p, st = _mix_prompt(_mix_c_prompt_kernel, "mix_c_prompt", xp, ada_p, wts, C_HALO,
                                         [pltpu.VMEM((C_HALO + TOK_TILE, D), F32)], seq)
            states["c_p"] = st[:, C_HALO - (CONV_C_WIDTH - 1):][None]
            hist = jnp.transpose(state_conv_c[0], (1, 0, 2))
            x1_s, h2_s, new = _mix_sample(_mix_c_sample_kernel, "mix_c_sample", xs, ada_s, [hist] + wts)
            states["c_s"] = jnp.concatenate([state_conv_c[0][:, 1:], new[:, None]], axis=1)[None]
        else:
            common = [d_w_in[0].astype(BF16), _row(d_b_in[0]), _row(d_ln_g[0]), _row(d_ln_b[0])]
            wts = common + [d_w_s[0], d_b_s[0].T, d_w_out[0].astype(BF16)] + ln1
            x1_p, h2_p, st = _mix_prompt(_mix_d_prompt_kernel, "mix_d_prompt", xp, ada_p, wts, CHUNK, [], seq)
            states["d_p"] = st[None]
            w_s0 = jnp.repeat(d_w_s[0][:, 0, 0], SGU_GROUP).reshape(1, D)
            b_s0 = jnp.repeat(d_b_s[0][:, 0], SGU_GROUP).reshape(1, D)
            wts_s = common + [w_s0, b_s0, d_w_out[0].astype(BF16)] + ln1
            x1_s, h2_s, new = _mix_sample(_mix_d_sample_kernel, "mix_d_sample", xs, ada_s, wts_s)
            states["d_s"] = new[:, None][None]

        xp, xs = _moe_and_norm(h2_p, h2_s, x1_p, x1_s, ada_p, ada_s, _row(ln2_g[i]), _row(ln2_b[i]),
                               w_router[i], b_router[i], w_moe_in[i], b_moe_in[i], w_moe_out[i], b_moe_out[i], seq)

    return (xp.reshape(bp, seq, D), xs.reshape(bs, 1, D),
            states["a_p"], states["a_s"], states["b_p"], states["b_s"],
            states["c_p"], states["c_s"], states["d_p"], states["d_s"])
```

```python
import functools

import jax
import jax.numpy as jnp
from jax import lax
from jax.experimental import pallas as pl
from jax.experimental.pallas import tpu as pltpu

F32 = jnp.float32
BF16 = jnp.bfloat16

D = 1024
DEPTH = 4
N_EXPERTS = 32
TOP_K = 4
CONV_A_WIDTH = 31
POOL_WINDOWS = (2, 4, 8, 16)
POOL_GROUP = D // 4
POOL_HIST = 15
CONV_C_WIDTH = 3
CHUNK = 128
N_SGU_GROUPS = 4
SGU_GROUP = D // N_SGU_GROUPS
SWIGLU_LIMIT = 7.0
SWIGLU_ALPHA = 1.702
ALPHA = (2 * DEPTH) ** 0.25
LN_EPS = 1e-5

LANES = 128
SUBLANES_BF16 = 16
VMEM_LIMIT = 56 * 1024 * 1024

TOK_TILE = 512
ROW_CHUNK = SUBLANES_BF16
BLOCK_CHUNKS = 32
BLOCK_ROWS = BLOCK_CHUNKS * ROW_CHUNK


def _tile_capacity(tt):
    cap = TOP_K * tt + N_EXPERTS * (ROW_CHUNK - 1)
    return -(-cap // 256) * 256


def _cparams(sem=None):
    return pltpu.CompilerParams(dimension_semantics=sem, vmem_limit_bytes=VMEM_LIMIT)


def _dot(a, b):
    return jnp.dot(a, b, preferred_element_type=F32)


def _dot_split(a, b):
    hi = a.astype(BF16)
    lo = (a - hi.astype(F32)).astype(BF16)
    return _dot(hi, b) + _dot(lo, b)


def _ln(x, g, b):
    mu = jnp.mean(x, axis=-1, keepdims=True)
    xc = x - mu
    var = jnp.mean(xc * xc, axis=-1, keepdims=True)
    return xc * lax.rsqrt(var + LN_EPS) * g + b


def _split_ada(ada):
    return [ada[:, i * D:(i + 1) * D] for i in range(6)]


def _post_mixer(x, out, ada, ln_g, ln_b, wr_ref, br_ref, x1_ref, h2_ref, lg_ref):
    _, _, g1, sh2, sc2, _ = _split_ada(ada)
    x1 = _ln(ALPHA * x + (1.0 + g1) * out, ln_g, ln_b)
    x1_ref[...] = x1
    h2 = x1 * (1.0 + sc2) + sh2
    hi = h2.astype(BF16)
    h2_ref[...] = hi
    lo = (h2 - hi.astype(F32)).astype(BF16)
    wr = wr_ref[...]
    w_hi = wr.astype(BF16)
    w_lo = (wr - w_hi.astype(F32)).astype(BF16)
    lg_ref[...] = _dot(hi, w_hi) + _dot(lo, w_hi) + _dot(hi, w_lo) + br_ref[...]


def _ada_kernel(c_ref, w_ref, b_ref, o_ref):
    c = c_ref[...]
    sc = (c * jax.nn.sigmoid(c)).astype(BF16)
    o_ref[0] = _dot(sc, w_ref[0].astype(BF16)) + b_ref[0]


def _ada(c_all, w_ada, b_ada):
    rows = c_all.shape[0]
    return pl.pallas_call(
        _ada_kernel,
        out_shape=jax.ShapeDtypeStruct((DEPTH, rows, 6 * D), F32),
        grid=(DEPTH, 6),
        in_specs=[
            pl.BlockSpec((rows, D), lambda i, j: (0, 0)),
            pl.BlockSpec((1, D, D), lambda i, j: (i, 0, j)),
            pl.BlockSpec((1, 1, D), lambda i, j: (i, 0, j)),
        ],
        out_specs=pl.BlockSpec((1, rows, D), lambda i, j: (i, 0, j)),
        compiler_params=_cparams(("arbitrary", "arbitrary")),
        name="ada",
    )(c_all, w_ada, b_ada.reshape(DEPTH, 1, 6 * D))


A_HALO = 32
B_HALO = 16
C_HALO = 8


def _mix_a_prompt_kernel(x_ref, ada_ref, w_in_ref, b_in_ref, w_dw_ref, b_dw_ref, lng_ref, lnb_ref,
                         w_out_ref, ln1g_ref, ln1b_ref, wr_ref, br_ref, x1_ref, h2_ref, lg_ref, st_ref, u_scr):
    t = pl.program_id(1)
    tt = x_ref.shape[0]
    x = x_ref[...]
    ada = ada_ref[0]
    sh1, sc1 = ada[:, 0:D], ada[:, D:2 * D]
    h = (x * (1.0 + sc1) + sh1).astype(BF16)
    z = _dot(h, w_in_ref[...]) + b_in_ref[...]
    u = z[:, :D] * jax.nn.sigmoid(z[:, D:])

    @pl.when(t == 0)
    def _():
        u_scr[0:A_HALO, :] = jnp.zeros((A_HALO, D), F32)

    u_scr[A_HALO:A_HALO + tt, :] = u
    first = A_HALO - (CONV_A_WIDTH - 1)
    y = jnp.zeros((tt, D), F32) + b_dw_ref[...]
    for k in range(CONV_A_WIDTH):
        y = y + w_dw_ref[k:k + 1, :] * u_scr[first + k:first + k + tt, :]
    y = _ln(y, lng_ref[...], lnb_ref[...])
    y = y * jax.nn.sigmoid(y)
    out = _dot(y.astype(BF16), w_out_ref[...])
    _post_mixer(x, out, ada, ln1g_ref[...], ln1b_ref[...], wr_ref, br_ref, x1_ref, h2_ref, lg_ref)
    tail = u_scr[tt:tt + A_HALO, :]
    u_scr[0:A_HALO, :] = tail

    @pl.when(t == pl.num_programs(1) - 1)
    def _():
        st_ref[0] = tail


def _mix_b_prompt_kernel(x_ref, ada_ref, w_grp_ref, scale_ref, ln1g_ref, ln1b_ref,
                         wr_ref, br_ref, x1_ref, h2_ref, lg_ref, st_ref, h_scr):
    t = pl.program_id(1)
    tt = x_ref.shape[0]
    x = x_ref[...]
    ada = ada_ref[0]
    sh1, sc1 = ada[:, 0:D], ada[:, D:2 * D]
    h = x * (1.0 + sc1) + sh1

    @pl.when(t == 0)
    def _():
        h_scr[0:B_HALO, :] = jnp.zeros((B_HALO, D), F32)

    h_scr[B_HALO:B_HALO + tt, :] = h
    pos = (t * tt + lax.broadcasted_iota(jnp.int32, (tt, POOL_GROUP), 0)).astype(F32)
    outs = []
    for gi, w in enumerate(POOL_WINDOWS):
        lo = gi * POOL_GROUP
        s = h_scr[B_HALO:B_HALO + tt, lo:lo + POOL_GROUP]
        for j in range(1, w):
            s = s + h_scr[B_HALO - j:B_HALO - j + tt, lo:lo + POOL_GROUP]
        cnt = jnp.minimum(float(w), pos + 1.0)
        pooled = s / cnt - h[:, lo:lo + POOL_GROUP]
        outs.append(_dot_split(pooled, w_grp_ref[gi].astype(BF16)))
    out = jnp.concatenate(outs, axis=-1) * scale_ref[...]
    _post_mixer(x, out, ada, ln1g_ref[...], ln1b_ref[...], wr_ref, br_ref, x1_ref, h2_ref, lg_ref)
    tail = h_scr[tt:tt + B_HALO, :]
    h_scr[0:B_HALO, :] = tail

    @pl.when(t == pl.num_programs(1) - 1)
    def _():
        st_ref[0] = tail


def _mix_c_prompt_kernel(x_ref, ada_ref, w_in_ref, w_conv_ref, w_out_ref, ln1g_ref, ln1b_ref,
                         wr_ref, br_ref, x1_ref, h2_ref, lg_ref, st_ref, v_scr):
    t = pl.program_id(1)
    tt = x_ref.shape[0]
    x = x_ref[...]
    ada = ada_ref[0]
    sh1, sc1 = ada[:, 0:D], ada[:, D:2 * D]
    h = (x * (1.0 + sc1) + sh1).astype(BF16)
    z = _dot(h, w_in_ref[...])
    bg = z[:, :D]
    v = z[:, D:2 * D] * z[:, 2 * D:]

    @pl.when(t == 0)
    def _():
        v_scr[0:C_HALO, :] = jnp.zeros((C_HALO, D), F32)

    v_scr[C_HALO:C_HALO + tt, :] = v
    first = C_HALO - (CONV_C_WIDTH - 1)
    y = w_conv_ref[CONV_C_WIDTH - 1:CONV_C_WIDTH, :] * v
    for k in range(CONV_C_WIDTH - 1):
        y = y + w_conv_ref[k:k + 1, :] * v_scr[first + k:first + k + tt, :]
    out = _dot((bg * y).astype(BF16), w_out_ref[...])
    _post_mixer(x, out, ada, ln1g_ref[...], ln1b_ref[...], wr_ref, br_ref, x1_ref, h2_ref, lg_ref)
    tail = v_scr[tt:tt + C_HALO, :]
    v_scr[0:C_HALO, :] = tail

    @pl.when(t == pl.num_programs(1) - 1)
    def _():
        st_ref[0] = tail


def _gelu_exact(x):
    return 0.5 * x * (1.0 + lax.erf(x * (2.0 ** -0.5)))


def _mix_d_prompt_kernel(x_ref, ada_ref, w_in_ref, b_in_ref, lng_ref, lnb_ref, w_s_ref, b_st_ref,
                         w_out_ref, ln1g_ref, ln1b_ref, wr_ref, br_ref, x1_ref, h2_ref, lg_ref, st_ref):
    t = pl.program_id(1)
    tt = x_ref.shape[0]
    x = x_ref[...]
    ada = ada_ref[0]
    sh1, sc1 = ada[:, 0:D], ada[:, D:2 * D]
    h = (x * (1.0 + sc1) + sh1).astype(BF16)
    z = _gelu_exact(_dot(h, w_in_ref[...]) + b_in_ref[...])
    u = z[:, :D]
    v = _ln(z[:, D:], lng_ref[...], lnb_ref[...])
    vb = v.astype(BF16)
    row = lax.broadcasted_iota(jnp.int32, (CHUNK, CHUNK), 0)
    col = lax.broadcasted_iota(jnp.int32, (CHUNK, CHUNK), 1)
    causal = col <= row
    chunks = []
    for c in range(tt // CHUNK):
        groups = []
        for g in range(N_SGU_GROUPS):
            w = jnp.where(causal, w_s_ref[g], 0.0).astype(BF16)
            m = _dot(w, vb[c * CHUNK:(c + 1) * CHUNK, g * SGU_GROUP:(g + 1) * SGU_GROUP])
            groups.append(m + b_st_ref[:, g:g + 1])
        chunks.append(jnp.concatenate(groups, axis=-1))
    mixed = jnp.concatenate(chunks, axis=0)
    out = _dot((u * mixed).astype(BF16), w_out_ref[...])
    _post_mixer(x, out, ada, ln1g_ref[...], ln1b_ref[...], wr_ref, br_ref, x1_ref, h2_ref, lg_ref)

    @pl.when(t == pl.num_programs(1) - 1)
    def _():
        st_ref[0] = v[tt - CHUNK:, :]


def _full(shape):
    nd = len(shape)
    return pl.BlockSpec(shape, lambda b, t: (0,) * nd)


def _mix_prompt(kernel_fn, name, x, ada_p, weights, state_rows, scratch, seq):
    n = x.shape[0]
    batch = n // seq
    nt = seq // TOK_TILE
    tok = pl.BlockSpec((TOK_TILE, D), lambda b, t: (b * nt + t, 0))
    lgt = pl.BlockSpec((TOK_TILE, LANES), lambda b, t: (b * nt + t, 0))
    return pl.pallas_call(
        kernel_fn,
        out_shape=(jax.ShapeDtypeStruct((n, D), F32), jax.ShapeDtypeStruct((n, D), BF16),
                   jax.ShapeDtypeStruct((n, LANES), F32), jax.ShapeDtypeStruct((batch, state_rows, D), F32)),
        grid=(batch, nt),
        in_specs=[tok, pl.BlockSpec((1, 1, 6 * D), lambda b, t: (b, 0, 0))] + [_full(w.shape) for w in weights],
        out_specs=(tok, tok, lgt, pl.BlockSpec((1, state_rows, D), lambda b, t: (b, 0, 0))),
        scratch_shapes=scratch,
        compiler_params=_cparams(("arbitrary", "arbitrary")),
        name=name,
    )(x, ada_p, *weights)


def _mix_a_sample_kernel(x_ref, ada_ref, hist_ref, w_in_ref, b_in_ref, w_dw_ref, b_dw_ref, lng_ref, lnb_ref,
                         w_out_ref, ln1g_ref, ln1b_ref, wr_ref, br_ref, x1_ref, h2_ref, lg_ref, new_ref):
    x = x_ref[...]
    ada = ada_ref[...]
    sh1, sc1 = ada[:, 0:D], ada[:, D:2 * D]
    h = (x * (1.0 + sc1) + sh1).astype(BF16)
    z = _dot(h, w_in_ref[...]) + b_in_ref[...]
    u = z[:, :D] * jax.nn.sigmoid(z[:, D:])
    y = b_dw_ref[...] + w_dw_ref[CONV_A_WIDTH - 1:CONV_A_WIDTH, :] * u
    for k in range(CONV_A_WIDTH - 1):
        y = y + w_dw_ref[k:k + 1, :] * hist_ref[k]
    y = _ln(y, lng_ref[...], lnb_ref[...])
    y = y * jax.nn.sigmoid(y)
    out = _dot(y.astype(BF16), w_out_ref[...])
    _post_mixer(x, out, ada, ln1g_ref[...], ln1b_ref[...], wr_ref, br_ref, x1_ref, h2_ref, lg_ref)
    new_ref[...] = u


def _mix_b_sample_kernel(x_ref, ada_ref, hist_ref, w_grp_ref, scale_ref, ln1g_ref, ln1b_ref,
                         wr_ref, br_ref, x1_ref, h2_ref, lg_ref, new_ref):
    x = x_ref[...]
    ada = ada_ref[...]
    sh1, sc1 = ada[:, 0:D], ada[:, D:2 * D]
    h = x * (1.0 + sc1) + sh1
    outs = []
    for gi, w in enumerate(POOL_WINDOWS):
        lo = gi * POOL_GROUP
        s = h[:, lo:lo + POOL_GROUP]
        for j in range(1, w):
            s = s + hist_ref[POOL_HIST - j][:, lo:lo + POOL_GROUP]
        pooled = s / float(w) - h[:, lo:lo + POOL_GROUP]
        outs.append(_dot_split(pooled, w_grp_ref[gi].astype(BF16)))
    out = jnp.concatenate(outs, axis=-1) * scale_ref[...]
    _post_mixer(x, out, ada, ln1g_ref[...], ln1b_ref[...], wr_ref, br_ref, x1_ref, h2_ref, lg_ref)
    new_ref[...] = h


def _mix_c_sample_kernel(x_ref, ada_ref, hist_ref, w_in_ref, w_conv_ref, w_out_ref, ln1g_ref, ln1b_ref,
                         wr_ref, br_ref, x1_ref, h2_ref, lg_ref, new_ref):
    x = x_ref[...]
    ada = ada_ref[...]
    sh1, sc1 = ada[:, 0:D], ada[:, D:2 * D]
    h = (x * (1.0 + sc1) + sh1).astype(BF16)
    z = _dot(h, w_in_ref[...])
    bg = z[:, :D]
    v = z[:, D:2 * D] * z[:, 2 * D:]
    y = w_conv_ref[CONV_C_WIDTH - 1:CONV_C_WIDTH, :] * v
    for k in range(CONV_C_WIDTH - 1):
        y = y + w_conv_ref[k:k + 1, :] * hist_ref[k]
    out = _dot((bg * y).astype(BF16), w_out_ref[...])
    _post_mixer(x, out, ada, ln1g_ref[...], ln1b_ref[...], wr_ref, br_ref, x1_ref, h2_ref, lg_ref)
    new_ref[...] = v


def _mix_d_sample_kernel(x_ref, ada_ref, w_in_ref, b_in_ref, lng_ref, lnb_ref, w_s0_ref, b_s0_ref,
                         w_out_ref, ln1g_ref, ln1b_ref, wr_ref, br_ref, x1_ref, h2_ref, lg_ref, new_ref):
    x = x_ref[...]
    ada = ada_ref[...]
    sh1, sc1 = ada[:, 0:D], ada[:, D:2 * D]
    h = (x * (1.0 + sc1) + sh1).astype(BF16)
    z = _gelu_exact(_dot(h, w_in_ref[...]) + b_in_ref[...])
    u = z[:, :D]
    v = _ln(z[:, D:], lng_ref[...], lnb_ref[...])
    mixed = w_s0_ref[...] * v + b_s0_ref[...]
    out = _dot((u * mixed).astype(BF16), w_out_ref[...])
    _post_mixer(x, out, ada, ln1g_ref[...], ln1b_ref[...], wr_ref, br_ref, x1_ref, h2_ref, lg_ref)
    new_ref[...] = v


def _mix_sample(kernel_fn, name, x, ada_s, arrays):
    n = x.shape[0]
    return pl.pallas_call(
        kernel_fn,
        out_shape=(jax.ShapeDtypeStruct((n, D), F32), jax.ShapeDtypeStruct((n, D), BF16),
                   jax.ShapeDtypeStruct((n, LANES), F32), jax.ShapeDtypeStruct((n, D), F32)),
        compiler_params=_cparams(),
        name=name,
    )(x, ada_s, *arrays)


def _route_kernel(h_ref, lg_ref, xs_ref, meta_ref, cnt_ref, *, n_valid):
    i = pl.program_id(0)
    tt = h_ref.shape[0]
    cap = xs_ref.shape[0]
    hb = h_ref[...]
    logits = lg_ref[...].T[:N_EXPERTS, :]
    e_iota = lax.broadcasted_iota(jnp.int32, (N_EXPERTS, tt), 0)
    valid = (i * tt + lax.broadcasted_iota(jnp.int32, (1, tt), 1)) < n_valid
    work = logits
    sel, top = [], []
    for _ in range(TOP_K):
        m = jnp.max(work, axis=0, keepdims=True)
        idx = jnp.min(jnp.where(work == m, e_iota, N_EXPERTS), axis=0, keepdims=True)
        oh = e_iota == idx
        sel.append(oh)
        top.append(m)
        work = jnp.where(oh, -jnp.inf, work)
    ex = [jnp.exp(v - top[0]) for v in top]
    denom = ex[0] + ex[1] + ex[2] + ex[3]
    gates = [e / denom for e in ex]

    member = jnp.where(sel[0] | sel[1] | sel[2] | sel[3], 1.0, 0.0)
    member = jnp.where(valid, member, 0.0)
    r_i = lax.broadcasted_iota(jnp.int32, (tt, tt), 0)
    c_i = lax.broadcasted_iota(jnp.int32, (tt, tt), 1)
    before = jnp.where(r_i < c_i, 1.0, 0.0).astype(BF16)
    rank = _dot(member.astype(BF16), before)
    count = jnp.sum(member, axis=1, keepdims=True)
    chunks = jnp.floor((count + float(ROW_CHUNK - 1)) * (1.0 / ROW_CHUNK))
    chunks_b = jnp.broadcast_to(chunks, (N_EXPERTS, LANES))
    er = lax.broadcasted_iota(jnp.int32, (N_EXPERTS, N_EXPERTS), 0)
    ec = lax.broadcasted_iota(jnp.int32, (N_EXPERTS, N_EXPERTS), 1)
    lower = jnp.where(ec < er, 1.0, 0.0).astype(BF16)
    chunk_off = _dot(lower, chunks_b.astype(BF16))
    base = chunk_off[:, 0:1] * float(ROW_CHUNK)
    slot = base + rank
    pos = [jnp.where(valid, jnp.sum(jnp.where(s, slot, 0.0), axis=0, keepdims=True), -1.0) for s in sel]

    row = lax.broadcasted_iota(jnp.int32, (cap, tt), 0).astype(F32)
    onehot = jnp.where(row == pos[0], 1.0, 0.0)
    for k in range(1, TOP_K):
        onehot = jnp.where(row == pos[k], 1.0, onehot)
    xs_ref[...] = _dot(onehot.astype(BF16), hb).astype(BF16)

    meta = jnp.concatenate(pos + gates + [jnp.zeros((LANES - 2 * TOP_K, tt), F32)], axis=0)
    meta_ref[...] = meta.T
    cnt_ref[0] = chunks_b.astype(jnp.int32)


def _route(h2, logits, n_valid):
    n_pad = h2.shape[0]
    nt = n_pad // TOK_TILE
    cap = _tile_capacity(TOK_TILE)
    return pl.pallas_call(
        functools.partial(_route_kernel, n_valid=n_valid),
        out_shape=(jax.ShapeDtypeStruct((nt * cap, D), BF16),
                   jax.ShapeDtypeStruct((n_pad, LANES), F32),
                   jax.ShapeDtypeStruct((nt, N_EXPERTS, LANES), jnp.int32)),
        grid=(nt,),
        in_specs=[pl.BlockSpec((TOK_TILE, D), lambda i: (i, 0)),
                  pl.BlockSpec((TOK_TILE, LANES), lambda i: (i, 0))],
        out_specs=(pl.BlockSpec((cap, D), lambda i: (i, 0)),
                   pl.BlockSpec((TOK_TILE, LANES), lambda i: (i, 0)),
                   pl.BlockSpec((1, N_EXPERTS, LANES), lambda i: (i, 0, 0))),
        compiler_params=_cparams(("arbitrary",)),
        name="route_dispatch",
    )(h2, logits)


def _plan_kernel(cnt_ref, be_ref, nb_ref, src_ref, dst_ref, z_ref, nz_ref, tile_off, *, nt, cap_chunks, nb_max,
                 max_pad, nz_max):
    trash_base = nt * cap_chunks
    for t in range(nt):
        tile_off[t] = 0

    def expert_body(e, carry):
        p0, blk0, pad0 = carry

        def tile_body(t, p):
            c = cnt_ref[t * N_EXPERTS + e]
            off = tile_off[t]
            base = t * cap_chunks + off

            def chunk_body(k, _):
                src_ref[p + k] = base + k
                dst_ref[p + k] = base + k
                return 0

            lax.fori_loop(0, c, chunk_body, 0)
            tile_off[t] = off + c
            return p + c

        p1 = lax.fori_loop(0, nt, tile_body, p0)
        nblk = lax.shift_right_logical(p1 - p0 + (BLOCK_CHUNKS - 1), BLOCK_CHUNKS.bit_length() - 1)
        p2 = p0 + nblk * BLOCK_CHUNKS

        def pad_body(q, _):
            src_ref[q] = 0
            dst_ref[q] = trash_base + pad0 + (q - p1)
            return 0

        lax.fori_loop(p1, p2, pad_body, 0)

        def blk_body(bb, _):
            be_ref[bb] = e
            return 0

        lax.fori_loop(blk0, blk0 + nblk, blk_body, 0)
        return p2, blk0 + nblk, pad0 + (p2 - p1)

    p_end, nb, n_pad = lax.fori_loop(0, N_EXPERTS, expert_body, (0, 0, 0))
    nb_ref[0] = nb

    def dead_blk(bb, _):
        be_ref[bb] = N_EXPERTS - 1
        return 0

    lax.fori_loop(nb, nb_max, dead_blk, 0)

    def dead_chunk(q, _):
        src_ref[q] = 0
        dst_ref[q] = 0
        return 0

    lax.fori_loop(p_end, nb_max * BLOCK_CHUNKS, dead_chunk, 0)

    def tail_body(t, z):
        used = tile_off[t]

        def one(k, _):
            z_ref[z + k - used] = t * cap_chunks + k
            return 0

        lax.fori_loop(used, cap_chunks, one, 0)
        return z + (cap_chunks - used)

    z1 = lax.fori_loop(0, nt, tail_body, 0)

    def trash_body(k, _):
        z_ref[z1 + k - n_pad] = trash_base + k
        return 0

    lax.fori_loop(n_pad, max_pad, trash_body, 0)
    nz = z1 + (max_pad - n_pad)
    nz_ref[0] = nz

    def dead_z(k, _):
        z_ref[k] = 0
        return 0

    lax.fori_loop(nz, nz_max, dead_z, 0)


def _plan(cnt_flat, nt, cap_chunks, nb_max, max_pad, nz_max):
    smem = pl.BlockSpec(memory_space=pltpu.SMEM)
    i32 = jnp.int32
    return pl.pallas_call(
        functools.partial(_plan_kernel, nt=nt, cap_chunks=cap_chunks, nb_max=nb_max, max_pad=max_pad, nz_max=nz_max),
        out_shape=(jax.ShapeDtypeStruct((nb_max,), i32), jax.ShapeDtypeStruct((1,), i32),
                   jax.ShapeDtypeStruct((nb_max * BLOCK_CHUNKS,), i32),
                   jax.ShapeDtypeStruct((nb_max * BLOCK_CHUNKS,), i32),
                   jax.ShapeDtypeStruct((nz_max,), i32), jax.ShapeDtypeStruct((1,), i32)),
        in_specs=[smem],
        out_specs=(smem,) * 6,
        scratch_shapes=[pltpu.SMEM((nt,), i32)],
        name="plan",
    )(cnt_flat)


def _expert_kernel(be_ref, nb_ref, src_ref, dst_ref, z_ref, nz_ref, xs_hbm, w_in_ref, b_in_ref, w_out_ref,
                   b_out_ref, ys_hbm, xbuf, ybuf, w_in_bf, w_out_bf, zbuf, sem_in, sem_out, sem_z):
    b = pl.program_id(0)
    nb = nb_ref[0]
    nz = nz_ref[0]
    slot = lax.rem(b, 2)

    def chunk_rows(c):
        return pl.ds(pl.multiple_of(c * ROW_CHUNK, ROW_CHUNK), ROW_CHUNK)

    def gather(blk, s):
        for j in range(BLOCK_CHUNKS):
            c = src_ref[blk * BLOCK_CHUNKS + j]
            pltpu.make_async_copy(xs_hbm.at[chunk_rows(c)], xbuf.at[s, pl.ds(j * ROW_CHUNK, ROW_CHUNK)],
                                  sem_in.at[s]).start()

    def gather_wait(s):
        pltpu.make_async_copy(xs_hbm.at[pl.ds(0, BLOCK_ROWS)], xbuf.at[s], sem_in.at[s]).wait()

    def scatter(blk, s):
        for j in range(BLOCK_CHUNKS):
            c = dst_ref[blk * BLOCK_CHUNKS + j]
            pltpu.make_async_copy(ybuf.at[s, pl.ds(j * ROW_CHUNK, ROW_CHUNK)], ys_hbm.at[chunk_rows(c)],
                                  sem_out.at[s]).start()

    def scatter_wait(s):
        pltpu.make_async_copy(ybuf.at[s], ys_hbm.at[pl.ds(0, BLOCK_ROWS)], sem_out.at[s]).wait()

    @pl.when(b == 0)
    def _():
        gather(0, 0)
        zbuf[...] = jnp.zeros(zbuf.shape, zbuf.dtype)

        def zero_one(i, _):
            pltpu.make_async_copy(zbuf, ys_hbm.at[chunk_rows(z_ref[i])], sem_z).start()
            return 0

        lax.fori_loop(0, nz, zero_one, 0)

    @pl.when(b < nb)
    def _():
        new_expert = jnp.logical_or(b == 0, be_ref[b] != be_ref[jnp.maximum(b - 1, 0)])

        @pl.when(new_expert)
        def _():
            w_in_bf[...] = w_in_ref[...].astype(BF16)
            w_out_bf[...] = w_out_ref[...].astype(BF16)

        gather_wait(slot)

        @pl.when(b + 1 < nb)
        def _():
            gather(b + 1, 1 - slot)

        @pl.when(b >= 2)
        def _():
            scatter_wait(slot)

        x = xbuf[slot]
        h = _dot(x, w_in_bf[...]) + b_in_ref[...]
        g = jnp.minimum(h[:, :D], SWIGLU_LIMIT)
        u = jnp.clip(h[:, D:], -SWIGLU_LIMIT, SWIGLU_LIMIT)
        a = (u + 1.0) * (g * jax.nn.sigmoid(SWIGLU_ALPHA * g))
        y = _dot(a.astype(BF16), w_out_bf[...]) + b_out_ref[...]
        ybuf[slot] = y.astype(BF16)
        scatter(b, slot)

        @pl.when(b == 0)
        def _():
            def zero_wait(i, _):
                pltpu.make_async_copy(zbuf, ys_hbm.at[pl.ds(0, ROW_CHUNK)], sem_z).wait()
                return 0

            lax.fori_loop(0, nz, zero_wait, 0)

        @pl.when(b == nb - 1)
        def _():
            scatter_wait(slot)

            @pl.when(b >= 1)
            def _():
                scatter_wait(1 - slot)


def _experts(xs, plan, layer, w_in, b_in, w_out, b_out, nb_max, ys_rows):
    wmap = lambda b, be, *_: (layer, be[b], 0, 0)
    grid_spec = pltpu.PrefetchScalarGridSpec(
        num_scalar_prefetch=6,
        grid=(nb_max,),
        in_specs=[pl.BlockSpec(memory_space=pl.ANY),
                  pl.BlockSpec((None, None, D, 2 * D), wmap),
                  pl.BlockSpec((None, None, 1, 2 * D), wmap),
                  pl.BlockSpec((None, None, D, D), wmap),
                  pl.BlockSpec((None, None, 1, D), wmap)],
        out_specs=pl.BlockSpec(memory_space=pl.ANY),
        scratch_shapes=[pltpu.VMEM((2, BLOCK_ROWS, D), BF16), pltpu.VMEM((2, BLOCK_ROWS, D), BF16),
                        pltpu.VMEM((D, 2 * D), BF16), pltpu.VMEM((D, D), BF16), pltpu.VMEM((ROW_CHUNK, D), BF16),
                        pltpu.SemaphoreType.DMA((2,)), pltpu.SemaphoreType.DMA((2,)), pltpu.SemaphoreType.DMA(())],
    )
    return pl.pallas_call(
        _expert_kernel,
        out_shape=jax.ShapeDtypeStruct((ys_rows, D), BF16),
        grid_spec=grid_spec,
        compiler_params=_cparams(("arbitrary",)),
        name="experts",
    )(*plan, xs, w_in, b_in, w_out, b_out)


def _combine_kernel(ys_ref, meta_ref, x1_ref, ada_ref, lng_ref, lnb_ref, o_ref):
    tt = x1_ref.shape[0]
    cap = ys_ref.shape[0]
    meta = meta_ref[...]
    col = lax.broadcasted_iota(jnp.int32, (tt, cap), 1).astype(F32)
    gmat = jnp.zeros((tt, cap), F32)
    for k in range(TOP_K):
        gmat = jnp.where(col == meta[:, k:k + 1], meta[:, TOP_K + k:TOP_K + k + 1], gmat)
    f = _dot(gmat.astype(BF16), ys_ref[...])
    ada = ada_ref[0] if len(ada_ref.shape) == 3 else ada_ref[...]
    g2 = ada[:, 5 * D:6 * D]
    o_ref[...] = _ln(ALPHA * x1_ref[...] + (1.0 + g2) * f, lng_ref[...], lnb_ref[...])


def _combine_prompt(ys, meta, x1, ada_p, ln_g, ln_b, seq):
    n = x1.shape[0]
    cap = _tile_capacity(TOK_TILE)
    per_seq = seq // TOK_TILE
    return pl.pallas_call(
        _combine_kernel,
        out_shape=jax.ShapeDtypeStruct((n, D), F32),
        grid=(n // TOK_TILE,),
        in_specs=[pl.BlockSpec((cap, D), lambda i: (i, 0)),
                  pl.BlockSpec((TOK_TILE, LANES), lambda i: (i, 0)),
                  pl.BlockSpec((TOK_TILE, D), lambda i: (i, 0)),
                  pl.BlockSpec((1, 1, 6 * D), lambda i: (i // per_seq, 0, 0)),
                  pl.BlockSpec((1, D), lambda i: (0, 0)),
                  pl.BlockSpec((1, D), lambda i: (0, 0))],
        out_specs=pl.BlockSpec((TOK_TILE, D), lambda i: (i, 0)),
        compiler_params=_cparams(("arbitrary",)),
        name="combine_prompt",
    )(ys, meta, x1, ada_p, ln_g, ln_b)


def _combine_sample(ys, meta, x1, ada_s, ln_g, ln_b, tile):
    n = x1.shape[0]
    cap = _tile_capacity(TOK_TILE)
    return pl.pallas_call(
        _combine_kernel,
        out_shape=jax.ShapeDtypeStruct((n, D), F32),
        grid=(1,),
        in_specs=[pl.BlockSpec((cap, D), lambda i: (tile, 0)),
                  pl.BlockSpec((n, LANES), lambda i: (tile * (TOK_TILE // n), 0)),
                  pl.BlockSpec((n, D), lambda i: (0, 0)),
                  pl.BlockSpec((n, 6 * D), lambda i: (0, 0)),
                  pl.BlockSpec((1, D), lambda i: (0, 0)),
                  pl.BlockSpec((1, D), lambda i: (0, 0))],
        out_specs=pl.BlockSpec((n, D), lambda i: (0, 0)),
        compiler_params=_cparams(("arbitrary",)),
        name="combine_sample",
    )(ys, meta, x1, ada_s, ln_g, ln_b)


def _moe_and_norm(h2_p, h2_s, lg_p, lg_s, x1_p, x1_s, ada_p, ada_s, ln_g, ln_b,
                  layer, w_in, b_in, w_out, b_out, seq):
    n_p, n_s = h2_p.shape[0], h2_s.shape[0]
    n_valid = n_p + n_s
    nt = -(-n_valid // TOK_TILE)
    n_pad = nt * TOK_TILE
    cap = _tile_capacity(TOK_TILE)
    cap_chunks = cap // ROW_CHUNK
    h2 = jnp.concatenate([h2_p, h2_s, jnp.zeros((n_pad - n_valid, D), BF16)], axis=0)
    logits = jnp.concatenate([lg_p, lg_s, jnp.zeros((n_pad - n_valid, LANES), F32)], axis=0)
    xs, meta, cnt = _route(h2, logits, n_valid)
    total_chunks_max = (TOP_K * n_valid + nt * N_EXPERTS * (ROW_CHUNK - 1)) // ROW_CHUNK
    nb_max = -(-total_chunks_max // BLOCK_CHUNKS) + N_EXPERTS
    max_pad = N_EXPERTS * (BLOCK_CHUNKS - 1)
    nz_max = nt * cap_chunks - (TOP_K * n_valid) // ROW_CHUNK + max_pad
    plan = _plan(cnt[:, :, 0].reshape(-1), nt, cap_chunks, nb_max, max_pad, nz_max)
    ys = _experts(xs, plan, layer, w_in, b_in, w_out, b_out, nb_max, (nt * cap_chunks + max_pad) * ROW_CHUNK)
    x2_p = _combine_prompt(ys, meta, x1_p, ada_p, ln_g, ln_b, seq)
    x2_s = _combine_sample(ys, meta, x1_s, ada_s, ln_g, ln_b, n_p // TOK_TILE)
    return x2_p, x2_s


def _row(v):
    return v.reshape(1, -1)


def kernel(x_prompt, x_sample, state_conv_a, state_pool_b, state_conv_c, c_prompt, c_sample,
           w_ada, b_ada, ln1_g, ln1_b, ln2_g, ln2_b,
           a_w_in, a_b_in, a_w_dw, a_b_dw, a_ln_g, a_ln_b, a_w_out,
           b_w_grp, b_scale, c_w_in, c_w_conv, c_w_out,
           d_w_in, d_b_in, d_ln_g, d_ln_b, d_w_s, d_b_s, d_w_out,
           w_router, b_router, w_moe_in, b_moe_in, w_moe_out, b_moe_out):
    bp, seq, _ = x_prompt.shape
    bs = x_sample.shape[0]
    assert seq % TOK_TILE == 0 and (bp * seq) % TOK_TILE == 0 and TOK_TILE % bs == 0
    assert x_sample.shape[1] == 1 and w_ada.shape[0] == DEPTH == 4

    ada = _ada(jnp.concatenate([c_prompt, c_sample], axis=0), w_ada, b_ada)
    xp = x_prompt.reshape(bp * seq, D)
    xs = x_sample.reshape(bs, D)
    b_moe_in4 = b_moe_in.reshape(DEPTH, N_EXPERTS, 1, 2 * D)
    b_moe_out4 = b_moe_out.reshape(DEPTH, N_EXPERTS, 1, D)
    states = {}

    for i in range(DEPTH):
        ada_p = ada[i, :bp].reshape(bp, 1, 6 * D)
        ada_s = ada[i, bp:]
        wr = jnp.pad(w_router[i], ((0, 0), (0, LANES - N_EXPERTS)))
        ln1 = [_row(ln1_g[i]), _row(ln1_b[i]), wr, _row(jnp.pad(b_router[i], (0, LANES - N_EXPERTS)))]
        if i == 0:
            wts = [a_w_in[0].astype(BF16), _row(a_b_in[0]), a_w_dw[0], _row(a_b_dw[0]), _row(a_ln_g[0]),
                   _row(a_ln_b[0]), a_w_out[0].astype(BF16)] + ln1
            x1_p, h2_p, lg_p, st =_mix_prompt(_mix_a_prompt_kernel, "mix_a_prompt", xp, ada_p, wts, A_HALO,
                                         [pltpu.VMEM((A_HALO + TOK_TILE, D), F32)], seq)
            states["a_p"] = st[:, A_HALO - (CONV_A_WIDTH - 1):][None]
            hist = jnp.transpose(state_conv_a[0], (1, 0, 2))
            x1_s, h2_s, lg_s, new =_mix_sample(_mix_a_sample_kernel, "mix_a_sample", xs, ada_s, [hist] + wts)
            states["a_s"] = jnp.concatenate([state_conv_a[0][:, 1:], new[:, None]], axis=1)[None]
        elif i == 1:
            wts = [b_w_grp[0], _row(b_scale[0])] + ln1
            x1_p, h2_p, lg_p, st =_mix_prompt(_mix_b_prompt_kernel, "mix_b_prompt", xp, ada_p, wts, B_HALO,
                                         [pltpu.VMEM((B_HALO + TOK_TILE, D), F32)], seq)
            states["b_p"] = st[:, B_HALO - POOL_HIST:][None]
            hist = jnp.transpose(state_pool_b[0], (1, 0, 2))
            x1_s, h2_s, lg_s, new =_mix_sample(_mix_b_sample_kernel, "mix_b_sample", xs, ada_s, [hist] + wts)
            states["b_s"] = jnp.concatenate([state_pool_b[0][:, 1:], new[:, None]], axis=1)[None]
        elif i == 2:
            wts = [c_w_in[0].astype(BF16), c_w_conv[0], c_w_out[0].astype(BF16)] + ln1
            x1_p, h2_p, lg_p, st =_mix_prompt(_mix_c_prompt_kernel, "mix_c_prompt", xp, ada_p, wts, C_HALO,
                                         [pltpu.VMEM((C_HALO + TOK_TILE, D), F32)], seq)
            states["c_p"] = st[:, C_HALO - (CONV_C_WIDTH - 1):][None]
            hist = jnp.transpose(state_conv_c[0], (1, 0, 2))
            x1_s, h2_s, lg_s, new =_mix_sample(_mix_c_sample_kernel, "mix_c_sample", xs, ada_s, [hist] + wts)
            states["c_s"] = jnp.concatenate([state_conv_c[0][:, 1:], new[:, None]], axis=1)[None]
        else:
            common = [d_w_in[0].astype(BF16), _row(d_b_in[0]), _row(d_ln_g[0]), _row(d_ln_b[0])]
            wts = common + [d_w_s[0], d_b_s[0].T, d_w_out[0].astype(BF16)] + ln1
            x1_p, h2_p, lg_p, st =_mix_prompt(_mix_d_prompt_kernel, "mix_d_prompt", xp, ada_p, wts, CHUNK, [], seq)
            states["d_p"] = st[None]
            w_s0 = jnp.repeat(d_w_s[0][:, 0, 0], SGU_GROUP).reshape(1, D)
            b_s0 = jnp.repeat(d_b_s[0][:, 0], SGU_GROUP).reshape(1, D)
            wts_s = common + [w_s0, b_s0, d_w_out[0].astype(BF16)] + ln1
            x1_s, h2_s, lg_s, new =_mix_sample(_mix_d_sample_kernel, "mix_d_sample", xs, ada_s, wts_s)
            states["d_s"] = new[:, None][None]

        xp, xs = _moe_and_norm(h2_p, h2_s, lg_p, lg_s, x1_p, x1_s, ada_p, ada_s, _row(ln2_g[i]), _row(ln2_b[i]),
                               i, w_moe_in, b_moe_in4, w_moe_out, b_moe_out4, seq)

    return (xp.reshape(bp, seq, D), xs.reshape(bs, 1, D),
            states["a_p"], states["a_s"], states["b_p"], states["b_s"],
            states["c_p"], states["c_s"], states["d_p"], states["d_s"])
```

```python
import functools

import jax
import jax.numpy as jnp
from jax import lax
from jax.experimental import pallas as pl
from jax.experimental.pallas import tpu as pltpu

F32 = jnp.float32
BF16 = jnp.bfloat16

D = 1024
DEPTH = 4
N_EXPERTS = 32
TOP_K = 4
CONV_A_WIDTH = 31
POOL_WINDOWS = (2, 4, 8, 16)
POOL_GROUP = D // 4
POOL_HIST = 15
CONV_C_WIDTH = 3
CHUNK = 128
N_SGU_GROUPS = 4
SGU_GROUP = D // N_SGU_GROUPS
SWIGLU_LIMIT = 7.0
SWIGLU_ALPHA = 1.702
ALPHA = (2 * DEPTH) ** 0.25
LN_EPS = 1e-5

LANES = 128
SUBLANES_BF16 = 16
VMEM_LIMIT = 56 * 1024 * 1024

TOK_TILE = 512
MOE_TILE = 256
ROW_CHUNK = SUBLANES_BF16
BLOCK_CHUNKS = 32
BLOCK_ROWS = BLOCK_CHUNKS * ROW_CHUNK
HALF_CHUNKS = BLOCK_CHUNKS // 2
HALF_ROWS = BLOCK_ROWS // 2


def _tile_capacity(tt):
    cap = TOP_K * tt + N_EXPERTS * (ROW_CHUNK - 1)
    return -(-cap // 256) * 256


def _cparams(sem=None):
    return pltpu.CompilerParams(dimension_semantics=sem, vmem_limit_bytes=VMEM_LIMIT)


def _dot(a, b):
    return jnp.dot(a, b, preferred_element_type=F32)


def _dot_split(a, b):
    hi = a.astype(BF16)
    lo = (a - hi.astype(F32)).astype(BF16)
    return _dot(hi, b) + _dot(lo, b)


def _ln(x, g, b):
    mu = jnp.mean(x, axis=-1, keepdims=True)
    xc = x - mu
    var = jnp.mean(xc * xc, axis=-1, keepdims=True)
    return xc * lax.rsqrt(var + LN_EPS) * g + b


def _split_ada(ada):
    return [ada[:, i * D:(i + 1) * D] for i in range(6)]


def _post_mixer(x, out, ada, ln_g, ln_b, wr_ref, br_ref, x1_ref, h2_ref, lg_ref):
    _, _, g1, sh2, sc2, _ = _split_ada(ada)
    x1 = _ln(ALPHA * x + (1.0 + g1) * out, ln_g, ln_b)
    x1_ref[...] = x1
    h2 = x1 * (1.0 + sc2) + sh2
    hi = h2.astype(BF16)
    h2_ref[...] = hi
    lo = (h2 - hi.astype(F32)).astype(BF16)
    wr = wr_ref[...]
    w_hi = wr.astype(BF16)
    w_lo = (wr - w_hi.astype(F32)).astype(BF16)
    lg_ref[...] = _dot(hi, w_hi) + _dot(lo, w_hi) + _dot(hi, w_lo) + br_ref[...]


def _ada_kernel(c_ref, w_ref, b_ref, o_ref):
    c = c_ref[...]
    sc = (c * jax.nn.sigmoid(c)).astype(BF16)
    o_ref[0] = _dot(sc, w_ref[0].astype(BF16)) + b_ref[0]


def _ada(c_all, w_ada, b_ada):
    rows = c_all.shape[0]
    return pl.pallas_call(
        _ada_kernel,
        out_shape=jax.ShapeDtypeStruct((DEPTH, rows, 6 * D), F32),
        grid=(DEPTH, 6),
        in_specs=[
            pl.BlockSpec((rows, D), lambda i, j: (0, 0)),
            pl.BlockSpec((1, D, D), lambda i, j: (i, 0, j)),
            pl.BlockSpec((1, 1, D), lambda i, j: (i, 0, j)),
        ],
        out_specs=pl.BlockSpec((1, rows, D), lambda i, j: (i, 0, j)),
        compiler_params=_cparams(("arbitrary", "arbitrary")),
        name="ada",
    )(c_all, w_ada, b_ada.reshape(DEPTH, 1, 6 * D))


A_HALO = 32
B_HALO = 16
C_HALO = 8


def _mix_a_prompt_kernel(x_ref, ada_ref, w_in_ref, b_in_ref, w_dw_ref, b_dw_ref, lng_ref, lnb_ref,
                         w_out_ref, ln1g_ref, ln1b_ref, wr_ref, br_ref, x1_ref, h2_ref, lg_ref, st_ref, u_scr):
    t = pl.program_id(1)
    tt = x_ref.shape[0]
    x = x_ref[...]
    ada = ada_ref[0]
    sh1, sc1 = ada[:, 0:D], ada[:, D:2 * D]
    h = (x * (1.0 + sc1) + sh1).astype(BF16)
    z = _dot(h, w_in_ref[...]) + b_in_ref[...]
    u = z[:, :D] * jax.nn.sigmoid(z[:, D:])

    @pl.when(t == 0)
    def _():
        u_scr[0:A_HALO, :] = jnp.zeros((A_HALO, D), F32)

    u_scr[A_HALO:A_HALO + tt, :] = u
    first = A_HALO - (CONV_A_WIDTH - 1)
    y = jnp.zeros((tt, D), F32) + b_dw_ref[...]
    for k in range(CONV_A_WIDTH):
        y = y + w_dw_ref[k:k + 1, :] * u_scr[first + k:first + k + tt, :]
    y = _ln(y, lng_ref[...], lnb_ref[...])
    y = y * jax.nn.sigmoid(y)
    out = _dot(y.astype(BF16), w_out_ref[...])
    _post_mixer(x, out, ada, ln1g_ref[...], ln1b_ref[...], wr_ref, br_ref, x1_ref, h2_ref, lg_ref)
    tail = u_scr[tt:tt + A_HALO, :]
    u_scr[0:A_HALO, :] = tail

    @pl.when(t == pl.num_programs(1) - 1)
    def _():
        st_ref[0] = tail


def _mix_b_prompt_kernel(x_ref, ada_ref, w_grp_ref, scale_ref, ln1g_ref, ln1b_ref,
                         wr_ref, br_ref, x1_ref, h2_ref, lg_ref, st_ref, h_scr):
    t = pl.program_id(1)
    tt = x_ref.shape[0]
    x = x_ref[...]
    ada = ada_ref[0]
    sh1, sc1 = ada[:, 0:D], ada[:, D:2 * D]
    h = x * (1.0 + sc1) + sh1

    @pl.when(t == 0)
    def _():
        h_scr[0:B_HALO, :] = jnp.zeros((B_HALO, D), F32)

    h_scr[B_HALO:B_HALO + tt, :] = h
    pos = (t * tt + lax.broadcasted_iota(jnp.int32, (tt, POOL_GROUP), 0)).astype(F32)
    outs = []
    for gi, w in enumerate(POOL_WINDOWS):
        lo = gi * POOL_GROUP
        s = h_scr[B_HALO:B_HALO + tt, lo:lo + POOL_GROUP]
        for j in range(1, w):
            s = s + h_scr[B_HALO - j:B_HALO - j + tt, lo:lo + POOL_GROUP]
        cnt = jnp.minimum(float(w), pos + 1.0)
        pooled = s / cnt - h[:, lo:lo + POOL_GROUP]
        outs.append(_dot_split(pooled, w_grp_ref[gi].astype(BF16)))
    out = jnp.concatenate(outs, axis=-1) * scale_ref[...]
    _post_mixer(x, out, ada, ln1g_ref[...], ln1b_ref[...], wr_ref, br_ref, x1_ref, h2_ref, lg_ref)
    tail = h_scr[tt:tt + B_HALO, :]
    h_scr[0:B_HALO, :] = tail

    @pl.when(t == pl.num_programs(1) - 1)
    def _():
        st_ref[0] = tail


def _mix_c_prompt_kernel(x_ref, ada_ref, w_in_ref, w_conv_ref, w_out_ref, ln1g_ref, ln1b_ref,
                         wr_ref, br_ref, x1_ref, h2_ref, lg_ref, st_ref, v_scr):
    t = pl.program_id(1)
    tt = x_ref.shape[0]
    x = x_ref[...]
    ada = ada_ref[0]
    sh1, sc1 = ada[:, 0:D], ada[:, D:2 * D]
    h = (x * (1.0 + sc1) + sh1).astype(BF16)
    z = _dot(h, w_in_ref[...])
    bg = z[:, :D]
    v = z[:, D:2 * D] * z[:, 2 * D:]

    @pl.when(t == 0)
    def _():
        v_scr[0:C_HALO, :] = jnp.zeros((C_HALO, D), F32)

    v_scr[C_HALO:C_HALO + tt, :] = v
    first = C_HALO - (CONV_C_WIDTH - 1)
    y = w_conv_ref[CONV_C_WIDTH - 1:CONV_C_WIDTH, :] * v
    for k in range(CONV_C_WIDTH - 1):
        y = y + w_conv_ref[k:k + 1, :] * v_scr[first + k:first + k + tt, :]
    out = _dot((bg * y).astype(BF16), w_out_ref[...])
    _post_mixer(x, out, ada, ln1g_ref[...], ln1b_ref[...], wr_ref, br_ref, x1_ref, h2_ref, lg_ref)
    tail = v_scr[tt:tt + C_HALO, :]
    v_scr[0:C_HALO, :] = tail

    @pl.when(t == pl.num_programs(1) - 1)
    def _():
        st_ref[0] = tail


def _gelu_exact(x):
    return 0.5 * x * (1.0 + lax.erf(x * (2.0 ** -0.5)))


def _mix_d_prompt_kernel(x_ref, ada_ref, w_in_ref, b_in_ref, lng_ref, lnb_ref, w_s_ref, b_st_ref,
                         w_out_ref, ln1g_ref, ln1b_ref, wr_ref, br_ref, x1_ref, h2_ref, lg_ref, st_ref):
    t = pl.program_id(1)
    tt = x_ref.shape[0]
    x = x_ref[...]
    ada = ada_ref[0]
    sh1, sc1 = ada[:, 0:D], ada[:, D:2 * D]
    h = (x * (1.0 + sc1) + sh1).astype(BF16)
    z = _gelu_exact(_dot(h, w_in_ref[...]) + b_in_ref[...])
    u = z[:, :D]
    v = _ln(z[:, D:], lng_ref[...], lnb_ref[...])
    vb = v.astype(BF16)
    row = lax.broadcasted_iota(jnp.int32, (CHUNK, CHUNK), 0)
    col = lax.broadcasted_iota(jnp.int32, (CHUNK, CHUNK), 1)
    causal = col <= row
    chunks = []
    for c in range(tt // CHUNK):
        groups = []
        for g in range(N_SGU_GROUPS):
            w = jnp.where(causal, w_s_ref[g], 0.0).astype(BF16)
            m = _dot(w, vb[c * CHUNK:(c + 1) * CHUNK, g * SGU_GROUP:(g + 1) * SGU_GROUP])
            groups.append(m + b_st_ref[:, g:g + 1])
        chunks.append(jnp.concatenate(groups, axis=-1))
    mixed = jnp.concatenate(chunks, axis=0)
    out = _dot((u * mixed).astype(BF16), w_out_ref[...])
    _post_mixer(x, out, ada, ln1g_ref[...], ln1b_ref[...], wr_ref, br_ref, x1_ref, h2_ref, lg_ref)

    @pl.when(t == pl.num_programs(1) - 1)
    def _():
        st_ref[0] = v[tt - CHUNK:, :]


def _full(shape):
    nd = len(shape)
    return pl.BlockSpec(shape, lambda b, t: (0,) * nd)


def _mix_prompt(kernel_fn, name, x, ada_p, weights, state_rows, scratch, seq):
    n = x.shape[0]
    batch = n // seq
    nt = seq // TOK_TILE
    tok = pl.BlockSpec((TOK_TILE, D), lambda b, t: (b * nt + t, 0))
    lgt = pl.BlockSpec((TOK_TILE, LANES), lambda b, t: (b * nt + t, 0))
    return pl.pallas_call(
        kernel_fn,
        out_shape=(jax.ShapeDtypeStruct((n, D), F32), jax.ShapeDtypeStruct((n, D), BF16),
                   jax.ShapeDtypeStruct((n, LANES), F32), jax.ShapeDtypeStruct((batch, state_rows, D), F32)),
        grid=(batch, nt),
        in_specs=[tok, pl.BlockSpec((1, 1, 6 * D), lambda b, t: (b, 0, 0))] + [_full(w.shape) for w in weights],
        out_specs=(tok, tok, lgt, pl.BlockSpec((1, state_rows, D), lambda b, t: (b, 0, 0))),
        scratch_shapes=scratch,
        compiler_params=_cparams(("arbitrary", "arbitrary")),
        name=name,
    )(x, ada_p, *weights)


def _mix_a_sample_kernel(x_ref, ada_ref, hist_ref, w_in_ref, b_in_ref, w_dw_ref, b_dw_ref, lng_ref, lnb_ref,
                         w_out_ref, ln1g_ref, ln1b_ref, wr_ref, br_ref, x1_ref, h2_ref, lg_ref, new_ref):
    x = x_ref[...]
    ada = ada_ref[...]
    sh1, sc1 = ada[:, 0:D], ada[:, D:2 * D]
    h = (x * (1.0 + sc1) + sh1).astype(BF16)
    z = _dot(h, w_in_ref[...]) + b_in_ref[...]
    u = z[:, :D] * jax.nn.sigmoid(z[:, D:])
    y = b_dw_ref[...] + w_dw_ref[CONV_A_WIDTH - 1:CONV_A_WIDTH, :] * u
    for k in range(CONV_A_WIDTH - 1):
        y = y + w_dw_ref[k:k + 1, :] * hist_ref[k]
    y = _ln(y, lng_ref[...], lnb_ref[...])
    y = y * jax.nn.sigmoid(y)
    out = _dot(y.astype(BF16), w_out_ref[...])
    _post_mixer(x, out, ada, ln1g_ref[...], ln1b_ref[...], wr_ref, br_ref, x1_ref, h2_ref, lg_ref)
    new_ref[...] = u


def _mix_b_sample_kernel(x_ref, ada_ref, hist_ref, w_grp_ref, scale_ref, ln1g_ref, ln1b_ref,
                         wr_ref, br_ref, x1_ref, h2_ref, lg_ref, new_ref):
    x = x_ref[...]
    ada = ada_ref[...]
    sh1, sc1 = ada[:, 0:D], ada[:, D:2 * D]
    h = x * (1.0 + sc1) + sh1
    outs = []
    for gi, w in enumerate(POOL_WINDOWS):
        lo = gi * POOL_GROUP
        s = h[:, lo:lo + POOL_GROUP]
        for j in range(1, w):
            s = s + hist_ref[POOL_HIST - j][:, lo:lo + POOL_GROUP]
        pooled = s / float(w) - h[:, lo:lo + POOL_GROUP]
        outs.append(_dot_split(pooled, w_grp_ref[gi].astype(BF16)))
    out = jnp.concatenate(outs, axis=-1) * scale_ref[...]
    _post_mixer(x, out, ada, ln1g_ref[...], ln1b_ref[...], wr_ref, br_ref, x1_ref, h2_ref, lg_ref)
    new_ref[...] = h


def _mix_c_sample_kernel(x_ref, ada_ref, hist_ref, w_in_ref, w_conv_ref, w_out_ref, ln1g_ref, ln1b_ref,
                         wr_ref, br_ref, x1_ref, h2_ref, lg_ref, new_ref):
    x = x_ref[...]
    ada = ada_ref[...]
    sh1, sc1 = ada[:, 0:D], ada[:, D:2 * D]
    h = (x * (1.0 + sc1) + sh1).astype(BF16)
    z = _dot(h, w_in_ref[...])
    bg = z[:, :D]
    v = z[:, D:2 * D] * z[:, 2 * D:]
    y = w_conv_ref[CONV_C_WIDTH - 1:CONV_C_WIDTH, :] * v
    for k in range(CONV_C_WIDTH - 1):
        y = y + w_conv_ref[k:k + 1, :] * hist_ref[k]
    out = _dot((bg * y).astype(BF16), w_out_ref[...])
    _post_mixer(x, out, ada, ln1g_ref[...], ln1b_ref[...], wr_ref, br_ref, x1_ref, h2_ref, lg_ref)
    new_ref[...] = v


def _mix_d_sample_kernel(x_ref, ada_ref, w_in_ref, b_in_ref, lng_ref, lnb_ref, w_s0_ref, b_s0_ref,
                         w_out_ref, ln1g_ref, ln1b_ref, wr_ref, br_ref, x1_ref, h2_ref, lg_ref, new_ref):
    x = x_ref[...]
    ada = ada_ref[...]
    sh1, sc1 = ada[:, 0:D], ada[:, D:2 * D]
    h = (x * (1.0 + sc1) + sh1).astype(BF16)
    z = _gelu_exact(_dot(h, w_in_ref[...]) + b_in_ref[...])
    u = z[:, :D]
    v = _ln(z[:, D:], lng_ref[...], lnb_ref[...])
    mixed = w_s0_ref[...] * v + b_s0_ref[...]
    out = _dot((u * mixed).astype(BF16), w_out_ref[...])
    _post_mixer(x, out, ada, ln1g_ref[...], ln1b_ref[...], wr_ref, br_ref, x1_ref, h2_ref, lg_ref)
    new_ref[...] = v


def _mix_sample(kernel_fn, name, x, ada_s, arrays):
    n = x.shape[0]
    return pl.pallas_call(
        kernel_fn,
        out_shape=(jax.ShapeDtypeStruct((n, D), F32), jax.ShapeDtypeStruct((n, D), BF16),
                   jax.ShapeDtypeStruct((n, LANES), F32), jax.ShapeDtypeStruct((n, D), F32)),
        compiler_params=_cparams(),
        name=name,
    )(x, ada_s, *arrays)


def _route_kernel(hp_ref, hs_ref, lgp_ref, lgs_ref, xs_ref, meta_ref, cnt_ref, *, n_valid):
    i = pl.program_id(0)
    tt = hp_ref.shape[0]
    ns = hs_ref.shape[0]
    cap = xs_ref.shape[0]
    is_sample = i == pl.num_programs(0) - 1
    hs = jnp.concatenate([hs_ref[...], jnp.zeros((tt - ns, D), BF16)], axis=0)
    lgs = jnp.concatenate([lgs_ref[...], jnp.zeros((tt - ns, LANES), F32)], axis=0)
    hb = jnp.where(is_sample, hs, hp_ref[...])
    logits = jnp.where(is_sample, lgs, lgp_ref[...]).T[:N_EXPERTS, :]
    e_iota = lax.broadcasted_iota(jnp.int32, (N_EXPERTS, tt), 0)
    valid = (i * tt + lax.broadcasted_iota(jnp.int32, (1, tt), 1)) < n_valid
    work = logits
    sel, top = [], []
    for _ in range(TOP_K):
        m = jnp.max(work, axis=0, keepdims=True)
        idx = jnp.min(jnp.where(work == m, e_iota, N_EXPERTS), axis=0, keepdims=True)
        oh = e_iota == idx
        sel.append(oh)
        top.append(m)
        work = jnp.where(oh, -jnp.inf, work)
    ex = [jnp.exp(v - top[0]) for v in top]
    denom = ex[0] + ex[1] + ex[2] + ex[3]
    gates = [e / denom for e in ex]

    member = jnp.where(sel[0] | sel[1] | sel[2] | sel[3], 1.0, 0.0)
    member = jnp.where(valid, member, 0.0)
    r_i = lax.broadcasted_iota(jnp.int32, (tt, tt), 0)
    c_i = lax.broadcasted_iota(jnp.int32, (tt, tt), 1)
    before = jnp.where(r_i < c_i, 1.0, 0.0).astype(BF16)
    rank = _dot(member.astype(BF16), before)
    count = jnp.sum(member, axis=1, keepdims=True)
    chunks = jnp.floor((count + float(ROW_CHUNK - 1)) * (1.0 / ROW_CHUNK))
    chunks_b = jnp.broadcast_to(chunks, (N_EXPERTS, LANES))
    er = lax.broadcasted_iota(jnp.int32, (N_EXPERTS, N_EXPERTS), 0)
    ec = lax.broadcasted_iota(jnp.int32, (N_EXPERTS, N_EXPERTS), 1)
    lower = jnp.where(ec < er, 1.0, 0.0).astype(BF16)
    chunk_off = _dot(lower, chunks_b.astype(BF16))
    base = chunk_off[:, 0:1] * float(ROW_CHUNK)
    slot = base + rank
    pos = [jnp.where(valid, jnp.sum(jnp.where(s, slot, 0.0), axis=0, keepdims=True), -1.0) for s in sel]

    row = lax.broadcasted_iota(jnp.int32, (cap, tt), 0).astype(F32)
    onehot = jnp.where(row == pos[0], 1.0, 0.0)
    for k in range(1, TOP_K):
        onehot = jnp.where(row == pos[k], 1.0, onehot)
    xs_ref[...] = _dot(onehot.astype(BF16), hb).astype(BF16)

    meta = jnp.concatenate(pos + gates + [jnp.zeros((LANES - 2 * TOP_K, tt), F32)], axis=0)
    meta_ref[...] = meta.T
    cnt_ref[0] = chunks_b.astype(jnp.int32)


def _route(h2_p, h2_s, lg_p, lg_s):
    n_p, n_s = h2_p.shape[0], h2_s.shape[0]
    ntp = n_p // MOE_TILE
    nt = ntp + 1
    cap = _tile_capacity(MOE_TILE)
    prompt_tile = lambda i: (jnp.minimum(i, ntp - 1), 0)
    return pl.pallas_call(
        functools.partial(_route_kernel, n_valid=n_p + n_s),
        out_shape=(jax.ShapeDtypeStruct((nt * cap, D), BF16),
                   jax.ShapeDtypeStruct((nt * MOE_TILE, LANES), F32),
                   jax.ShapeDtypeStruct((nt, N_EXPERTS, LANES), jnp.int32)),
        grid=(nt,),
        in_specs=[pl.BlockSpec((MOE_TILE, D), prompt_tile),
                  pl.BlockSpec((n_s, D), lambda i: (0, 0)),
                  pl.BlockSpec((MOE_TILE, LANES), prompt_tile),
                  pl.BlockSpec((n_s, LANES), lambda i: (0, 0))],
        out_specs=(pl.BlockSpec((cap, D), lambda i: (i, 0)),
                   pl.BlockSpec((MOE_TILE, LANES), lambda i: (i, 0)),
                   pl.BlockSpec((1, N_EXPERTS, LANES), lambda i: (i, 0, 0))),
        compiler_params=_cparams(("arbitrary",)),
        name="route_dispatch",
    )(h2_p, h2_s, lg_p, lg_s)


def _plan_kernel(cnt_ref, be_ref, nx_ref, half_ref, nb_ref, src_ref, dst_ref, z_ref, nz_ref, tile_off, *, nt,
                 cap_chunks, nb_max, max_pad, nz_max):
    trash_base = nt * cap_chunks
    for t in range(nt):
        tile_off[t] = 0

    def expert_body(e, carry):
        p0, blk0, pad0 = carry

        def tile_body(t, p):
            c = cnt_ref[t * N_EXPERTS + e]
            off = tile_off[t]
            base = t * cap_chunks + off

            def chunk_body(k, _):
                src_ref[p + k] = base + k
                dst_ref[p + k] = base + k
                return 0

            lax.fori_loop(0, c, chunk_body, 0)
            tile_off[t] = off + c
            return p + c

        p1 = lax.fori_loop(0, nt, tile_body, p0)
        nblk = lax.shift_right_logical(p1 - p0 + (BLOCK_CHUNKS - 1), BLOCK_CHUNKS.bit_length() - 1)
        p2 = p0 + nblk * BLOCK_CHUNKS

        def pad_body(q, _):
            src_ref[q] = 0
            dst_ref[q] = trash_base + pad0 + (q - p1)
            return 0

        lax.fori_loop(p1, p2, pad_body, 0)

        def blk_body(bb, _):
            be_ref[bb] = e
            half_ref[bb] = 0
            return 0

        lax.fori_loop(blk0, blk0 + nblk, blk_body, 0)

        @pl.when(nblk > 0)
        def _():
            half_ref[blk0 + nblk - 1] = jnp.where(p2 - p1 >= HALF_CHUNKS, 1, 0)

        return p2, blk0 + nblk, pad0 + (p2 - p1)

    p_end, nb, n_pad = lax.fori_loop(0, N_EXPERTS, expert_body, (0, 0, 0))
    nb_ref[0] = nb

    def dead_blk(bb, _):
        be_ref[bb] = N_EXPERTS - 1
        half_ref[bb] = 0
        nx_ref[bb] = -1
        return 0

    lax.fori_loop(nb, nb_max, dead_blk, 0)

    def next_body(k, nxt):
        bb = nb - 1 - k
        nx_ref[bb] = nxt
        prev = be_ref[jnp.maximum(bb - 1, 0)]
        return jnp.where(prev != be_ref[bb], be_ref[bb], nxt)

    lax.fori_loop(0, nb, next_body, -1)

    def dead_chunk(q, _):
        src_ref[q] = 0
        dst_ref[q] = 0
        return 0

    lax.fori_loop(p_end, nb_max * BLOCK_CHUNKS, dead_chunk, 0)

    def tail_body(t, z):
        used = tile_off[t]

        def one(k, _):
            z_ref[z + k - used] = t * cap_chunks + k
            return 0

        lax.fori_loop(used, cap_chunks, one, 0)
        return z + (cap_chunks - used)

    z1 = lax.fori_loop(0, nt, tail_body, 0)

    def trash_body(k, _):
        z_ref[z1 + k - n_pad] = trash_base + k
        return 0

    lax.fori_loop(n_pad, max_pad, trash_body, 0)
    nz = z1 + (max_pad - n_pad)
    nz_ref[0] = nz

    def dead_z(k, _):
        z_ref[k] = 0
        return 0

    lax.fori_loop(nz, nz_max, dead_z, 0)


def _plan(cnt_flat, nt, cap_chunks, nb_max, max_pad, nz_max):
    smem = pl.BlockSpec(memory_space=pltpu.SMEM)
    i32 = jnp.int32
    return pl.pallas_call(
        functools.partial(_plan_kernel, nt=nt, cap_chunks=cap_chunks, nb_max=nb_max, max_pad=max_pad, nz_max=nz_max),
        out_shape=(jax.ShapeDtypeStruct((nb_max,), i32), jax.ShapeDtypeStruct((nb_max,), i32),
                   jax.ShapeDtypeStruct((nb_max,), i32), jax.ShapeDtypeStruct((1,), i32),
                   jax.ShapeDtypeStruct((nb_max * BLOCK_CHUNKS,), i32),
                   jax.ShapeDtypeStruct((nb_max * BLOCK_CHUNKS,), i32),
                   jax.ShapeDtypeStruct((nz_max,), i32), jax.ShapeDtypeStruct((1,), i32)),
        in_specs=[smem],
        out_specs=(smem,) * 8,
        scratch_shapes=[pltpu.SMEM((nt,), i32)],
        name="plan",
    )(cnt_flat)


def _expert_kernel(be_ref, nx_ref, half_ref, nb_ref, src_ref, dst_ref, z_ref, nz_ref, xs_hbm, w_in_hbm, b_in_ref,
                   w_out_hbm, b_out_ref, ys_hbm, xbuf, ybuf, w_in_f32, w_out_f32, w_in_bf, w_out_bf, zbuf,
                   sem_in, sem_out, sem_z, sem_w, *, layer):
    b = pl.program_id(0)
    nb = nb_ref[0]
    nz = nz_ref[0]
    slot = lax.rem(b, 2)

    def weight_copies(e):
        return (pltpu.make_async_copy(w_in_hbm.at[layer, e], w_in_f32, sem_w.at[0]),
                pltpu.make_async_copy(w_out_hbm.at[layer, e], w_out_f32, sem_w.at[1]))

    def ffn(rows):
        x = xbuf[slot, 0:rows, :]
        h = _dot(x, w_in_bf[...]) + b_in_ref[...]
        g = jnp.minimum(h[:, :D], SWIGLU_LIMIT)
        u = jnp.clip(h[:, D:], -SWIGLU_LIMIT, SWIGLU_LIMIT)
        a = (u + 1.0) * (g * jax.nn.sigmoid(SWIGLU_ALPHA * g))
        y = _dot(a.astype(BF16), w_out_bf[...]) + b_out_ref[...]
        ybuf[slot, 0:rows, :] = y.astype(BF16)

    def chunk_rows(c):
        return pl.ds(pl.multiple_of(c * ROW_CHUNK, ROW_CHUNK), ROW_CHUNK)

    def gather(blk, s):
        for j in range(BLOCK_CHUNKS):
            c = src_ref[blk * BLOCK_CHUNKS + j]
            pltpu.make_async_copy(xs_hbm.at[chunk_rows(c)], xbuf.at[s, pl.ds(j * ROW_CHUNK, ROW_CHUNK)],
                                  sem_in.at[s]).start()

    def gather_wait(s):
        pltpu.make_async_copy(xs_hbm.at[pl.ds(0, BLOCK_ROWS)], xbuf.at[s], sem_in.at[s]).wait()

    def scatter(blk, s):
        for j in range(BLOCK_CHUNKS):
            c = dst_ref[blk * BLOCK_CHUNKS + j]
            pltpu.make_async_copy(ybuf.at[s, pl.ds(j * ROW_CHUNK, ROW_CHUNK)], ys_hbm.at[chunk_rows(c)],
                                  sem_out.at[s]).start()

    def scatter_wait(s):
        pltpu.make_async_copy(ybuf.at[s], ys_hbm.at[pl.ds(0, BLOCK_ROWS)], sem_out.at[s]).wait()

    @pl.when(b == 0)
    def _():
        for c in weight_copies(be_ref[0]):
            c.start()
        gather(0, 0)
        zbuf[...] = jnp.zeros(zbuf.shape, zbuf.dtype)
        ybuf[...] = jnp.zeros(ybuf.shape, ybuf.dtype)

        def zero_one(i, _):
            pltpu.make_async_copy(zbuf, ys_hbm.at[chunk_rows(z_ref[i])], sem_z).start()
            return 0

        lax.fori_loop(0, nz, zero_one, 0)

    @pl.when(b < nb)
    def _():
        new_expert = jnp.logical_or(b == 0, be_ref[b] != be_ref[jnp.maximum(b - 1, 0)])

        @pl.when(new_expert)
        def _():
            for c in weight_copies(be_ref[b]):
                c.wait()
            w_in_bf[...] = w_in_f32[...].astype(BF16)
            w_out_bf[...] = w_out_f32[...].astype(BF16)

            @pl.when(nx_ref[b] >= 0)
            def _():
                for c in weight_copies(nx_ref[b]):
                    c.start()

        gather_wait(slot)

        @pl.when(b + 1 < nb)
        def _():
            gather(b + 1, 1 - slot)

        @pl.when(b >= 2)
        def _():
            scatter_wait(slot)

        @pl.when(half_ref[b] == 0)
        def _():
            ffn(BLOCK_ROWS)

        @pl.when(half_ref[b] != 0)
        def _():
            ffn(HALF_ROWS)

        scatter(b, slot)

        @pl.when(b == 0)
        def _():
            def zero_wait(i, _):
                pltpu.make_async_copy(zbuf, ys_hbm.at[pl.ds(0, ROW_CHUNK)], sem_z).wait()
                return 0

            lax.fori_loop(0, nz, zero_wait, 0)

        @pl.when(b == nb - 1)
        def _():
            scatter_wait(slot)

            @pl.when(b >= 1)
            def _():
                scatter_wait(1 - slot)


def _experts(xs, plan, layer, w_in, b_in, w_out, b_out, nb_max, ys_rows):
    bmap = lambda b, be, *_: (layer, be[b], 0, 0)
    grid_spec = pltpu.PrefetchScalarGridSpec(
        num_scalar_prefetch=8,
        grid=(nb_max,),
        in_specs=[pl.BlockSpec(memory_space=pl.ANY),
                  pl.BlockSpec(memory_space=pl.ANY),
                  pl.BlockSpec((None, None, 1, 2 * D), bmap),
                  pl.BlockSpec(memory_space=pl.ANY),
                  pl.BlockSpec((None, None, 1, D), bmap)],
        out_specs=pl.BlockSpec(memory_space=pl.ANY),
        scratch_shapes=[pltpu.VMEM((2, BLOCK_ROWS, D), BF16), pltpu.VMEM((2, BLOCK_ROWS, D), BF16),
                        pltpu.VMEM((D, 2 * D), F32), pltpu.VMEM((D, D), F32),
                        pltpu.VMEM((D, 2 * D), BF16), pltpu.VMEM((D, D), BF16), pltpu.VMEM((ROW_CHUNK, D), BF16),
                        pltpu.SemaphoreType.DMA((2,)), pltpu.SemaphoreType.DMA((2,)), pltpu.SemaphoreType.DMA(()),
                        pltpu.SemaphoreType.DMA((2,))],
    )
    return pl.pallas_call(
        functools.partial(_expert_kernel, layer=layer),
        out_shape=jax.ShapeDtypeStruct((ys_rows, D), BF16),
        grid_spec=grid_spec,
        compiler_params=_cparams(("arbitrary",)),
        name="experts",
    )(*plan, xs, w_in, b_in, w_out, b_out)


def _combine_kernel(ys_ref, meta_ref, x1_ref, ada_ref, lng_ref, lnb_ref, o_ref):
    tt = x1_ref.shape[0]
    cap = ys_ref.shape[0]
    meta = meta_ref[...]
    col = lax.broadcasted_iota(jnp.int32, (tt, cap), 1).astype(F32)
    gmat = jnp.zeros((tt, cap), F32)
    for k in range(TOP_K):
        gmat = jnp.where(col == meta[:, k:k + 1], meta[:, TOP_K + k:TOP_K + k + 1], gmat)
    f = _dot(gmat.astype(BF16), ys_ref[...])
    ada = ada_ref[0] if len(ada_ref.shape) == 3 else ada_ref[...]
    g2 = ada[:, 5 * D:6 * D]
    o_ref[...] = _ln(ALPHA * x1_ref[...] + (1.0 + g2) * f, lng_ref[...], lnb_ref[...])


def _combine_prompt(ys, meta, x1, ada_p, ln_g, ln_b, seq):
    n = x1.shape[0]
    cap = _tile_capacity(MOE_TILE)
    per_seq = seq // MOE_TILE
    return pl.pallas_call(
        _combine_kernel,
        out_shape=jax.ShapeDtypeStruct((n, D), F32),
        grid=(n // MOE_TILE,),
        in_specs=[pl.BlockSpec((cap, D), lambda i: (i, 0)),
                  pl.BlockSpec((MOE_TILE, LANES), lambda i: (i, 0)),
                  pl.BlockSpec((MOE_TILE, D), lambda i: (i, 0)),
                  pl.BlockSpec((1, 1, 6 * D), lambda i: (i // per_seq, 0, 0)),
                  pl.BlockSpec((1, D), lambda i: (0, 0)),
                  pl.BlockSpec((1, D), lambda i: (0, 0))],
        out_specs=pl.BlockSpec((MOE_TILE, D), lambda i: (i, 0)),
        compiler_params=_cparams(("arbitrary",)),
        name="combine_prompt",
    )(ys, meta, x1, ada_p, ln_g, ln_b)


def _combine_sample(ys, meta, x1, ada_s, ln_g, ln_b, tile):
    n = x1.shape[0]
    cap = _tile_capacity(MOE_TILE)
    return pl.pallas_call(
        _combine_kernel,
        out_shape=jax.ShapeDtypeStruct((n, D), F32),
        grid=(1,),
        in_specs=[pl.BlockSpec((cap, D), lambda i: (tile, 0)),
                  pl.BlockSpec((n, LANES), lambda i: (tile * (MOE_TILE // n), 0)),
                  pl.BlockSpec((n, D), lambda i: (0, 0)),
                  pl.BlockSpec((n, 6 * D), lambda i: (0, 0)),
                  pl.BlockSpec((1, D), lambda i: (0, 0)),
                  pl.BlockSpec((1, D), lambda i: (0, 0))],
        out_specs=pl.BlockSpec((n, D), lambda i: (0, 0)),
        compiler_params=_cparams(("arbitrary",)),
        name="combine_sample",
    )(ys, meta, x1, ada_s, ln_g, ln_b)


def _moe_and_norm(h2_p, h2_s, lg_p, lg_s, x1_p, x1_s, ada_p, ada_s, ln_g, ln_b,
                  layer, w_in, b_in, w_out, b_out, seq):
    n_p, n_s = h2_p.shape[0], h2_s.shape[0]
    n_valid = n_p + n_s
    nt = n_p // MOE_TILE + 1
    cap = _tile_capacity(MOE_TILE)
    cap_chunks = cap // ROW_CHUNK
    xs, meta, cnt = _route(h2_p, h2_s, lg_p, lg_s)
    total_chunks_max = (TOP_K * n_valid + nt * N_EXPERTS * (ROW_CHUNK - 1)) // ROW_CHUNK
    nb_max = -(-total_chunks_max // BLOCK_CHUNKS) + N_EXPERTS
    max_pad = N_EXPERTS * (BLOCK_CHUNKS - 1)
    nz_max = nt * cap_chunks - (TOP_K * n_valid) // ROW_CHUNK + max_pad
    plan = _plan(cnt[:, :, 0].reshape(-1), nt, cap_chunks, nb_max, max_pad, nz_max)
    ys = _experts(xs, plan, layer, w_in, b_in, w_out, b_out, nb_max, (nt * cap_chunks + max_pad) * ROW_CHUNK)
    x2_p = _combine_prompt(ys, meta, x1_p, ada_p, ln_g, ln_b, seq)
    x2_s = _combine_sample(ys, meta, x1_s, ada_s, ln_g, ln_b, n_p // MOE_TILE)
    return x2_p, x2_s


def _row(v):
    return v.reshape(1, -1)


def kernel(x_prompt, x_sample, state_conv_a, state_pool_b, state_conv_c, c_prompt, c_sample,
           w_ada, b_ada, ln1_g, ln1_b, ln2_g, ln2_b,
           a_w_in, a_b_in, a_w_dw, a_b_dw, a_ln_g, a_ln_b, a_w_out,
           b_w_grp, b_scale, c_w_in, c_w_conv, c_w_out,
           d_w_in, d_b_in, d_ln_g, d_ln_b, d_w_s, d_b_s, d_w_out,
           w_router, b_router, w_moe_in, b_moe_in, w_moe_out, b_moe_out):
    bp, seq, _ = x_prompt.shape
    bs = x_sample.shape[0]
    assert seq % TOK_TILE == 0 and seq % MOE_TILE == 0 and MOE_TILE % bs == 0
    assert x_sample.shape[1] == 1 and w_ada.shape[0] == DEPTH == 4

    ada = _ada(jnp.concatenate([c_prompt, c_sample], axis=0), w_ada, b_ada)
    xp = x_prompt.reshape(bp * seq, D)
    xs = x_sample.reshape(bs, D)
    b_moe_in4 = b_moe_in.reshape(DEPTH, N_EXPERTS, 1, 2 * D)
    b_moe_out4 = b_moe_out.reshape(DEPTH, N_EXPERTS, 1, D)
    states = {}

    for i in range(DEPTH):
        ada_p = ada[i, :bp].reshape(bp, 1, 6 * D)
        ada_s = ada[i, bp:]
        wr = jnp.pad(w_router[i], ((0, 0), (0, LANES - N_EXPERTS)))
        ln1 = [_row(ln1_g[i]), _row(ln1_b[i]), wr, _row(jnp.pad(b_router[i], (0, LANES - N_EXPERTS)))]
        if i == 0:
            wts = [a_w_in[0].astype(BF16), _row(a_b_in[0]), a_w_dw[0], _row(a_b_dw[0]), _row(a_ln_g[0]),
                   _row(a_ln_b[0]), a_w_out[0].astype(BF16)] + ln1
            x1_p, h2_p, lg_p, st =_mix_prompt(_mix_a_prompt_kernel, "mix_a_prompt", xp, ada_p, wts, A_HALO,
                                         [pltpu.VMEM((A_HALO + TOK_TILE, D), F32)], seq)
            states["a_p"] = st[:, A_HALO - (CONV_A_WIDTH - 1):][None]
            hist = jnp.transpose(state_conv_a[0], (1, 0, 2))
            x1_s, h2_s, lg_s, new =_mix_sample(_mix_a_sample_kernel, "mix_a_sample", xs, ada_s, [hist] + wts)
            states["a_s"] = jnp.concatenate([state_conv_a[0][:, 1:], new[:, None]], axis=1)[None]
        elif i == 1:
            wts = [b_w_grp[0], _row(b_scale[0])] + ln1
            x1_p, h2_p, lg_p, st =_mix_prompt(_mix_b_prompt_kernel, "mix_b_prompt", xp, ada_p, wts, B_HALO,
                                         [pltpu.VMEM((B_HALO + TOK_TILE, D), F32)], seq)
            states["b_p"] = st[:, B_HALO - POOL_HIST:][None]
            hist = jnp.transpose(state_pool_b[0], (1, 0, 2))
            x1_s, h2_s, lg_s, new =_mix_sample(_mix_b_sample_kernel, "mix_b_sample", xs, ada_s, [hist] + wts)
            states["b_s"] = jnp.concatenate([state_pool_b[0][:, 1:], new[:, None]], axis=1)[None]
        elif i == 2:
            wts = [c_w_in[0].astype(BF16), c_w_conv[0], c_w_out[0].astype(BF16)] + ln1
            x1_p, h2_p, lg_p, st =_mix_prompt(_mix_c_prompt_kernel, "mix_c_prompt", xp, ada_p, wts, C_HALO,
                                         [pltpu.VMEM((C_HALO + TOK_TILE, D), F32)], seq)
            states["c_p"] = st[:, C_HALO - (CONV_C_WIDTH - 1):][None]
            hist = jnp.transpose(state_conv_c[0], (1, 0, 2))
            x1_s, h2_s, lg_s, new =_mix_sample(_mix_c_sample_kernel, "mix_c_sample", xs, ada_s, [hist] + wts)
            states["c_s"] = jnp.concatenate([state_conv_c[0][:, 1:], new[:, None]], axis=1)[None]
        else:
            common = [d_w_in[0].astype(BF16), _row(d_b_in[0]), _row(d_ln_g[0]), _row(d_ln_b[0])]
            wts = common + [d_w_s[0], d_b_s[0].T, d_w_out[0].astype(BF16)] + ln1
            x1_p, h2_p, lg_p, st =_mix_prompt(_mix_d_prompt_kernel, "mix_d_prompt", xp, ada_p, wts, CHUNK, [], seq)
            states["d_p"] = st[None]
            w_s0 = jnp.repeat(d_w_s[0][:, 0, 0], SGU_GROUP).reshape(1, D)
            b_s0 = jnp.repeat(d_b_s[0][:, 0], SGU_GROUP).reshape(1, D)
            wts_s = common + [w_s0, b_s0, d_w_out[0].astype(BF16)] + ln1
            x1_s, h2_s, lg_s, new =_mix_sample(_mix_d_sample_kernel, "mix_d_sample", xs, ada_s, wts_s)
            states["d_s"] = new[:, None][None]

        xp, xs = _moe_and_norm(h2_p, h2_s, lg_p, lg_s, x1_p, x1_s, ada_p, ada_s, _row(ln2_g[i]), _row(ln2_b[i]),
                               i, w_moe_in, b_moe_in4, w_moe_out, b_moe_out4, seq)

    return (xp.reshape(bp, seq, D), xs.reshape(bs, 1, D),
            states["a_p"], states["a_s"], states["b_p"], states["b_s"],
            states["c_p"], states["c_s"], states["d_p"], states["d_s"])
```

```python
import functools

import jax
import jax.numpy as jnp
from jax import lax
from jax.experimental import pallas as pl
from jax.experimental.pallas import tpu as pltpu

F32 = jnp.float32
BF16 = jnp.bfloat16

D = 1024
DEPTH = 4
N_EXPERTS = 32
TOP_K = 4
CONV_A_WIDTH = 31
POOL_WINDOWS = (2, 4, 8, 16)
POOL_GROUP = D // 4
POOL_HIST = 15
CONV_C_WIDTH = 3
CHUNK = 128
N_SGU_GROUPS = 4
SGU_GROUP = D // N_SGU_GROUPS
SWIGLU_LIMIT = 7.0
SWIGLU_ALPHA = 1.702
ALPHA = (2 * DEPTH) ** 0.25
LN_EPS = 1e-5

LANES = 128
SUBLANES_BF16 = 16
VMEM_LIMIT = 56 * 1024 * 1024

TOK_TILE = 512
MOE_TILE = 512
ROW_CHUNK = SUBLANES_BF16
BLOCK_CHUNKS = 32
BLOCK_ROWS = BLOCK_CHUNKS * ROW_CHUNK
HALF_CHUNKS = BLOCK_CHUNKS // 2
HALF_ROWS = BLOCK_ROWS // 2


def _tile_capacity(tt):
    cap = TOP_K * tt + N_EXPERTS * (ROW_CHUNK - 1)
    return -(-cap // 256) * 256


def _cparams(sem=None):
    return pltpu.CompilerParams(dimension_semantics=sem, vmem_limit_bytes=VMEM_LIMIT)


def _dot(a, b):
    return jnp.dot(a, b, preferred_element_type=F32)


def _dot_split(a, b):
    hi = a.astype(BF16)
    lo = (a - hi.astype(F32)).astype(BF16)
    return _dot(hi, b) + _dot(lo, b)


def _ln(x, g, b):
    mu = jnp.mean(x, axis=-1, keepdims=True)
    xc = x - mu
    var = jnp.mean(xc * xc, axis=-1, keepdims=True)
    return xc * lax.rsqrt(var + LN_EPS) * g + b


def _split_ada(ada):
    return [ada[:, i * D:(i + 1) * D] for i in range(6)]


def _post_mixer(x, out, ada, ln_g, ln_b, wr_ref, br_ref, x1_ref, h2_ref, lg_ref):
    _, _, g1, sh2, sc2, _ = _split_ada(ada)
    x1 = _ln(ALPHA * x + (1.0 + g1) * out, ln_g, ln_b)
    x1_ref[...] = x1
    h2 = x1 * (1.0 + sc2) + sh2
    hi = h2.astype(BF16)
    h2_ref[...] = hi
    lo = (h2 - hi.astype(F32)).astype(BF16)
    wr = wr_ref[...]
    w_hi = wr.astype(BF16)
    w_lo = (wr - w_hi.astype(F32)).astype(BF16)
    lg_ref[...] = _dot(hi, w_hi) + _dot(lo, w_hi) + _dot(hi, w_lo) + br_ref[...]


def _ada_kernel(c_ref, w_ref, b_ref, o_ref):
    c = c_ref[...]
    sc = (c * jax.nn.sigmoid(c)).astype(BF16)
    o_ref[0] = _dot(sc, w_ref[0].astype(BF16)) + b_ref[0]


def _ada(c_all, w_ada, b_ada):
    rows = c_all.shape[0]
    return pl.pallas_call(
        _ada_kernel,
        out_shape=jax.ShapeDtypeStruct((DEPTH, rows, 6 * D), F32),
        grid=(DEPTH, 6),
        in_specs=[
            pl.BlockSpec((rows, D), lambda i, j: (0, 0)),
            pl.BlockSpec((1, D, D), lambda i, j: (i, 0, j)),
            pl.BlockSpec((1, 1, D), lambda i, j: (i, 0, j)),
        ],
        out_specs=pl.BlockSpec((1, rows, D), lambda i, j: (i, 0, j)),
        compiler_params=_cparams(("arbitrary", "arbitrary")),
        name="ada",
    )(c_all, w_ada, b_ada.reshape(DEPTH, 1, 6 * D))


A_HALO = 32
B_HALO = 16
C_HALO = 8


def _mix_a_prompt_kernel(x_ref, ada_ref, w_in_ref, b_in_ref, w_dw_ref, b_dw_ref, lng_ref, lnb_ref,
                         w_out_ref, ln1g_ref, ln1b_ref, wr_ref, br_ref, x1_ref, h2_ref, lg_ref, st_ref, u_scr):
    t = pl.program_id(1)
    tt = x_ref.shape[0]
    x = x_ref[...]
    ada = ada_ref[0]
    sh1, sc1 = ada[:, 0:D], ada[:, D:2 * D]
    h = (x * (1.0 + sc1) + sh1).astype(BF16)
    z = _dot(h, w_in_ref[...]) + b_in_ref[...]
    u = z[:, :D] * jax.nn.sigmoid(z[:, D:])

    @pl.when(t == 0)
    def _():
        u_scr[0:A_HALO, :] = jnp.zeros((A_HALO, D), F32)

    u_scr[A_HALO:A_HALO + tt, :] = u
    first = A_HALO - (CONV_A_WIDTH - 1)
    y = jnp.zeros((tt, D), F32) + b_dw_ref[...]
    for k in range(CONV_A_WIDTH):
        y = y + w_dw_ref[k:k + 1, :] * u_scr[first + k:first + k + tt, :]
    y = _ln(y, lng_ref[...], lnb_ref[...])
    y = y * jax.nn.sigmoid(y)
    out = _dot(y.astype(BF16), w_out_ref[...])
    _post_mixer(x, out, ada, ln1g_ref[...], ln1b_ref[...], wr_ref, br_ref, x1_ref, h2_ref, lg_ref)
    tail = u_scr[tt:tt + A_HALO, :]
    u_scr[0:A_HALO, :] = tail

    @pl.when(t == pl.num_programs(1) - 1)
    def _():
        st_ref[0] = tail


def _mix_b_prompt_kernel(x_ref, ada_ref, w_grp_ref, scale_ref, ln1g_ref, ln1b_ref,
                         wr_ref, br_ref, x1_ref, h2_ref, lg_ref, st_ref, h_scr):
    t = pl.program_id(1)
    tt = x_ref.shape[0]
    x = x_ref[...]
    ada = ada_ref[0]
    sh1, sc1 = ada[:, 0:D], ada[:, D:2 * D]
    h = x * (1.0 + sc1) + sh1

    @pl.when(t == 0)
    def _():
        h_scr[0:B_HALO, :] = jnp.zeros((B_HALO, D), F32)

    h_scr[B_HALO:B_HALO + tt, :] = h
    pos = (t * tt + lax.broadcasted_iota(jnp.int32, (tt, POOL_GROUP), 0)).astype(F32)
    outs = []
    for gi, w in enumerate(POOL_WINDOWS):
        lo = gi * POOL_GROUP
        s = h_scr[B_HALO:B_HALO + tt, lo:lo + POOL_GROUP]
        for j in range(1, w):
            s = s + h_scr[B_HALO - j:B_HALO - j + tt, lo:lo + POOL_GROUP]
        cnt = jnp.minimum(float(w), pos + 1.0)
        pooled = s / cnt - h[:, lo:lo + POOL_GROUP]
        outs.append(_dot_split(pooled, w_grp_ref[gi].astype(BF16)))
    out = jnp.concatenate(outs, axis=-1) * scale_ref[...]
    _post_mixer(x, out, ada, ln1g_ref[...], ln1b_ref[...], wr_ref, br_ref, x1_ref, h2_ref, lg_ref)
    tail = h_scr[tt:tt + B_HALO, :]
    h_scr[0:B_HALO, :] = tail

    @pl.when(t == pl.num_programs(1) - 1)
    def _():
        st_ref[0] = tail


def _mix_c_prompt_kernel(x_ref, ada_ref, w_in_ref, w_conv_ref, w_out_ref, ln1g_ref, ln1b_ref,
                         wr_ref, br_ref, x1_ref, h2_ref, lg_ref, st_ref, v_scr):
    t = pl.program_id(1)
    tt = x_ref.shape[0]
    x = x_ref[...]
    ada = ada_ref[0]
    sh1, sc1 = ada[:, 0:D], ada[:, D:2 * D]
    h = (x * (1.0 + sc1) + sh1).astype(BF16)
    z = _dot(h, w_in_ref[...])
    bg = z[:, :D]
    v = z[:, D:2 * D] * z[:, 2 * D:]

    @pl.when(t == 0)
    def _():
        v_scr[0:C_HALO, :] = jnp.zeros((C_HALO, D), F32)

    v_scr[C_HALO:C_HALO + tt, :] = v
    first = C_HALO - (CONV_C_WIDTH - 1)
    y = w_conv_ref[CONV_C_WIDTH - 1:CONV_C_WIDTH, :] * v
    for k in range(CONV_C_WIDTH - 1):
        y = y + w_conv_ref[k:k + 1, :] * v_scr[first + k:first + k + tt, :]
    out = _dot((bg * y).astype(BF16), w_out_ref[...])
    _post_mixer(x, out, ada, ln1g_ref[...], ln1b_ref[...], wr_ref, br_ref, x1_ref, h2_ref, lg_ref)
    tail = v_scr[tt:tt + C_HALO, :]
    v_scr[0:C_HALO, :] = tail

    @pl.when(t == pl.num_programs(1) - 1)
    def _():
        st_ref[0] = tail


def _gelu_exact(x):
    return 0.5 * x * (1.0 + lax.erf(x * (2.0 ** -0.5)))


def _mix_d_prompt_kernel(x_ref, ada_ref, w_in_ref, b_in_ref, lng_ref, lnb_ref, w_s_ref, b_st_ref,
                         w_out_ref, ln1g_ref, ln1b_ref, wr_ref, br_ref, x1_ref, h2_ref, lg_ref, st_ref):
    t = pl.program_id(1)
    tt = x_ref.shape[0]
    x = x_ref[...]
    ada = ada_ref[0]
    sh1, sc1 = ada[:, 0:D], ada[:, D:2 * D]
    h = (x * (1.0 + sc1) + sh1).astype(BF16)
    z = _gelu_exact(_dot(h, w_in_ref[...]) + b_in_ref[...])
    u = z[:, :D]
    v = _ln(z[:, D:], lng_ref[...], lnb_ref[...])
    vb = v.astype(BF16)
    row = lax.broadcasted_iota(jnp.int32, (CHUNK, CHUNK), 0)
    col = lax.broadcasted_iota(jnp.int32, (CHUNK, CHUNK), 1)
    causal = col <= row
    chunks = []
    for c in range(tt // CHUNK):
        groups = []
        for g in range(N_SGU_GROUPS):
            w = jnp.where(causal, w_s_ref[g], 0.0).astype(BF16)
            m = _dot(w, vb[c * CHUNK:(c + 1) * CHUNK, g * SGU_GROUP:(g + 1) * SGU_GROUP])
            groups.append(m + b_st_ref[:, g:g + 1])
        chunks.append(jnp.concatenate(groups, axis=-1))
    mixed = jnp.concatenate(chunks, axis=0)
    out = _dot((u * mixed).astype(BF16), w_out_ref[...])
    _post_mixer(x, out, ada, ln1g_ref[...], ln1b_ref[...], wr_ref, br_ref, x1_ref, h2_ref, lg_ref)

    @pl.when(t == pl.num_programs(1) - 1)
    def _():
        st_ref[0] = v[tt - CHUNK:, :]


def _full(shape):
    nd = len(shape)
    return pl.BlockSpec(shape, lambda b, t: (0,) * nd)


def _mix_prompt(kernel_fn, name, x, ada_p, weights, state_rows, scratch, seq):
    n = x.shape[0]
    batch = n // seq
    nt = seq // TOK_TILE
    tok = pl.BlockSpec((TOK_TILE, D), lambda b, t: (b * nt + t, 0))
    lgt = pl.BlockSpec((TOK_TILE, LANES), lambda b, t: (b * nt + t, 0))
    return pl.pallas_call(
        kernel_fn,
        out_shape=(jax.ShapeDtypeStruct((n, D), F32), jax.ShapeDtypeStruct((n, D), BF16),
                   jax.ShapeDtypeStruct((n, LANES), F32), jax.ShapeDtypeStruct((batch, state_rows, D), F32)),
        grid=(batch, nt),
        in_specs=[tok, pl.BlockSpec((1, 1, 6 * D), lambda b, t: (b, 0, 0))] + [_full(w.shape) for w in weights],
        out_specs=(tok, tok, lgt, pl.BlockSpec((1, state_rows, D), lambda b, t: (b, 0, 0))),
        scratch_shapes=scratch,
        compiler_params=_cparams(("arbitrary", "arbitrary")),
        name=name,
    )(x, ada_p, *weights)


def _mix_a_sample_kernel(x_ref, ada_ref, hist_ref, w_in_ref, b_in_ref, w_dw_ref, b_dw_ref, lng_ref, lnb_ref,
                         w_out_ref, ln1g_ref, ln1b_ref, wr_ref, br_ref, x1_ref, h2_ref, lg_ref, new_ref):
    x = x_ref[...]
    ada = ada_ref[...]
    sh1, sc1 = ada[:, 0:D], ada[:, D:2 * D]
    h = (x * (1.0 + sc1) + sh1).astype(BF16)
    z = _dot(h, w_in_ref[...]) + b_in_ref[...]
    u = z[:, :D] * jax.nn.sigmoid(z[:, D:])
    y = b_dw_ref[...] + w_dw_ref[CONV_A_WIDTH - 1:CONV_A_WIDTH, :] * u
    for k in range(CONV_A_WIDTH - 1):
        y = y + w_dw_ref[k:k + 1, :] * hist_ref[k]
    y = _ln(y, lng_ref[...], lnb_ref[...])
    y = y * jax.nn.sigmoid(y)
    out = _dot(y.astype(BF16), w_out_ref[...])
    _post_mixer(x, out, ada, ln1g_ref[...], ln1b_ref[...], wr_ref, br_ref, x1_ref, h2_ref, lg_ref)
    new_ref[...] = u


def _mix_b_sample_kernel(x_ref, ada_ref, hist_ref, w_grp_ref, scale_ref, ln1g_ref, ln1b_ref,
                         wr_ref, br_ref, x1_ref, h2_ref, lg_ref, new_ref):
    x = x_ref[...]
    ada = ada_ref[...]
    sh1, sc1 = ada[:, 0:D], ada[:, D:2 * D]
    h = x * (1.0 + sc1) + sh1
    outs = []
    for gi, w in enumerate(POOL_WINDOWS):
        lo = gi * POOL_GROUP
        s = h[:, lo:lo + POOL_GROUP]
        for j in range(1, w):
            s = s + hist_ref[POOL_HIST - j][:, lo:lo + POOL_GROUP]
        pooled = s / float(w) - h[:, lo:lo + POOL_GROUP]
        outs.append(_dot_split(pooled, w_grp_ref[gi].astype(BF16)))
    out = jnp.concatenate(outs, axis=-1) * scale_ref[...]
    _post_mixer(x, out, ada, ln1g_ref[...], ln1b_ref[...], wr_ref, br_ref, x1_ref, h2_ref, lg_ref)
    new_ref[...] = h


def _mix_c_sample_kernel(x_ref, ada_ref, hist_ref, w_in_ref, w_conv_ref, w_out_ref, ln1g_ref, ln1b_ref,
                         wr_ref, br_ref, x1_ref, h2_ref, lg_ref, new_ref):
    x = x_ref[...]
    ada = ada_ref[...]
    sh1, sc1 = ada[:, 0:D], ada[:, D:2 * D]
    h = (x * (1.0 + sc1) + sh1).astype(BF16)
    z = _dot(h, w_in_ref[...])
    bg = z[:, :D]
    v = z[:, D:2 * D] * z[:, 2 * D:]
    y = w_conv_ref[CONV_C_WIDTH - 1:CONV_C_WIDTH, :] * v
    for k in range(CONV_C_WIDTH - 1):
        y = y + w_conv_ref[k:k + 1, :] * hist_ref[k]
    out = _dot((bg * y).astype(BF16), w_out_ref[...])
    _post_mixer(x, out, ada, ln1g_ref[...], ln1b_ref[...], wr_ref, br_ref, x1_ref, h2_ref, lg_ref)
    new_ref[...] = v


def _mix_d_sample_kernel(x_ref, ada_ref, w_in_ref, b_in_ref, lng_ref, lnb_ref, w_s0_ref, b_s0_ref,
                         w_out_ref, ln1g_ref, ln1b_ref, wr_ref, br_ref, x1_ref, h2_ref, lg_ref, new_ref):
    x = x_ref[...]
    ada = ada_ref[...]
    sh1, sc1 = ada[:, 0:D], ada[:, D:2 * D]
    h = (x * (1.0 + sc1) + sh1).astype(BF16)
    z = _gelu_exact(_dot(h, w_in_ref[...]) + b_in_ref[...])
    u = z[:, :D]
    v = _ln(z[:, D:], lng_ref[...], lnb_ref[...])
    mixed = w_s0_ref[...] * v + b_s0_ref[...]
    out = _dot((u * mixed).astype(BF16), w_out_ref[...])
    _post_mixer(x, out, ada, ln1g_ref[...], ln1b_ref[...], wr_ref, br_ref, x1_ref, h2_ref, lg_ref)
    new_ref[...] = v


def _mix_sample(kernel_fn, name, x, ada_s, arrays):
    n = x.shape[0]
    return pl.pallas_call(
        kernel_fn,
        out_shape=(jax.ShapeDtypeStruct((n, D), F32), jax.ShapeDtypeStruct((n, D), BF16),
                   jax.ShapeDtypeStruct((n, LANES), F32), jax.ShapeDtypeStruct((n, D), F32)),
        compiler_params=_cparams(),
        name=name,
    )(x, ada_s, *arrays)


def _route_kernel(hp_ref, hs_ref, lgp_ref, lgs_ref, xs_ref, meta_ref, cnt_ref, *, n_valid):
    i = pl.program_id(0)
    tt = hp_ref.shape[0]
    ns = hs_ref.shape[0]
    cap = xs_ref.shape[0]
    is_sample = i == pl.num_programs(0) - 1
    hs = jnp.concatenate([hs_ref[...], jnp.zeros((tt - ns, D), BF16)], axis=0)
    lgs = jnp.concatenate([lgs_ref[...], jnp.zeros((tt - ns, LANES), F32)], axis=0)
    hb = jnp.where(is_sample, hs, hp_ref[...])
    logits = jnp.where(is_sample, lgs, lgp_ref[...]).T[:N_EXPERTS, :]
    e_iota = lax.broadcasted_iota(jnp.int32, (N_EXPERTS, tt), 0)
    valid = (i * tt + lax.broadcasted_iota(jnp.int32, (1, tt), 1)) < n_valid
    work = logits
    sel, top = [], []
    for _ in range(TOP_K):
        m = jnp.max(work, axis=0, keepdims=True)
        idx = jnp.min(jnp.where(work == m, e_iota, N_EXPERTS), axis=0, keepdims=True)
        oh = e_iota == idx
        sel.append(oh)
        top.append(m)
        work = jnp.where(oh, -jnp.inf, work)
    ex = [jnp.exp(v - top[0]) for v in top]
    denom = ex[0] + ex[1] + ex[2] + ex[3]
    gates = [e / denom for e in ex]

    member = jnp.where(sel[0] | sel[1] | sel[2] | sel[3], 1.0, 0.0)
    member = jnp.where(valid, member, 0.0)
    r_i = lax.broadcasted_iota(jnp.int32, (tt, tt), 0)
    c_i = lax.broadcasted_iota(jnp.int32, (tt, tt), 1)
    before = jnp.where(r_i < c_i, 1.0, 0.0).astype(BF16)
    rank = _dot(member.astype(BF16), before)
    count = jnp.sum(member, axis=1, keepdims=True)
    chunks = jnp.floor((count + float(ROW_CHUNK - 1)) * (1.0 / ROW_CHUNK))
    chunks_b = jnp.broadcast_to(chunks, (N_EXPERTS, LANES))
    er = lax.broadcasted_iota(jnp.int32, (N_EXPERTS, N_EXPERTS), 0)
    ec = lax.broadcasted_iota(jnp.int32, (N_EXPERTS, N_EXPERTS), 1)
    lower = jnp.where(ec < er, 1.0, 0.0).astype(BF16)
    chunk_off = _dot(lower, chunks_b.astype(BF16))
    base = chunk_off[:, 0:1] * float(ROW_CHUNK)
    slot = base + rank
    pos = [jnp.where(valid, jnp.sum(jnp.where(s, slot, 0.0), axis=0, keepdims=True), -1.0) for s in sel]

    row = lax.broadcasted_iota(jnp.int32, (cap, tt), 0).astype(F32)
    onehot = jnp.where(row == pos[0], 1.0, 0.0)
    for k in range(1, TOP_K):
        onehot = jnp.where(row == pos[k], 1.0, onehot)
    xs_ref[...] = _dot(onehot.astype(BF16), hb).astype(BF16)

    meta = jnp.concatenate(pos + gates + [jnp.zeros((LANES - 2 * TOP_K, tt), F32)], axis=0)
    meta_ref[...] = meta.T
    cnt_ref[0] = chunks_b.astype(jnp.int32)


def _route(h2_p, h2_s, lg_p, lg_s):
    n_p, n_s = h2_p.shape[0], h2_s.shape[0]
    ntp = n_p // MOE_TILE
    nt = ntp + 1
    cap = _tile_capacity(MOE_TILE)
    prompt_tile = lambda i: (jnp.minimum(i, ntp - 1), 0)
    return pl.pallas_call(
        functools.partial(_route_kernel, n_valid=n_p + n_s),
        out_shape=(jax.ShapeDtypeStruct((nt * cap, D), BF16),
                   jax.ShapeDtypeStruct((nt * MOE_TILE, LANES), F32),
                   jax.ShapeDtypeStruct((nt, N_EXPERTS, LANES), jnp.int32)),
        grid=(nt,),
        in_specs=[pl.BlockSpec((MOE_TILE, D), prompt_tile),
                  pl.BlockSpec((n_s, D), lambda i: (0, 0)),
                  pl.BlockSpec((MOE_TILE, LANES), prompt_tile),
                  pl.BlockSpec((n_s, LANES), lambda i: (0, 0))],
        out_specs=(pl.BlockSpec((cap, D), lambda i: (i, 0)),
                   pl.BlockSpec((MOE_TILE, LANES), lambda i: (i, 0)),
                   pl.BlockSpec((1, N_EXPERTS, LANES), lambda i: (i, 0, 0))),
        compiler_params=_cparams(("arbitrary",)),
        name="route_dispatch",
    )(h2_p, h2_s, lg_p, lg_s)


def _plan_kernel(cnt_ref, be_ref, nx_ref, half_ref, nb_ref, src_ref, dst_ref, z_ref, nz_ref, tile_off, blk0_s,
                 nblk_s, npad_s, *, nt, cap_chunks, nb_max, max_pad, nz_max):
    del nb_max, nz_max
    i32 = jnp.int32
    trash_base = nt * cap_chunks
    shift = BLOCK_CHUNKS.bit_length() - 1

    def index_grid(ref):
        r, c = ref.shape
        return lax.broadcasted_iota(i32, (r, c), 0) * c + lax.broadcasted_iota(i32, (r, c), 1)

    def in_run(idx, start, length):
        return jnp.logical_and(idx >= start, idx < start + length)

    for t in range(nt):
        tile_off[t] = 0

    d = index_grid(src_ref)

    def expert_body(e, carry):
        p0, blk0, pad0, src, dst = carry

        def tile_body(t, inner):
            p, src, dst = inner
            c = cnt_ref[t * N_EXPERTS + e]
            off = tile_off[t]
            tile_off[t] = off + c
            run = in_run(d, p, c)
            val = d + (t * cap_chunks + off - p)
            return p + c, jnp.where(run, val, src), jnp.where(run, val, dst)

        p1, src, dst = lax.fori_loop(0, nt, tile_body, (p0, src, dst))
        nblk = lax.shift_right_logical(p1 - p0 + (BLOCK_CHUNKS - 1), shift)
        p2 = p0 + nblk * BLOCK_CHUNKS
        pad = in_run(d, p1, p2 - p1)
        src = jnp.where(pad, 0, src)
        dst = jnp.where(pad, d + (trash_base + pad0 - p1), dst)
        blk0_s[e] = blk0
        nblk_s[e] = nblk
        npad_s[e] = p2 - p1
        return p2, blk0 + nblk, pad0 + (p2 - p1), src, dst

    zeros = jnp.zeros(src_ref.shape, i32)
    zero = jnp.int32(0)
    _, nb, n_pad, src, dst = lax.fori_loop(0, N_EXPERTS, expert_body, (zero, zero, zero, zeros, zeros))
    src_ref[...] = src
    dst_ref[...] = dst
    nb_ref[0] = nb

    bi = index_grid(be_ref)

    def block_body(k, carry):
        be, nx, half, nxt = carry
        e = N_EXPERTS - 1 - k
        b0, n = blk0_s[e], nblk_s[e]
        mine = in_run(bi, b0, n)
        last = jnp.logical_and(bi == b0 + n - 1, jnp.logical_and(n > 0, npad_s[e] >= HALF_CHUNKS))
        return (jnp.where(mine, e, be), jnp.where(mine, nxt, nx), jnp.where(last, 1, half),
                jnp.where(n > 0, e, nxt))

    init = (jnp.full(be_ref.shape, N_EXPERTS - 1, i32), jnp.full(be_ref.shape, -1, i32),
            jnp.zeros(be_ref.shape, i32), jnp.int32(-1))
    be, nx, half, _ = lax.fori_loop(0, N_EXPERTS, block_body, init)
    be_ref[...] = be
    nx_ref[...] = nx
    half_ref[...] = half

    zi = index_grid(z_ref)

    def tail_body(t, carry):
        z0, z = carry
        used = tile_off[t]
        n = cap_chunks - used
        return z0 + n, jnp.where(in_run(zi, z0, n), zi + (t * cap_chunks + used - z0), z)

    z1, z = lax.fori_loop(0, nt, tail_body, (zero, jnp.zeros(z_ref.shape, i32)))
    n_trash = max_pad - n_pad
    z_ref[...] = jnp.where(in_run(zi, z1, n_trash), zi + (trash_base + n_pad - z1), z)
    nz_ref[0] = z1 + n_trash


def _plan(cnt_flat, nt, cap_chunks, nb_max, max_pad, nz_max):
    smem = pl.BlockSpec(memory_space=pltpu.SMEM)
    vmem = pl.BlockSpec(memory_space=pltpu.VMEM)
    i32 = jnp.int32

    def table(n):
        return jax.ShapeDtypeStruct((-(-n // (8 * LANES)) * 8, LANES), i32)

    out = pl.pallas_call(
        functools.partial(_plan_kernel, nt=nt, cap_chunks=cap_chunks, nb_max=nb_max, max_pad=max_pad, nz_max=nz_max),
        out_shape=(table(nb_max), table(nb_max), table(nb_max), jax.ShapeDtypeStruct((1,), i32),
                   table(nb_max * BLOCK_CHUNKS), table(nb_max * BLOCK_CHUNKS),
                   table(nz_max), jax.ShapeDtypeStruct((1,), i32)),
        in_specs=[smem],
        out_specs=(vmem, vmem, vmem, smem, vmem, vmem, vmem, smem),
        scratch_shapes=[pltpu.SMEM((nt,), i32), pltpu.SMEM((N_EXPERTS,), i32), pltpu.SMEM((N_EXPERTS,), i32),
                        pltpu.SMEM((N_EXPERTS,), i32)],
        name="plan",
    )(cnt_flat)
    return tuple(o.reshape(-1) for o in out)


def _expert_kernel(be_ref, nx_ref, half_ref, nb_ref, src_ref, dst_ref, z_ref, nz_ref, xs_hbm, w_in_hbm, b_in_ref,
                   w_out_hbm, b_out_ref, ys_hbm, xbuf, ybuf, w_in_f32, w_out_f32, w_in_bf, w_out_bf, zbuf,
                   sem_in, sem_out, sem_z, sem_w, *, layer):
    b = pl.program_id(0)
    nb = nb_ref[0]
    nz = nz_ref[0]
    slot = lax.rem(b, 2)

    def weight_copies(e):
        return (pltpu.make_async_copy(w_in_hbm.at[layer, e], w_in_f32, sem_w.at[0]),
                pltpu.make_async_copy(w_out_hbm.at[layer, e], w_out_f32, sem_w.at[1]))

    def ffn(rows):
        x = xbuf[slot, 0:rows, :]
        h = _dot(x, w_in_bf[...]) + b_in_ref[...]
        g = jnp.minimum(h[:, :D], SWIGLU_LIMIT)
        u = jnp.clip(h[:, D:], -SWIGLU_LIMIT, SWIGLU_LIMIT)
        a = (u + 1.0) * (g * jax.nn.sigmoid(SWIGLU_ALPHA * g))
        y = _dot(a.astype(BF16), w_out_bf[...]) + b_out_ref[...]
        ybuf[slot, 0:rows, :] = y.astype(BF16)

    def chunk_rows(c):
        return pl.ds(pl.multiple_of(c * ROW_CHUNK, ROW_CHUNK), ROW_CHUNK)

    def gather(blk, s):
        for j in range(BLOCK_CHUNKS):
            c = src_ref[blk * BLOCK_CHUNKS + j]
            pltpu.make_async_copy(xs_hbm.at[chunk_rows(c)], xbuf.at[s, pl.ds(j * ROW_CHUNK, ROW_CHUNK)],
                                  sem_in.at[s]).start()

    def gather_wait(s):
        pltpu.make_async_copy(xs_hbm.at[pl.ds(0, BLOCK_ROWS)], xbuf.at[s], sem_in.at[s]).wait()

    def scatter(blk, s):
        for j in range(BLOCK_CHUNKS):
            c = dst_ref[blk * BLOCK_CHUNKS + j]
            pltpu.make_async_copy(ybuf.at[s, pl.ds(j * ROW_CHUNK, ROW_CHUNK)], ys_hbm.at[chunk_rows(c)],
                                  sem_out.at[s]).start()

    def scatter_wait(s):
        pltpu.make_async_copy(ybuf.at[s], ys_hbm.at[pl.ds(0, BLOCK_ROWS)], sem_out.at[s]).wait()

    @pl.when(b == 0)
    def _():
        for c in weight_copies(be_ref[0]):
            c.start()
        gather(0, 0)
        zbuf[...] = jnp.zeros(zbuf.shape, zbuf.dtype)
        ybuf[...] = jnp.zeros(ybuf.shape, ybuf.dtype)

        def zero_one(i, _):
            pltpu.make_async_copy(zbuf, ys_hbm.at[chunk_rows(z_ref[i])], sem_z).start()
            return 0

        lax.fori_loop(0, nz, zero_one, 0)

    @pl.when(b < nb)
    def _():
        new_expert = jnp.logical_or(b == 0, be_ref[b] != be_ref[jnp.maximum(b - 1, 0)])

        @pl.when(new_expert)
        def _():
            for c in weight_copies(be_ref[b]):
                c.wait()
            w_in_bf[...] = w_in_f32[...].astype(BF16)
            w_out_bf[...] = w_out_f32[...].astype(BF16)

            @pl.when(nx_ref[b] >= 0)
            def _():
                for c in weight_copies(nx_ref[b]):
                    c.start()

        gather_wait(slot)

        @pl.when(b + 1 < nb)
        def _():
            gather(b + 1, 1 - slot)

        @pl.when(b >= 2)
        def _():
            scatter_wait(slot)

        @pl.when(half_ref[b] == 0)
        def _():
            ffn(BLOCK_ROWS)

        @pl.when(half_ref[b] != 0)
        def _():
            ffn(HALF_ROWS)

        scatter(b, slot)

        @pl.when(b == 0)
        def _():
            def zero_wait(i, _):
                pltpu.make_async_copy(zbuf, ys_hbm.at[pl.ds(0, ROW_CHUNK)], sem_z).wait()
                return 0

            lax.fori_loop(0, nz, zero_wait, 0)

        @pl.when(b == nb - 1)
        def _():
            scatter_wait(slot)

            @pl.when(b >= 1)
            def _():
                scatter_wait(1 - slot)


def _experts(xs, plan, layer, w_in, b_in, w_out, b_out, nb_max, ys_rows):
    bmap = lambda b, be, *_: (layer, be[b], 0, 0)
    grid_spec = pltpu.PrefetchScalarGridSpec(
        num_scalar_prefetch=8,
        grid=(nb_max,),
        in_specs=[pl.BlockSpec(memory_space=pl.ANY),
                  pl.BlockSpec(memory_space=pl.ANY),
                  pl.BlockSpec((None, None, 1, 2 * D), bmap),
                  pl.BlockSpec(memory_space=pl.ANY),
                  pl.BlockSpec((None, None, 1, D), bmap)],
        out_specs=pl.BlockSpec(memory_space=pl.ANY),
        scratch_shapes=[pltpu.VMEM((2, BLOCK_ROWS, D), BF16), pltpu.VMEM((2, BLOCK_ROWS, D), BF16),
                        pltpu.VMEM((D, 2 * D), F32), pltpu.VMEM((D, D), F32),
                        pltpu.VMEM((D, 2 * D), BF16), pltpu.VMEM((D, D), BF16), pltpu.VMEM((ROW_CHUNK, D), BF16),
                        pltpu.SemaphoreType.DMA((2,)), pltpu.SemaphoreType.DMA((2,)), pltpu.SemaphoreType.DMA(()),
                        pltpu.SemaphoreType.DMA((2,))],
    )
    return pl.pallas_call(
        functools.partial(_expert_kernel, layer=layer),
        out_shape=jax.ShapeDtypeStruct((ys_rows, D), BF16),
        grid_spec=grid_spec,
        compiler_params=_cparams(("arbitrary",)),
        name="experts",
    )(*plan, xs, w_in, b_in, w_out, b_out)


def _combine_kernel(ys_ref, meta_ref, x1_ref, ada_ref, lng_ref, lnb_ref, o_ref):
    tt = x1_ref.shape[0]
    cap = ys_ref.shape[0]
    meta = meta_ref[...]
    col = lax.broadcasted_iota(jnp.int32, (tt, cap), 1).astype(F32)
    gmat = jnp.zeros((tt, cap), F32)
    for k in range(TOP_K):
        gmat = jnp.where(col == meta[:, k:k + 1], meta[:, TOP_K + k:TOP_K + k + 1], gmat)
    f = _dot(gmat.astype(BF16), ys_ref[...])
    ada = ada_ref[0] if len(ada_ref.shape) == 3 else ada_ref[...]
    g2 = ada[:, 5 * D:6 * D]
    o_ref[...] = _ln(ALPHA * x1_ref[...] + (1.0 + g2) * f, lng_ref[...], lnb_ref[...])


def _combine_prompt(ys, meta, x1, ada_p, ln_g, ln_b, seq):
    n = x1.shape[0]
    cap = _tile_capacity(MOE_TILE)
    per_seq = seq // MOE_TILE
    return pl.pallas_call(
        _combine_kernel,
        out_shape=jax.ShapeDtypeStruct((n, D), F32),
        grid=(n // MOE_TILE,),
        in_specs=[pl.BlockSpec((cap, D), lambda i: (i, 0)),
                  pl.BlockSpec((MOE_TILE, LANES), lambda i: (i, 0)),
                  pl.BlockSpec((MOE_TILE, D), lambda i: (i, 0)),
                  pl.BlockSpec((1, 1, 6 * D), lambda i: (i // per_seq, 0, 0)),
                  pl.BlockSpec((1, D), lambda i: (0, 0)),
                  pl.BlockSpec((1, D), lambda i: (0, 0))],
        out_specs=pl.BlockSpec((MOE_TILE, D), lambda i: (i, 0)),
        compiler_params=_cparams(("arbitrary",)),
        name="combine_prompt",
    )(ys, meta, x1, ada_p, ln_g, ln_b)


def _combine_sample(ys, meta, x1, ada_s, ln_g, ln_b, tile):
    n = x1.shape[0]
    cap = _tile_capacity(MOE_TILE)
    return pl.pallas_call(
        _combine_kernel,
        out_shape=jax.ShapeDtypeStruct((n, D), F32),
        grid=(1,),
        in_specs=[pl.BlockSpec((cap, D), lambda i: (tile, 0)),
                  pl.BlockSpec((n, LANES), lambda i: (tile * (MOE_TILE // n), 0)),
                  pl.BlockSpec((n, D), lambda i: (0, 0)),
                  pl.BlockSpec((n, 6 * D), lambda i: (0, 0)),
                  pl.BlockSpec((1, D), lambda i: (0, 0)),
                  pl.BlockSpec((1, D), lambda i: (0, 0))],
        out_specs=pl.BlockSpec((n, D), lambda i: (0, 0)),
        compiler_params=_cparams(("arbitrary",)),
        name="combine_sample",
    )(ys, meta, x1, ada_s, ln_g, ln_b)


def _moe_and_norm(h2_p, h2_s, lg_p, lg_s, x1_p, x1_s, ada_p, ada_s, ln_g, ln_b,
                  layer, w_in, b_in, w_out, b_out, seq):
    n_p, n_s = h2_p.shape[0], h2_s.shape[0]
    n_valid = n_p + n_s
    nt = n_p // MOE_TILE + 1
    cap = _tile_capacity(MOE_TILE)
    cap_chunks = cap // ROW_CHUNK
    xs, meta, cnt = _route(h2_p, h2_s, lg_p, lg_s)
    total_chunks_max = (TOP_K * n_valid + nt * N_EXPERTS * (ROW_CHUNK - 1)) // ROW_CHUNK
    nb_max = -(-total_chunks_max // BLOCK_CHUNKS) + N_EXPERTS
    max_pad = N_EXPERTS * (BLOCK_CHUNKS - 1)
    nz_max = nt * cap_chunks - (TOP_K * n_valid) // ROW_CHUNK + max_pad
    plan = _plan(cnt[:, :, 0].reshape(-1), nt, cap_chunks, nb_max, max_pad, nz_max)
    ys = _experts(xs, plan, layer, w_in, b_in, w_out, b_out, nb_max, (nt * cap_chunks + max_pad) * ROW_CHUNK)
    x2_p = _combine_prompt(ys, meta, x1_p, ada_p, ln_g, ln_b, seq)
    x2_s = _combine_sample(ys, meta, x1_s, ada_s, ln_g, ln_b, n_p // MOE_TILE)
    return x2_p, x2_s


def _row(v):
    return v.reshape(1, -1)


def kernel(x_prompt, x_sample, state_conv_a, state_pool_b, state_conv_c, c_prompt, c_sample,
           w_ada, b_ada, ln1_g, ln1_b, ln2_g, ln2_b,
           a_w_in, a_b_in, a_w_dw, a_b_dw, a_ln_g, a_ln_b, a_w_out,
           b_w_grp, b_scale, c_w_in, c_w_conv, c_w_out,
           d_w_in, d_b_in, d_ln_g, d_ln_b, d_w_s, d_b_s, d_w_out,
           w_router, b_router, w_moe_in, b_moe_in, w_moe_out, b_moe_out):
    bp, seq, _ = x_prompt.shape
    bs = x_sample.shape[0]
    assert seq % TOK_TILE == 0 and seq % MOE_TILE == 0 and MOE_TILE % bs == 0
    assert x_sample.shape[1] == 1 and w_ada.shape[0] == DEPTH == 4

    ada = _ada(jnp.concatenate([c_prompt, c_sample], axis=0), w_ada, b_ada)
    xp = x_prompt.reshape(bp * seq, D)
    xs = x_sample.reshape(bs, D)
    b_moe_in4 = b_moe_in.reshape(DEPTH, N_EXPERTS, 1, 2 * D)
    b_moe_out4 = b_moe_out.reshape(DEPTH, N_EXPERTS, 1, D)
    states = {}

    for i in range(DEPTH):
        ada_p = ada[i, :bp].reshape(bp, 1, 6 * D)
        ada_s = ada[i, bp:]
        wr = jnp.pad(w_router[i], ((0, 0), (0, LANES - N_EXPERTS)))
        ln1 = [_row(ln1_g[i]), _row(ln1_b[i]), wr, _row(jnp.pad(b_router[i], (0, LANES - N_EXPERTS)))]
        if i == 0:
            wts = [a_w_in[0].astype(BF16), _row(a_b_in[0]), a_w_dw[0], _row(a_b_dw[0]), _row(a_ln_g[0]),
                   _row(a_ln_b[0]), a_w_out[0].astype(BF16)] + ln1
            x1_p, h2_p, lg_p, st =_mix_prompt(_mix_a_prompt_kernel, "mix_a_prompt", xp, ada_p, wts, A_HALO,
                                         [pltpu.VMEM((A_HALO + TOK_TILE, D), F32)], seq)
            states["a_p"] = st[:, A_HALO - (CONV_A_WIDTH - 1):][None]
            hist = jnp.transpose(state_conv_a[0], (1, 0, 2))
            x1_s, h2_s, lg_s, new =_mix_sample(_mix_a_sample_kernel, "mix_a_sample", xs, ada_s, [hist] + wts)
            states["a_s"] = jnp.concatenate([state_conv_a[0][:, 1:], new[:, None]], axis=1)[None]
        elif i == 1:
            wts = [b_w_grp[0], _row(b_scale[0])] + ln1
            x1_p, h2_p, lg_p, st =_mix_prompt(_mix_b_prompt_kernel, "mix_b_prompt", xp, ada_p, wts, B_HALO,
                                         [pltpu.VMEM((B_HALO + TOK_TILE, D), F32)], seq)
            states["b_p"] = st[:, B_HALO - POOL_HIST:][None]
            hist = jnp.transpose(state_pool_b[0], (1, 0, 2))
            x1_s, h2_s, lg_s, new =_mix_sample(_mix_b_sample_kernel, "mix_b_sample", xs, ada_s, [hist] + wts)
            states["b_s"] = jnp.concatenate([state_pool_b[0][:, 1:], new[:, None]], axis=1)[None]
        elif i == 2:
            wts = [c_w_in[0].astype(BF16), c_w_conv[0], c_w_out[0].astype(BF16)] + ln1
            x1_p, h2_p, lg_p, st =_mix_prompt(_mix_c_prompt_kernel, "mix_c_prompt", xp, ada_p, wts, C_HALO,
                                         [pltpu.VMEM((C_HALO + TOK_TILE, D), F32)], seq)
            states["c_p"] = st[:, C_HALO - (CONV_C_WIDTH - 1):][None]
            hist = jnp.transpose(state_conv_c[0], (1, 0, 2))
            x1_s, h2_s, lg_s, new =_mix_sample(_mix_c_sample_kernel, "mix_c_sample", xs, ada_s, [hist] + wts)
            states["c_s"] = jnp.concatenate([state_conv_c[0][:, 1:], new[:, None]], axis=1)[None]
        else:
            common = [d_w_in[0].astype(BF16), _row(d_b_in[0]), _row(d_ln_g[0]), _row(d_ln_b[0])]
            wts = common + [d_w_s[0], d_b_s[0].T, d_w_out[0].astype(BF16)] + ln1
            x1_p, h2_p, lg_p, st =_mix_prompt(_mix_d_prompt_kernel, "mix_d_prompt", xp, ada_p, wts, CHUNK, [], seq)
            states["d_p"] = st[None]
            w_s0 = jnp.repeat(d_w_s[0][:, 0, 0], SGU_GROUP).reshape(1, D)
            b_s0 = jnp.repeat(d_b_s[0][:, 0], SGU_GROUP).reshape(1, D)
            wts_s = common + [w_s0, b_s0, d_w_out[0].astype(BF16)] + ln1
            x1_s, h2_s, lg_s, new =_mix_sample(_mix_d_sample_kernel, "mix_d_sample", xs, ada_s, wts_s)
            states["d_s"] = new[:, None][None]

        xp, xs = _moe_and_norm(h2_p, h2_s, lg_p, lg_s, x1_p, x1_s, ada_p, ada_s, _row(ln2_g[i]), _row(ln2_b[i]),
                               i, w_moe_in, b_moe_in4, w_moe_out, b_moe_out4, seq)

    return (xp.reshape(bp, seq, D), xs.reshape(bs, 1, D),
            states["a_p"], states["a_s"], states["b_p"], states["b_s"],
            states["c_p"], states["c_s"], states["d_p"], states["d_s"])
```

```python
import functools

import jax
import jax.numpy as jnp
from jax import lax
from jax.experimental import pallas as pl
from jax.experimental.pallas import tpu as pltpu

F32 = jnp.float32
BF16 = jnp.bfloat16

D = 1024
DEPTH = 4
N_EXPERTS = 32
TOP_K = 4
CONV_A_WIDTH = 31
POOL_WINDOWS = (2, 4, 8, 16)
POOL_GROUP = D // 4
POOL_HIST = 15
CONV_C_WIDTH = 3
CHUNK = 128
N_SGU_GROUPS = 4
SGU_GROUP = D // N_SGU_GROUPS
SWIGLU_LIMIT = 7.0
SWIGLU_ALPHA = 1.702
ALPHA = (2 * DEPTH) ** 0.25
LN_EPS = 1e-5

LANES = 128
SUBLANES_BF16 = 16
VMEM_LIMIT = 56 * 1024 * 1024

TOK_TILE = 512
MOE_TILE = 512
ROW_CHUNK = SUBLANES_BF16
BLOCK_CHUNKS = 32
BLOCK_ROWS = BLOCK_CHUNKS * ROW_CHUNK
HALF_CHUNKS = BLOCK_CHUNKS // 2
HALF_ROWS = BLOCK_ROWS // 2


def _tile_capacity(tt):
    cap = TOP_K * tt + N_EXPERTS * (ROW_CHUNK - 1)
    return -(-cap // 256) * 256


def _cparams(sem=None):
    return pltpu.CompilerParams(dimension_semantics=sem, vmem_limit_bytes=VMEM_LIMIT)


def _dot(a, b):
    return jnp.dot(a, b, preferred_element_type=F32)


def _dot_split(a, b):
    hi = a.astype(BF16)
    lo = (a - hi.astype(F32)).astype(BF16)
    return _dot(hi, b) + _dot(lo, b)


def _ln(x, g, b):
    mu = jnp.mean(x, axis=-1, keepdims=True)
    xc = x - mu
    var = jnp.mean(xc * xc, axis=-1, keepdims=True)
    return xc * lax.rsqrt(var + LN_EPS) * g + b


def _split_ada(ada):
    return [ada[:, i * D:(i + 1) * D] for i in range(6)]


def _post_mixer(x, out, ada, ln_g, ln_b, wr_ref, br_ref, x1_ref, h2_ref, lg_ref):
    _, _, g1, sh2, sc2, _ = _split_ada(ada)
    x1 = _ln(ALPHA * x + (1.0 + g1) * out, ln_g, ln_b)
    x1_ref[...] = x1
    h2 = x1 * (1.0 + sc2) + sh2
    hi = h2.astype(BF16)
    h2_ref[...] = hi
    lo = (h2 - hi.astype(F32)).astype(BF16)
    wr = wr_ref[...]
    w_hi = wr.astype(BF16)
    w_lo = (wr - w_hi.astype(F32)).astype(BF16)
    both = _dot(hi, jnp.concatenate([w_hi, w_lo], axis=1))
    lg_ref[...] = both[:, :LANES] + both[:, LANES:] + _dot(lo, w_hi) + br_ref[...]


def _ada_kernel(c_ref, w_ref, b_ref, o_ref):
    c = c_ref[...]
    sc = (c * jax.nn.sigmoid(c)).astype(BF16)
    o_ref[0] = _dot(sc, w_ref[0].astype(BF16)) + b_ref[0]


def _ada(c_all, w_ada, b_ada):
    rows = c_all.shape[0]
    return pl.pallas_call(
        _ada_kernel,
        out_shape=jax.ShapeDtypeStruct((DEPTH, rows, 6 * D), F32),
        grid=(DEPTH, 6),
        in_specs=[
            pl.BlockSpec((rows, D), lambda i, j: (0, 0)),
            pl.BlockSpec((1, D, D), lambda i, j: (i, 0, j)),
            pl.BlockSpec((1, 1, D), lambda i, j: (i, 0, j)),
        ],
        out_specs=pl.BlockSpec((1, rows, D), lambda i, j: (i, 0, j)),
        compiler_params=_cparams(("arbitrary", "arbitrary")),
        name="ada",
    )(c_all, w_ada, b_ada.reshape(DEPTH, 1, 6 * D))


A_HALO = 32
B_HALO = 16
C_HALO = 8
CONV_ROWS = 64


def _mix_a_prompt_kernel(x_ref, ada_ref, w_in_ref, b_in_ref, w_dw_ref, b_dw_ref, lng_ref, lnb_ref,
                         w_out_ref, ln1g_ref, ln1b_ref, wr_ref, br_ref, x1_ref, h2_ref, lg_ref, st_ref,
                         u_scr, shift_scr, y_scr):
    t = pl.program_id(1)
    tt = x_ref.shape[0]
    x = x_ref[...]
    ada = ada_ref[0]
    sh1, sc1 = ada[:, 0:D], ada[:, D:2 * D]
    h = (x * (1.0 + sc1) + sh1).astype(BF16)
    z = _dot(h, w_in_ref[...]) + b_in_ref[...]
    u = z[:, :D] * jax.nn.sigmoid(z[:, D:])

    @pl.when(t == 0)
    def _():
        u_scr[0:A_HALO, :] = jnp.zeros((A_HALO, D), F32)

    u_scr[A_HALO:A_HALO + tt, :] = u
    first = A_HALO - (CONV_A_WIDTH - 1)
    for lb in range(D // LANES):
        lanes = slice(lb * LANES, (lb + 1) * LANES)
        for s in range(8):
            rows = tt + A_HALO - (0 if s == 0 else 8)
            shift_scr[s, 0:rows, :] = u_scr[s:s + rows, lanes]

        def row_block(rb, carry, lanes=lanes):
            r0 = rb * CONV_ROWS
            acc = jnp.broadcast_to(b_dw_ref[:, lanes], (CONV_ROWS, LANES))
            for k in range(CONV_A_WIDTH):
                q, s = divmod(first + k, 8)
                start = pl.multiple_of(r0 + 8 * q, 8)
                acc = acc + w_dw_ref[k:k + 1, lanes] * shift_scr[s, pl.ds(start, CONV_ROWS), :]
            y_scr[pl.ds(pl.multiple_of(r0, CONV_ROWS), CONV_ROWS), lanes] = acc
            return carry

        lax.fori_loop(0, tt // CONV_ROWS, row_block, 0)
    y = _ln(y_scr[...], lng_ref[...], lnb_ref[...])
    y = y * jax.nn.sigmoid(y)
    out = _dot(y.astype(BF16), w_out_ref[...])
    _post_mixer(x, out, ada, ln1g_ref[...], ln1b_ref[...], wr_ref, br_ref, x1_ref, h2_ref, lg_ref)
    tail = u_scr[tt:tt + A_HALO, :]
    u_scr[0:A_HALO, :] = tail

    @pl.when(t == pl.num_programs(1) - 1)
    def _():
        st_ref[0] = tail


def _mix_b_prompt_kernel(x_ref, ada_ref, w_grp_ref, scale_ref, ln1g_ref, ln1b_ref,
                         wr_ref, br_ref, x1_ref, h2_ref, lg_ref, st_ref, h_scr):
    t = pl.program_id(1)
    tt = x_ref.shape[0]
    x = x_ref[...]
    ada = ada_ref[0]
    sh1, sc1 = ada[:, 0:D], ada[:, D:2 * D]
    h = x * (1.0 + sc1) + sh1

    @pl.when(t == 0)
    def _():
        h_scr[0:B_HALO, :] = jnp.zeros((B_HALO, D), F32)

    h_scr[B_HALO:B_HALO + tt, :] = h
    pos = (t * tt + lax.broadcasted_iota(jnp.int32, (tt, POOL_GROUP), 0)).astype(F32)
    outs = []
    for gi, w in enumerate(POOL_WINDOWS):
        lo = gi * POOL_GROUP
        s = h_scr[B_HALO:B_HALO + tt, lo:lo + POOL_GROUP]
        for j in range(1, w):
            s = s + h_scr[B_HALO - j:B_HALO - j + tt, lo:lo + POOL_GROUP]
        cnt = jnp.minimum(float(w), pos + 1.0)
        pooled = s / cnt - h[:, lo:lo + POOL_GROUP]
        outs.append(_dot_split(pooled, w_grp_ref[gi].astype(BF16)))
    out = jnp.concatenate(outs, axis=-1) * scale_ref[...]
    _post_mixer(x, out, ada, ln1g_ref[...], ln1b_ref[...], wr_ref, br_ref, x1_ref, h2_ref, lg_ref)
    tail = h_scr[tt:tt + B_HALO, :]
    h_scr[0:B_HALO, :] = tail

    @pl.when(t == pl.num_programs(1) - 1)
    def _():
        st_ref[0] = tail


def _mix_c_prompt_kernel(x_ref, ada_ref, w_in_ref, w_conv_ref, w_out_ref, ln1g_ref, ln1b_ref,
                         wr_ref, br_ref, x1_ref, h2_ref, lg_ref, st_ref, v_scr):
    t = pl.program_id(1)
    tt = x_ref.shape[0]
    x = x_ref[...]
    ada = ada_ref[0]
    sh1, sc1 = ada[:, 0:D], ada[:, D:2 * D]
    h = (x * (1.0 + sc1) + sh1).astype(BF16)
    z = _dot(h, w_in_ref[...])
    bg = z[:, :D]
    v = z[:, D:2 * D] * z[:, 2 * D:]

    @pl.when(t == 0)
    def _():
        v_scr[0:C_HALO, :] = jnp.zeros((C_HALO, D), F32)

    v_scr[C_HALO:C_HALO + tt, :] = v
    first = C_HALO - (CONV_C_WIDTH - 1)
    y = w_conv_ref[CONV_C_WIDTH - 1:CONV_C_WIDTH, :] * v
    for k in range(CONV_C_WIDTH - 1):
        y = y + w_conv_ref[k:k + 1, :] * v_scr[first + k:first + k + tt, :]
    out = _dot((bg * y).astype(BF16), w_out_ref[...])
    _post_mixer(x, out, ada, ln1g_ref[...], ln1b_ref[...], wr_ref, br_ref, x1_ref, h2_ref, lg_ref)
    tail = v_scr[tt:tt + C_HALO, :]
    v_scr[0:C_HALO, :] = tail

    @pl.when(t == pl.num_programs(1) - 1)
    def _():
        st_ref[0] = tail


def _gelu_exact(x):
    return 0.5 * x * (1.0 + lax.erf(x * (2.0 ** -0.5)))


def _mix_d_prompt_kernel(x_ref, ada_ref, w_in_ref, b_in_ref, lng_ref, lnb_ref, w_s_ref, b_st_ref,
                         w_out_ref, ln1g_ref, ln1b_ref, wr_ref, br_ref, x1_ref, h2_ref, lg_ref, st_ref):
    t = pl.program_id(1)
    tt = x_ref.shape[0]
    x = x_ref[...]
    ada = ada_ref[0]
    sh1, sc1 = ada[:, 0:D], ada[:, D:2 * D]
    h = (x * (1.0 + sc1) + sh1).astype(BF16)
    z = _gelu_exact(_dot(h, w_in_ref[...]) + b_in_ref[...])
    u = z[:, :D]
    v = _ln(z[:, D:], lng_ref[...], lnb_ref[...])
    vb = v.astype(BF16)
    row = lax.broadcasted_iota(jnp.int32, (CHUNK, CHUNK), 0)
    col = lax.broadcasted_iota(jnp.int32, (CHUNK, CHUNK), 1)
    causal = col <= row
    chunks = []
    for c in range(tt // CHUNK):
        groups = []
        for g in range(N_SGU_GROUPS):
            w = jnp.where(causal, w_s_ref[g], 0.0).astype(BF16)
            m = _dot(w, vb[c * CHUNK:(c + 1) * CHUNK, g * SGU_GROUP:(g + 1) * SGU_GROUP])
            groups.append(m + b_st_ref[:, g:g + 1])
        chunks.append(jnp.concatenate(groups, axis=-1))
    mixed = jnp.concatenate(chunks, axis=0)
    out = _dot((u * mixed).astype(BF16), w_out_ref[...])
    _post_mixer(x, out, ada, ln1g_ref[...], ln1b_ref[...], wr_ref, br_ref, x1_ref, h2_ref, lg_ref)

    @pl.when(t == pl.num_programs(1) - 1)
    def _():
        st_ref[0] = v[tt - CHUNK:, :]


def _full(shape):
    nd = len(shape)
    return pl.BlockSpec(shape, lambda b, t: (0,) * nd)


def _mix_prompt(kernel_fn, name, x, ada_p, weights, state_rows, scratch, seq):
    n = x.shape[0]
    batch = n // seq
    nt = seq // TOK_TILE
    tok = pl.BlockSpec((TOK_TILE, D), lambda b, t: (b * nt + t, 0))
    lgt = pl.BlockSpec((TOK_TILE, LANES), lambda b, t: (b * nt + t, 0))
    return pl.pallas_call(
        kernel_fn,
        out_shape=(jax.ShapeDtypeStruct((n, D), F32), jax.ShapeDtypeStruct((n, D), BF16),
                   jax.ShapeDtypeStruct((n, LANES), F32), jax.ShapeDtypeStruct((batch, state_rows, D), F32)),
        grid=(batch, nt),
        in_specs=[tok, pl.BlockSpec((1, 1, 6 * D), lambda b, t: (b, 0, 0))] + [_full(w.shape) for w in weights],
        out_specs=(tok, tok, lgt, pl.BlockSpec((1, state_rows, D), lambda b, t: (b, 0, 0))),
        scratch_shapes=scratch,
        compiler_params=_cparams(("arbitrary", "arbitrary")),
        name=name,
    )(x, ada_p, *weights)


def _mix_a_sample_kernel(x_ref, ada_ref, hist_ref, w_in_ref, b_in_ref, w_dw_ref, b_dw_ref, lng_ref, lnb_ref,
                         w_out_ref, ln1g_ref, ln1b_ref, wr_ref, br_ref, x1_ref, h2_ref, lg_ref, new_ref):
    x = x_ref[...]
    ada = ada_ref[...]
    sh1, sc1 = ada[:, 0:D], ada[:, D:2 * D]
    h = (x * (1.0 + sc1) + sh1).astype(BF16)
    z = _dot(h, w_in_ref[...]) + b_in_ref[...]
    u = z[:, :D] * jax.nn.sigmoid(z[:, D:])
    y = b_dw_ref[...] + w_dw_ref[CONV_A_WIDTH - 1:CONV_A_WIDTH, :] * u
    for k in range(CONV_A_WIDTH - 1):
        y = y + w_dw_ref[k:k + 1, :] * hist_ref[k]
    y = _ln(y, lng_ref[...], lnb_ref[...])
    y = y * jax.nn.sigmoid(y)
    out = _dot(y.astype(BF16), w_out_ref[...])
    _post_mixer(x, out, ada, ln1g_ref[...], ln1b_ref[...], wr_ref, br_ref, x1_ref, h2_ref, lg_ref)
    new_ref[...] = u


def _mix_b_sample_kernel(x_ref, ada_ref, hist_ref, w_grp_ref, scale_ref, ln1g_ref, ln1b_ref,
                         wr_ref, br_ref, x1_ref, h2_ref, lg_ref, new_ref):
    x = x_ref[...]
    ada = ada_ref[...]
    sh1, sc1 = ada[:, 0:D], ada[:, D:2 * D]
    h = x * (1.0 + sc1) + sh1
    outs = []
    for gi, w in enumerate(POOL_WINDOWS):
        lo = gi * POOL_GROUP
        s = h[:, lo:lo + POOL_GROUP]
        for j in range(1, w):
            s = s + hist_ref[POOL_HIST - j][:, lo:lo + POOL_GROUP]
        pooled = s / float(w) - h[:, lo:lo + POOL_GROUP]
        outs.append(_dot_split(pooled, w_grp_ref[gi].astype(BF16)))
    out = jnp.concatenate(outs, axis=-1) * scale_ref[...]
    _post_mixer(x, out, ada, ln1g_ref[...], ln1b_ref[...], wr_ref, br_ref, x1_ref, h2_ref, lg_ref)
    new_ref[...] = h


def _mix_c_sample_kernel(x_ref, ada_ref, hist_ref, w_in_ref, w_conv_ref, w_out_ref, ln1g_ref, ln1b_ref,
                         wr_ref, br_ref, x1_ref, h2_ref, lg_ref, new_ref):
    x = x_ref[...]
    ada = ada_ref[...]
    sh1, sc1 = ada[:, 0:D], ada[:, D:2 * D]
    h = (x * (1.0 + sc1) + sh1).astype(BF16)
    z = _dot(h, w_in_ref[...])
    bg = z[:, :D]
    v = z[:, D:2 * D] * z[:, 2 * D:]
    y = w_conv_ref[CONV_C_WIDTH - 1:CONV_C_WIDTH, :] * v
    for k in range(CONV_C_WIDTH - 1):
        y = y + w_conv_ref[k:k + 1, :] * hist_ref[k]
    out = _dot((bg * y).astype(BF16), w_out_ref[...])
    _post_mixer(x, out, ada, ln1g_ref[...], ln1b_ref[...], wr_ref, br_ref, x1_ref, h2_ref, lg_ref)
    new_ref[...] = v


def _mix_d_sample_kernel(x_ref, ada_ref, w_in_ref, b_in_ref, lng_ref, lnb_ref, w_s0_ref, b_s0_ref,
                         w_out_ref, ln1g_ref, ln1b_ref, wr_ref, br_ref, x1_ref, h2_ref, lg_ref, new_ref):
    x = x_ref[...]
    ada = ada_ref[...]
    sh1, sc1 = ada[:, 0:D], ada[:, D:2 * D]
    h = (x * (1.0 + sc1) + sh1).astype(BF16)
    z = _gelu_exact(_dot(h, w_in_ref[...]) + b_in_ref[...])
    u = z[:, :D]
    v = _ln(z[:, D:], lng_ref[...], lnb_ref[...])
    mixed = w_s0_ref[...] * v + b_s0_ref[...]
    out = _dot((u * mixed).astype(BF16), w_out_ref[...])
    _post_mixer(x, out, ada, ln1g_ref[...], ln1b_ref[...], wr_ref, br_ref, x1_ref, h2_ref, lg_ref)
    new_ref[...] = v


def _mix_sample(kernel_fn, name, x, ada_s, arrays):
    n = x.shape[0]
    return pl.pallas_call(
        kernel_fn,
        out_shape=(jax.ShapeDtypeStruct((n, D), F32), jax.ShapeDtypeStruct((n, D), BF16),
                   jax.ShapeDtypeStruct((n, LANES), F32), jax.ShapeDtypeStruct((n, D), F32)),
        compiler_params=_cparams(),
        name=name,
    )(x, ada_s, *arrays)


def _route_kernel(hp_ref, hs_ref, lgp_ref, lgs_ref, xs_ref, meta_ref, cnt_ref, *, n_valid):
    i = pl.program_id(0)
    tt = hp_ref.shape[0]
    ns = hs_ref.shape[0]
    cap = xs_ref.shape[0]
    is_sample = i == pl.num_programs(0) - 1
    hs = jnp.concatenate([hs_ref[...], jnp.zeros((tt - ns, D), BF16)], axis=0)
    lgs = jnp.concatenate([lgs_ref[...], jnp.zeros((tt - ns, LANES), F32)], axis=0)
    hb = jnp.where(is_sample, hs, hp_ref[...])
    logits = jnp.where(is_sample, lgs, lgp_ref[...]).T[:N_EXPERTS, :]
    e_iota = lax.broadcasted_iota(jnp.int32, (N_EXPERTS, tt), 0)
    valid = (i * tt + lax.broadcasted_iota(jnp.int32, (1, tt), 1)) < n_valid
    work = logits
    sel, top = [], []
    for _ in range(TOP_K):
        m = jnp.max(work, axis=0, keepdims=True)
        idx = jnp.min(jnp.where(work == m, e_iota, N_EXPERTS), axis=0, keepdims=True)
        oh = e_iota == idx
        sel.append(oh)
        top.append(m)
        work = jnp.where(oh, -jnp.inf, work)
    ex = [jnp.exp(v - top[0]) for v in top]
    denom = ex[0] + ex[1] + ex[2] + ex[3]
    gates = [e / denom for e in ex]

    member = jnp.where(sel[0] | sel[1] | sel[2] | sel[3], 1.0, 0.0)
    member = jnp.where(valid, member, 0.0)
    r_i = lax.broadcasted_iota(jnp.int32, (tt, tt), 0)
    c_i = lax.broadcasted_iota(jnp.int32, (tt, tt), 1)
    before = jnp.where(r_i < c_i, 1.0, 0.0).astype(BF16)
    rank = _dot(member.astype(BF16), before)
    count = jnp.sum(member, axis=1, keepdims=True)
    chunks = jnp.floor((count + float(ROW_CHUNK - 1)) * (1.0 / ROW_CHUNK))
    chunks_b = jnp.broadcast_to(chunks, (N_EXPERTS, LANES))
    er = lax.broadcasted_iota(jnp.int32, (N_EXPERTS, N_EXPERTS), 0)
    ec = lax.broadcasted_iota(jnp.int32, (N_EXPERTS, N_EXPERTS), 1)
    lower = jnp.where(ec < er, 1.0, 0.0).astype(BF16)
    chunk_off = _dot(lower, chunks_b.astype(BF16))
    base = chunk_off[:, 0:1] * float(ROW_CHUNK)
    slot = base + rank
    pos = [jnp.where(valid, jnp.sum(jnp.where(s, slot, 0.0), axis=0, keepdims=True), -1.0) for s in sel]

    row = lax.broadcasted_iota(jnp.int32, (cap, tt), 0).astype(F32)
    onehot = jnp.where(row == pos[0], 1.0, 0.0)
    for k in range(1, TOP_K):
        onehot = jnp.where(row == pos[k], 1.0, onehot)
    xs_ref[...] = _dot(onehot.astype(BF16), hb).astype(BF16)

    meta = jnp.concatenate(pos + gates + [jnp.zeros((LANES - 2 * TOP_K, tt), F32)], axis=0)
    meta_ref[...] = meta.T
    cnt_ref[0] = chunks_b.astype(jnp.int32)


def _route(h2_p, h2_s, lg_p, lg_s):
    n_p, n_s = h2_p.shape[0], h2_s.shape[0]
    ntp = n_p // MOE_TILE
    nt = ntp + 1
    cap = _tile_capacity(MOE_TILE)
    prompt_tile = lambda i: (jnp.minimum(i, ntp - 1), 0)
    return pl.pallas_call(
        functools.partial(_route_kernel, n_valid=n_p + n_s),
        out_shape=(jax.ShapeDtypeStruct((nt * cap, D), BF16),
                   jax.ShapeDtypeStruct((nt * MOE_TILE, LANES), F32),
                   jax.ShapeDtypeStruct((nt, N_EXPERTS, LANES), jnp.int32)),
        grid=(nt,),
        in_specs=[pl.BlockSpec((MOE_TILE, D), prompt_tile),
                  pl.BlockSpec((n_s, D), lambda i: (0, 0)),
                  pl.BlockSpec((MOE_TILE, LANES), prompt_tile),
                  pl.BlockSpec((n_s, LANES), lambda i: (0, 0))],
        out_specs=(pl.BlockSpec((cap, D), lambda i: (i, 0)),
                   pl.BlockSpec((MOE_TILE, LANES), lambda i: (i, 0)),
                   pl.BlockSpec((1, N_EXPERTS, LANES), lambda i: (i, 0, 0))),
        compiler_params=_cparams(("arbitrary",)),
        name="route_dispatch",
    )(h2_p, h2_s, lg_p, lg_s)


def _plan_kernel(cnt_ref, be_ref, nx_ref, half_ref, nb_ref, src_ref, dst_ref, z_ref, nz_ref, tile_off, blk0_s,
                 nblk_s, npad_s, *, nt, cap_chunks, nb_max, max_pad, nz_max):
    del nb_max, nz_max
    i32 = jnp.int32
    trash_base = nt * cap_chunks
    shift = BLOCK_CHUNKS.bit_length() - 1

    def index_grid(ref):
        r, c = ref.shape
        return lax.broadcasted_iota(i32, (r, c), 0) * c + lax.broadcasted_iota(i32, (r, c), 1)

    def in_run(idx, start, length):
        return jnp.logical_and(idx >= start, idx < start + length)

    for t in range(nt):
        tile_off[t] = 0

    d = index_grid(src_ref)

    def expert_body(e, carry):
        p0, blk0, pad0, src, dst = carry

        def tile_body(t, inner):
            p, src, dst = inner
            c = cnt_ref[t * N_EXPERTS + e]
            off = tile_off[t]
            tile_off[t] = off + c
            run = in_run(d, p, c)
            val = d + (t * cap_chunks + off - p)
            return p + c, jnp.where(run, val, src), jnp.where(run, val, dst)

        p1, src, dst = lax.fori_loop(0, nt, tile_body, (p0, src, dst))
        nblk = lax.shift_right_logical(p1 - p0 + (BLOCK_CHUNKS - 1), shift)
        p2 = p0 + nblk * BLOCK_CHUNKS
        pad = in_run(d, p1, p2 - p1)
        src = jnp.where(pad, 0, src)
        dst = jnp.where(pad, d + (trash_base + pad0 - p1), dst)
        blk0_s[e] = blk0
        nblk_s[e] = nblk
        npad_s[e] = p2 - p1
        return p2, blk0 + nblk, pad0 + (p2 - p1), src, dst

    zeros = jnp.zeros(src_ref.shape, i32)
    zero = jnp.int32(0)
    _, nb, n_pad, src, dst = lax.fori_loop(0, N_EXPERTS, expert_body, (zero, zero, zero, zeros, zeros))
    src_ref[...] = src
    dst_ref[...] = dst
    nb_ref[0] = nb

    bi = index_grid(be_ref)

    def block_body(k, carry):
        be, nx, half, nxt = carry
        e = N_EXPERTS - 1 - k
        b0, n = blk0_s[e], nblk_s[e]
        mine = in_run(bi, b0, n)
        last = jnp.logical_and(bi == b0 + n - 1, jnp.logical_and(n > 0, npad_s[e] >= HALF_CHUNKS))
        return (jnp.where(mine, e, be), jnp.where(mine, nxt, nx), jnp.where(last, 1, half),
                jnp.where(n > 0, e, nxt))

    init = (jnp.full(be_ref.shape, N_EXPERTS - 1, i32), jnp.full(be_ref.shape, -1, i32),
            jnp.zeros(be_ref.shape, i32), jnp.int32(-1))
    be, nx, half, _ = lax.fori_loop(0, N_EXPERTS, block_body, init)
    be_ref[...] = be
    nx_ref[...] = nx
    half_ref[...] = half

    zi = index_grid(z_ref)

    def tail_body(t, carry):
        z0, z = carry
        used = tile_off[t]
        n = cap_chunks - used
        return z0 + n, jnp.where(in_run(zi, z0, n), zi + (t * cap_chunks + used - z0), z)

    z1, z = lax.fori_loop(0, nt, tail_body, (zero, jnp.zeros(z_ref.shape, i32)))
    n_trash = max_pad - n_pad
    z_ref[...] = jnp.where(in_run(zi, z1, n_trash), zi + (trash_base + n_pad - z1), z)
    nz_ref[0] = z1 + n_trash


def _plan(cnt_flat, nt, cap_chunks, nb_max, max_pad, nz_max):
    smem = pl.BlockSpec(memory_space=pltpu.SMEM)
    vmem = pl.BlockSpec(memory_space=pltpu.VMEM)
    i32 = jnp.int32

    def table(n):
        return jax.ShapeDtypeStruct((-(-n // (8 * LANES)) * 8, LANES), i32)

    out = pl.pallas_call(
        functools.partial(_plan_kernel, nt=nt, cap_chunks=cap_chunks, nb_max=nb_max, max_pad=max_pad, nz_max=nz_max),
        out_shape=(table(nb_max), table(nb_max), table(nb_max), jax.ShapeDtypeStruct((1,), i32),
                   table(nb_max * BLOCK_CHUNKS), table(nb_max * BLOCK_CHUNKS),
                   table(nz_max), jax.ShapeDtypeStruct((1,), i32)),
        in_specs=[smem],
        out_specs=(vmem, vmem, vmem, smem, vmem, vmem, vmem, smem),
        scratch_shapes=[pltpu.SMEM((nt,), i32), pltpu.SMEM((N_EXPERTS,), i32), pltpu.SMEM((N_EXPERTS,), i32),
                        pltpu.SMEM((N_EXPERTS,), i32)],
        name="plan",
    )(cnt_flat)
    return tuple(o.reshape(-1) for o in out)


def _expert_kernel(be_ref, nx_ref, half_ref, nb_ref, src_ref, dst_ref, z_ref, nz_ref, xs_hbm, w_in_hbm, b_in_ref,
                   w_out_hbm, b_out_ref, ys_hbm, xbuf, ybuf, w_in_f32, w_out_f32, w_in_bf, w_out_bf, zbuf,
                   sem_in, sem_out, sem_z, sem_w, *, layer):
    b = pl.program_id(0)
    nb = nb_ref[0]
    nz = nz_ref[0]
    slot = lax.rem(b, 2)

    def weight_copies(e):
        return (pltpu.make_async_copy(w_in_hbm.at[layer, e], w_in_f32, sem_w.at[0]),
                pltpu.make_async_copy(w_out_hbm.at[layer, e], w_out_f32, sem_w.at[1]))

    def ffn(rows):
        e = be_ref[b]
        x = xbuf[slot, 0:rows, :]
        h = _dot(x, w_in_bf[...]) + b_in_ref[layer, e]
        g = jnp.minimum(h[:, :D], SWIGLU_LIMIT)
        u = jnp.clip(h[:, D:], -SWIGLU_LIMIT, SWIGLU_LIMIT)
        a = (u + 1.0) * (g * jax.nn.sigmoid(SWIGLU_ALPHA * g))
        y = _dot(a.astype(BF16), w_out_bf[...]) + b_out_ref[layer, e]
        ybuf[slot, 0:rows, :] = y.astype(BF16)

    def chunk_rows(c):
        return pl.ds(pl.multiple_of(c * ROW_CHUNK, ROW_CHUNK), ROW_CHUNK)

    def gather(blk, s):
        for j in range(BLOCK_CHUNKS):
            c = src_ref[blk * BLOCK_CHUNKS + j]
            pltpu.make_async_copy(xs_hbm.at[chunk_rows(c)], xbuf.at[s, pl.ds(j * ROW_CHUNK, ROW_CHUNK)],
                                  sem_in.at[s]).start()

    def gather_wait(s):
        pltpu.make_async_copy(xs_hbm.at[pl.ds(0, BLOCK_ROWS)], xbuf.at[s], sem_in.at[s]).wait()

    def scatter(blk, s):
        for j in range(BLOCK_CHUNKS):
            c = dst_ref[blk * BLOCK_CHUNKS + j]
            pltpu.make_async_copy(ybuf.at[s, pl.ds(j * ROW_CHUNK, ROW_CHUNK)], ys_hbm.at[chunk_rows(c)],
                                  sem_out.at[s]).start()

    def scatter_wait(s):
        pltpu.make_async_copy(ybuf.at[s], ys_hbm.at[pl.ds(0, BLOCK_ROWS)], sem_out.at[s]).wait()

    @pl.when(b == 0)
    def _():
        for c in weight_copies(be_ref[0]):
            c.start()
        gather(0, 0)
        zbuf[...] = jnp.zeros(zbuf.shape, zbuf.dtype)
        ybuf[...] = jnp.zeros(ybuf.shape, ybuf.dtype)

        def zero_one(i, _):
            pltpu.make_async_copy(zbuf, ys_hbm.at[chunk_rows(z_ref[i])], sem_z).start()
            return 0

        lax.fori_loop(0, nz, zero_one, 0)

    @pl.when(b < nb)
    def _():
        new_expert = jnp.logical_or(b == 0, be_ref[b] != be_ref[jnp.maximum(b - 1, 0)])

        @pl.when(new_expert)
        def _():
            for c in weight_copies(be_ref[b]):
                c.wait()
            w_in_bf[...] = w_in_f32[...].astype(BF16)
            w_out_bf[...] = w_out_f32[...].astype(BF16)

            @pl.when(nx_ref[b] >= 0)
            def _():
                for c in weight_copies(nx_ref[b]):
                    c.start()

        gather_wait(slot)

        @pl.when(b + 1 < nb)
        def _():
            gather(b + 1, 1 - slot)

        @pl.when(b >= 2)
        def _():
            scatter_wait(slot)

        @pl.when(half_ref[b] == 0)
        def _():
            ffn(BLOCK_ROWS)

        @pl.when(half_ref[b] != 0)
        def _():
            ffn(HALF_ROWS)

        scatter(b, slot)

        @pl.when(b == 0)
        def _():
            def zero_wait(i, _):
                pltpu.make_async_copy(zbuf, ys_hbm.at[pl.ds(0, ROW_CHUNK)], sem_z).wait()
                return 0

            lax.fori_loop(0, nz, zero_wait, 0)

        @pl.when(b == nb - 1)
        def _():
            scatter_wait(slot)

            @pl.when(b >= 1)
            def _():
                scatter_wait(1 - slot)


def _experts(xs, plan, layer, w_in, b_in, w_out, b_out, nb_max, ys_rows):
    resident = lambda b, *_: (0, 0, 0, 0)
    grid_spec = pltpu.PrefetchScalarGridSpec(
        num_scalar_prefetch=8,
        grid=(nb_max,),
        in_specs=[pl.BlockSpec(memory_space=pl.ANY),
                  pl.BlockSpec(memory_space=pl.ANY),
                  pl.BlockSpec(b_in.shape, resident),
                  pl.BlockSpec(memory_space=pl.ANY),
                  pl.BlockSpec(b_out.shape, resident)],
        out_specs=pl.BlockSpec(memory_space=pl.ANY),
        scratch_shapes=[pltpu.VMEM((2, BLOCK_ROWS, D), BF16), pltpu.VMEM((2, BLOCK_ROWS, D), BF16),
                        pltpu.VMEM((D, 2 * D), F32), pltpu.VMEM((D, D), F32),
                        pltpu.VMEM((D, 2 * D), BF16), pltpu.VMEM((D, D), BF16), pltpu.VMEM((ROW_CHUNK, D), BF16),
                        pltpu.SemaphoreType.DMA((2,)), pltpu.SemaphoreType.DMA((2,)), pltpu.SemaphoreType.DMA(()),
                        pltpu.SemaphoreType.DMA((2,))],
    )
    return pl.pallas_call(
        functools.partial(_expert_kernel, layer=layer),
        out_shape=jax.ShapeDtypeStruct((ys_rows, D), BF16),
        grid_spec=grid_spec,
        compiler_params=_cparams(("arbitrary",)),
        name="experts",
    )(*plan, xs, w_in, b_in, w_out, b_out)


def _combine_kernel(ys_ref, meta_ref, x1_ref, ada_ref, lng_ref, lnb_ref, o_ref):
    tt = x1_ref.shape[0]
    cap = ys_ref.shape[0]
    meta = meta_ref[...]
    col = lax.broadcasted_iota(jnp.int32, (tt, cap), 1).astype(F32)
    gmat = jnp.zeros((tt, cap), F32)
    for k in range(TOP_K):
        gmat = jnp.where(col == meta[:, k:k + 1], meta[:, TOP_K + k:TOP_K + k + 1], gmat)
    f = _dot(gmat.astype(BF16), ys_ref[...])
    ada = ada_ref[0] if len(ada_ref.shape) == 3 else ada_ref[...]
    g2 = ada[:, 5 * D:6 * D]
    o_ref[...] = _ln(ALPHA * x1_ref[...] + (1.0 + g2) * f, lng_ref[...], lnb_ref[...])


def _combine_prompt(ys, meta, x1, ada_p, ln_g, ln_b, seq):
    n = x1.shape[0]
    cap = _tile_capacity(MOE_TILE)
    per_seq = seq // MOE_TILE
    return pl.pallas_call(
        _combine_kernel,
        out_shape=jax.ShapeDtypeStruct((n, D), F32),
        grid=(n // MOE_TILE,),
        in_specs=[pl.BlockSpec((cap, D), lambda i: (i, 0)),
                  pl.BlockSpec((MOE_TILE, LANES), lambda i: (i, 0)),
                  pl.BlockSpec((MOE_TILE, D), lambda i: (i, 0)),
                  pl.BlockSpec((1, 1, 6 * D), lambda i: (i // per_seq, 0, 0)),
                  pl.BlockSpec((1, D), lambda i: (0, 0)),
                  pl.BlockSpec((1, D), lambda i: (0, 0))],
        out_specs=pl.BlockSpec((MOE_TILE, D), lambda i: (i, 0)),
        compiler_params=_cparams(("arbitrary",)),
        name="combine_prompt",
    )(ys, meta, x1, ada_p, ln_g, ln_b)


def _combine_sample(ys, meta, x1, ada_s, ln_g, ln_b, tile):
    n = x1.shape[0]
    cap = _tile_capacity(MOE_TILE)
    return pl.pallas_call(
        _combine_kernel,
        out_shape=jax.ShapeDtypeStruct((n, D), F32),
        grid=(1,),
        in_specs=[pl.BlockSpec((cap, D), lambda i: (tile, 0)),
                  pl.BlockSpec((n, LANES), lambda i: (tile * (MOE_TILE // n), 0)),
                  pl.BlockSpec((n, D), lambda i: (0, 0)),
                  pl.BlockSpec((n, 6 * D), lambda i: (0, 0)),
                  pl.BlockSpec((1, D), lambda i: (0, 0)),
                  pl.BlockSpec((1, D), lambda i: (0, 0))],
        out_specs=pl.BlockSpec((n, D), lambda i: (0, 0)),
        compiler_params=_cparams(("arbitrary",)),
        name="combine_sample",
    )(ys, meta, x1, ada_s, ln_g, ln_b)


def _moe_and_norm(h2_p, h2_s, lg_p, lg_s, x1_p, x1_s, ada_p, ada_s, ln_g, ln_b,
                  layer, w_in, b_in, w_out, b_out, seq):
    n_p, n_s = h2_p.shape[0], h2_s.shape[0]
    n_valid = n_p + n_s
    nt = n_p // MOE_TILE + 1
    cap = _tile_capacity(MOE_TILE)
    cap_chunks = cap // ROW_CHUNK
    xs, meta, cnt = _route(h2_p, h2_s, lg_p, lg_s)
    total_chunks_max = (TOP_K * n_valid + nt * N_EXPERTS * (ROW_CHUNK - 1)) // ROW_CHUNK
    nb_max = -(-total_chunks_max // BLOCK_CHUNKS) + N_EXPERTS
    max_pad = N_EXPERTS * (BLOCK_CHUNKS - 1)
    nz_max = nt * cap_chunks - (TOP_K * n_valid) // ROW_CHUNK + max_pad
    plan = _plan(cnt[:, :, 0].reshape(-1), nt, cap_chunks, nb_max, max_pad, nz_max)
    ys = _experts(xs, plan, layer, w_in, b_in, w_out, b_out, nb_max, (nt * cap_chunks + max_pad) * ROW_CHUNK)
    x2_p = _combine_prompt(ys, meta, x1_p, ada_p, ln_g, ln_b, seq)
    x2_s = _combine_sample(ys, meta, x1_s, ada_s, ln_g, ln_b, n_p // MOE_TILE)
    return x2_p, x2_s


def _row(v):
    return v.reshape(1, -1)


def kernel(x_prompt, x_sample, state_conv_a, state_pool_b, state_conv_c, c_prompt, c_sample,
           w_ada, b_ada, ln1_g, ln1_b, ln2_g, ln2_b,
           a_w_in, a_b_in, a_w_dw, a_b_dw, a_ln_g, a_ln_b, a_w_out,
           b_w_grp, b_scale, c_w_in, c_w_conv, c_w_out,
           d_w_in, d_b_in, d_ln_g, d_ln_b, d_w_s, d_b_s, d_w_out,
           w_router, b_router, w_moe_in, b_moe_in, w_moe_out, b_moe_out):
    bp, seq, _ = x_prompt.shape
    bs = x_sample.shape[0]
    assert seq % TOK_TILE == 0 and seq % MOE_TILE == 0 and MOE_TILE % bs == 0
    assert x_sample.shape[1] == 1 and w_ada.shape[0] == DEPTH == 4

    ada = _ada(jnp.concatenate([c_prompt, c_sample], axis=0), w_ada, b_ada)
    xp = x_prompt.reshape(bp * seq, D)
    xs = x_sample.reshape(bs, D)
    b_moe_in4 = b_moe_in.reshape(DEPTH, N_EXPERTS, 1, 2 * D)
    b_moe_out4 = b_moe_out.reshape(DEPTH, N_EXPERTS, 1, D)
    states = {}

    for i in range(DEPTH):
        ada_p = ada[i, :bp].reshape(bp, 1, 6 * D)
        ada_s = ada[i, bp:]
        wr = jnp.pad(w_router[i], ((0, 0), (0, LANES - N_EXPERTS)))
        ln1 = [_row(ln1_g[i]), _row(ln1_b[i]), wr, _row(jnp.pad(b_router[i], (0, LANES - N_EXPERTS)))]
        if i == 0:
            wts = [a_w_in[0].astype(BF16), _row(a_b_in[0]), a_w_dw[0], _row(a_b_dw[0]), _row(a_ln_g[0]),
                   _row(a_ln_b[0]), a_w_out[0].astype(BF16)] + ln1
            x1_p, h2_p, lg_p, st =_mix_prompt(_mix_a_prompt_kernel, "mix_a_prompt", xp, ada_p, wts, A_HALO,
                                         [pltpu.VMEM((A_HALO + TOK_TILE, D), F32),
                                          pltpu.VMEM((8, A_HALO + TOK_TILE, LANES), F32),
                                          pltpu.VMEM((TOK_TILE, D), F32)], seq)
            states["a_p"] = st[:, A_HALO - (CONV_A_WIDTH - 1):][None]
            hist = jnp.transpose(state_conv_a[0], (1, 0, 2))
            x1_s, h2_s, lg_s, new =_mix_sample(_mix_a_sample_kernel, "mix_a_sample", xs, ada_s, [hist] + wts)
            states["a_s"] = jnp.concatenate([state_conv_a[0][:, 1:], new[:, None]], axis=1)[None]
        elif i == 1:
            wts = [b_w_grp[0], _row(b_scale[0])] + ln1
            x1_p, h2_p, lg_p, st =_mix_prompt(_mix_b_prompt_kernel, "mix_b_prompt", xp, ada_p, wts, B_HALO,
                                         [pltpu.VMEM((B_HALO + TOK_TILE, D), F32)], seq)
            states["b_p"] = st[:, B_HALO - POOL_HIST:][None]
            hist = jnp.transpose(state_pool_b[0], (1, 0, 2))
            x1_s, h2_s, lg_s, new =_mix_sample(_mix_b_sample_kernel, "mix_b_sample", xs, ada_s, [hist] + wts)
            states["b_s"] = jnp.concatenate([state_pool_b[0][:, 1:], new[:, None]], axis=1)[None]
        elif i == 2:
            wts = [c_w_in[0].astype(BF16), c_w_conv[0], c_w_out[0].astype(BF16)] + ln1
            x1_p, h2_p, lg_p, st =_mix_prompt(_mix_c_prompt_kernel, "mix_c_prompt", xp, ada_p, wts, C_HALO,
                                         [pltpu.VMEM((C_HALO + TOK_TILE, D), F32)], seq)
            states["c_p"] = st[:, C_HALO - (CONV_C_WIDTH - 1):][None]
            hist = jnp.transpose(state_conv_c[0], (1, 0, 2))
            x1_s, h2_s, lg_s, new =_mix_sample(_mix_c_sample_kernel, "mix_c_sample", xs, ada_s, [hist] + wts)
            states["c_s"] = jnp.concatenate([state_conv_c[0][:, 1:], new[:, None]], axis=1)[None]
        else:
            common = [d_w_in[0].astype(BF16), _row(d_b_in[0]), _row(d_ln_g[0]), _row(d_ln_b[0])]
            wts = common + [d_w_s[0], d_b_s[0].T, d_w_out[0].astype(BF16)] + ln1
            x1_p, h2_p, lg_p, st =_mix_prompt(_mix_d_prompt_kernel, "mix_d_prompt", xp, ada_p, wts, CHUNK, [], seq)
            states["d_p"] = st[None]
            w_s0 = jnp.repeat(d_w_s[0][:, 0, 0], SGU_GROUP).reshape(1, D)
            b_s0 = jnp.repeat(d_b_s[0][:, 0], SGU_GROUP).reshape(1, D)
            wts_s = common + [w_s0, b_s0, d_w_out[0].astype(BF16)] + ln1
            x1_s, h2_s, lg_s, new =_mix_sample(_mix_d_sample_kernel, "mix_d_sample", xs, ada_s, wts_s)
            states["d_s"] = new[:, None][None]

        xp, xs = _moe_and_norm(h2_p, h2_s, lg_p, lg_s, x1_p, x1_s, ada_p, ada_s, _row(ln2_g[i]), _row(ln2_b[i]),
                               i, w_moe_in, b_moe_in4, w_moe_out, b_moe_out4, seq)

    return (xp.reshape(bp, seq, D), xs.reshape(bs, 1, D),
            states["a_p"], states["a_s"], states["b_p"], states["b_s"],
            states["c_p"], states["c_s"], states["d_p"], states["d_s"])
```

```python
import functools

import jax
import jax.numpy as jnp
from jax import lax
from jax.experimental import pallas as pl
from jax.experimental.pallas import tpu as pltpu

F32 = jnp.float32
BF16 = jnp.bfloat16

D = 1024
DEPTH = 4
N_EXPERTS = 32
TOP_K = 4
CONV_A_WIDTH = 31
POOL_WINDOWS = (2, 4, 8, 16)
POOL_GROUP = D // 4
POOL_HIST = 15
CONV_C_WIDTH = 3
CHUNK = 128
N_SGU_GROUPS = 4
SGU_GROUP = D // N_SGU_GROUPS
SWIGLU_LIMIT = 7.0
SWIGLU_ALPHA = 1.702
ALPHA = (2 * DEPTH) ** 0.25
LN_EPS = 1e-5

LANES = 128
SUBLANES_BF16 = 16
VMEM_LIMIT = 56 * 1024 * 1024

TOK_TILE = 512
MOE_TILE = 512
ROW_CHUNK = SUBLANES_BF16
BLOCK_CHUNKS = 32
BLOCK_ROWS = BLOCK_CHUNKS * ROW_CHUNK
HALF_CHUNKS = BLOCK_CHUNKS // 2
HALF_ROWS = BLOCK_ROWS // 2


def _tile_capacity(tt):
    cap = TOP_K * tt + N_EXPERTS * (ROW_CHUNK - 1)
    return -(-cap // 256) * 256


def _cparams(sem=None):
    return pltpu.CompilerParams(dimension_semantics=sem, vmem_limit_bytes=VMEM_LIMIT)


def _dot(a, b):
    return jnp.dot(a, b, preferred_element_type=F32)


def _dot_split(a, b):
    hi = a.astype(BF16)
    lo = (a - hi.astype(F32)).astype(BF16)
    return _dot(hi, b) + _dot(lo, b)


def _ln(x, g, b):
    mu = jnp.mean(x, axis=-1, keepdims=True)
    xc = x - mu
    var = jnp.mean(xc * xc, axis=-1, keepdims=True)
    return xc * lax.rsqrt(var + LN_EPS) * g + b


def _split_ada(ada):
    return [ada[:, i * D:(i + 1) * D] for i in range(6)]


def _post_mixer(x, out, ada, ln_g, ln_b, wr_ref, br_ref, x1_ref, h2_ref, lg_ref):
    _, _, g1, sh2, sc2, _ = _split_ada(ada)
    x1 = _ln(ALPHA * x + (1.0 + g1) * out, ln_g, ln_b)
    x1_ref[...] = x1
    h2 = x1 * (1.0 + sc2) + sh2
    hi = h2.astype(BF16)
    h2_ref[...] = hi
    lo = (h2 - hi.astype(F32)).astype(BF16)
    wr = wr_ref[...]
    w_hi = wr.astype(BF16)
    w_lo = (wr - w_hi.astype(F32)).astype(BF16)
    both = _dot(hi, jnp.concatenate([w_hi, w_lo], axis=1))
    lg_ref[...] = both[:, :LANES] + both[:, LANES:] + _dot(lo, w_hi) + br_ref[...]


def _ada_kernel(c_ref, w_ref, b_ref, o_ref):
    c = c_ref[...]
    sc = (c * jax.nn.sigmoid(c)).astype(BF16)
    o_ref[0] = _dot(sc, w_ref[0].astype(BF16)) + b_ref[0]


def _ada(c_all, w_ada, b_ada):
    rows = c_all.shape[0]
    return pl.pallas_call(
        _ada_kernel,
        out_shape=jax.ShapeDtypeStruct((DEPTH, rows, 6 * D), F32),
        grid=(DEPTH, 6),
        in_specs=[
            pl.BlockSpec((rows, D), lambda i, j: (0, 0)),
            pl.BlockSpec((1, D, D), lambda i, j: (i, 0, j)),
            pl.BlockSpec((1, 1, D), lambda i, j: (i, 0, j)),
        ],
        out_specs=pl.BlockSpec((1, rows, D), lambda i, j: (i, 0, j)),
        compiler_params=_cparams(("arbitrary", "arbitrary")),
        name="ada",
    )(c_all, w_ada, b_ada.reshape(DEPTH, 1, 6 * D))


A_HALO = 32
B_HALO = 16
C_HALO = 8
CONV_ROWS = 64


def _mix_a_prompt_kernel(x_ref, ada_ref, w_in_ref, b_in_ref, w_dw_ref, b_dw_ref, lng_ref, lnb_ref,
                         w_out_ref, ln1g_ref, ln1b_ref, wr_ref, br_ref, x1_ref, h2_ref, lg_ref, st_ref,
                         u_scr, shift_scr, y_scr):
    t = pl.program_id(1)
    tt = x_ref.shape[0]
    x = x_ref[...]
    ada = ada_ref[0]
    sh1, sc1 = ada[:, 0:D], ada[:, D:2 * D]
    h = (x * (1.0 + sc1) + sh1).astype(BF16)
    z = _dot(h, w_in_ref[...]) + b_in_ref[...]
    u = z[:, :D] * jax.nn.sigmoid(z[:, D:])

    @pl.when(t == 0)
    def _():
        u_scr[0:A_HALO, :] = jnp.zeros((A_HALO, D), F32)

    u_scr[A_HALO:A_HALO + tt, :] = u
    first = A_HALO - (CONV_A_WIDTH - 1)
    for lb in range(D // LANES):
        lanes = slice(lb * LANES, (lb + 1) * LANES)
        for s in range(8):
            rows = tt + A_HALO - (0 if s == 0 else 8)
            shift_scr[s, 0:rows, :] = u_scr[s:s + rows, lanes]

        def row_block(rb, carry, lanes=lanes):
            r0 = rb * CONV_ROWS
            acc = jnp.broadcast_to(b_dw_ref[:, lanes], (CONV_ROWS, LANES))
            for k in range(CONV_A_WIDTH):
                q, s = divmod(first + k, 8)
                start = pl.multiple_of(r0 + 8 * q, 8)
                acc = acc + w_dw_ref[k:k + 1, lanes] * shift_scr[s, pl.ds(start, CONV_ROWS), :]
            y_scr[pl.ds(pl.multiple_of(r0, CONV_ROWS), CONV_ROWS), lanes] = acc
            return carry

        lax.fori_loop(0, tt // CONV_ROWS, row_block, 0)
    y = _ln(y_scr[...], lng_ref[...], lnb_ref[...])
    y = y * jax.nn.sigmoid(y)
    out = _dot(y.astype(BF16), w_out_ref[...])
    _post_mixer(x, out, ada, ln1g_ref[...], ln1b_ref[...], wr_ref, br_ref, x1_ref, h2_ref, lg_ref)
    tail = u_scr[tt:tt + A_HALO, :]
    u_scr[0:A_HALO, :] = tail

    @pl.when(t == pl.num_programs(1) - 1)
    def _():
        st_ref[0] = tail


def _mix_b_prompt_kernel(x_ref, ada_ref, w_grp_ref, scale_ref, ln1g_ref, ln1b_ref,
                         wr_ref, br_ref, x1_ref, h2_ref, lg_ref, st_ref, h_scr):
    t = pl.program_id(1)
    tt = x_ref.shape[0]
    x = x_ref[...]
    ada = ada_ref[0]
    sh1, sc1 = ada[:, 0:D], ada[:, D:2 * D]
    h = x * (1.0 + sc1) + sh1

    @pl.when(t == 0)
    def _():
        h_scr[0:B_HALO, :] = jnp.zeros((B_HALO, D), F32)

    h_scr[B_HALO:B_HALO + tt, :] = h
    pos = (t * tt + lax.broadcasted_iota(jnp.int32, (tt, POOL_GROUP), 0)).astype(F32)
    outs = []
    for gi, w in enumerate(POOL_WINDOWS):
        lo = gi * POOL_GROUP
        s = h_scr[B_HALO:B_HALO + tt, lo:lo + POOL_GROUP]
        for j in range(1, w):
            s = s + h_scr[B_HALO - j:B_HALO - j + tt, lo:lo + POOL_GROUP]
        cnt = jnp.minimum(float(w), pos + 1.0)
        pooled = s / cnt - h[:, lo:lo + POOL_GROUP]
        outs.append(_dot_split(pooled, w_grp_ref[gi].astype(BF16)))
    out = jnp.concatenate(outs, axis=-1) * scale_ref[...]
    _post_mixer(x, out, ada, ln1g_ref[...], ln1b_ref[...], wr_ref, br_ref, x1_ref, h2_ref, lg_ref)
    tail = h_scr[tt:tt + B_HALO, :]
    h_scr[0:B_HALO, :] = tail

    @pl.when(t == pl.num_programs(1) - 1)
    def _():
        st_ref[0] = tail


def _mix_c_prompt_kernel(x_ref, ada_ref, w_in_ref, w_conv_ref, w_out_ref, ln1g_ref, ln1b_ref,
                         wr_ref, br_ref, x1_ref, h2_ref, lg_ref, st_ref, v_scr):
    t = pl.program_id(1)
    tt = x_ref.shape[0]
    x = x_ref[...]
    ada = ada_ref[0]
    sh1, sc1 = ada[:, 0:D], ada[:, D:2 * D]
    h = (x * (1.0 + sc1) + sh1).astype(BF16)
    z = _dot(h, w_in_ref[...])
    bg = z[:, :D]
    v = z[:, D:2 * D] * z[:, 2 * D:]

    @pl.when(t == 0)
    def _():
        v_scr[0:C_HALO, :] = jnp.zeros((C_HALO, D), F32)

    v_scr[C_HALO:C_HALO + tt, :] = v
    first = C_HALO - (CONV_C_WIDTH - 1)
    y = w_conv_ref[CONV_C_WIDTH - 1:CONV_C_WIDTH, :] * v
    for k in range(CONV_C_WIDTH - 1):
        y = y + w_conv_ref[k:k + 1, :] * v_scr[first + k:first + k + tt, :]
    out = _dot((bg * y).astype(BF16), w_out_ref[...])
    _post_mixer(x, out, ada, ln1g_ref[...], ln1b_ref[...], wr_ref, br_ref, x1_ref, h2_ref, lg_ref)
    tail = v_scr[tt:tt + C_HALO, :]
    v_scr[0:C_HALO, :] = tail

    @pl.when(t == pl.num_programs(1) - 1)
    def _():
        st_ref[0] = tail


def _gelu_exact(x):
    return 0.5 * x * (1.0 + lax.erf(x * (2.0 ** -0.5)))


def _mix_d_prompt_kernel(x_ref, ada_ref, w_in_ref, b_in_ref, lng_ref, lnb_ref, w_s_ref, b_st_ref,
                         w_out_ref, ln1g_ref, ln1b_ref, wr_ref, br_ref, x1_ref, h2_ref, lg_ref, st_ref):
    t = pl.program_id(1)
    tt = x_ref.shape[0]
    x = x_ref[...]
    ada = ada_ref[0]
    sh1, sc1 = ada[:, 0:D], ada[:, D:2 * D]
    h = (x * (1.0 + sc1) + sh1).astype(BF16)
    z = _gelu_exact(_dot(h, w_in_ref[...]) + b_in_ref[...])
    u = z[:, :D]
    v = _ln(z[:, D:], lng_ref[...], lnb_ref[...])
    vb = v.astype(BF16)
    row = lax.broadcasted_iota(jnp.int32, (CHUNK, CHUNK), 0)
    col = lax.broadcasted_iota(jnp.int32, (CHUNK, CHUNK), 1)
    causal = col <= row
    chunks = []
    for c in range(tt // CHUNK):
        groups = []
        for g in range(N_SGU_GROUPS):
            w = jnp.where(causal, w_s_ref[g], 0.0).astype(BF16)
            m = _dot(w, vb[c * CHUNK:(c + 1) * CHUNK, g * SGU_GROUP:(g + 1) * SGU_GROUP])
            groups.append(m + b_st_ref[:, g:g + 1])
        chunks.append(jnp.concatenate(groups, axis=-1))
    mixed = jnp.concatenate(chunks, axis=0)
    out = _dot((u * mixed).astype(BF16), w_out_ref[...])
    _post_mixer(x, out, ada, ln1g_ref[...], ln1b_ref[...], wr_ref, br_ref, x1_ref, h2_ref, lg_ref)

    @pl.when(t == pl.num_programs(1) - 1)
    def _():
        st_ref[0] = v[tt - CHUNK:, :]


def _full(shape):
    nd = len(shape)
    return pl.BlockSpec(shape, lambda b, t: (0,) * nd)


def _mix_prompt(kernel_fn, name, x, ada_p, weights, state_rows, scratch, seq):
    n = x.shape[0]
    batch = n // seq
    nt = seq // TOK_TILE
    tok = pl.BlockSpec((TOK_TILE, D), lambda b, t: (b * nt + t, 0))
    lgt = pl.BlockSpec((TOK_TILE, LANES), lambda b, t: (b * nt + t, 0))
    return pl.pallas_call(
        kernel_fn,
        out_shape=(jax.ShapeDtypeStruct((n, D), F32), jax.ShapeDtypeStruct((n, D), BF16),
                   jax.ShapeDtypeStruct((n, LANES), F32), jax.ShapeDtypeStruct((batch, state_rows, D), F32)),
        grid=(batch, nt),
        in_specs=[tok, pl.BlockSpec((1, 1, 6 * D), lambda b, t: (b, 0, 0))] + [_full(w.shape) for w in weights],
        out_specs=(tok, tok, lgt, pl.BlockSpec((1, state_rows, D), lambda b, t: (b, 0, 0))),
        scratch_shapes=scratch,
        compiler_params=_cparams(("arbitrary", "arbitrary")),
        name=name,
    )(x, ada_p, *weights)


def _mix_a_sample_kernel(x_ref, ada_ref, hist_ref, w_in_ref, b_in_ref, w_dw_ref, b_dw_ref, lng_ref, lnb_ref,
                         w_out_ref, ln1g_ref, ln1b_ref, wr_ref, br_ref, x1_ref, h2_ref, lg_ref, new_ref):
    x = x_ref[...]
    ada = ada_ref[...]
    sh1, sc1 = ada[:, 0:D], ada[:, D:2 * D]
    h = (x * (1.0 + sc1) + sh1).astype(BF16)
    z = _dot(h, w_in_ref[...]) + b_in_ref[...]
    u = z[:, :D] * jax.nn.sigmoid(z[:, D:])
    y = b_dw_ref[...] + w_dw_ref[CONV_A_WIDTH - 1:CONV_A_WIDTH, :] * u
    for k in range(CONV_A_WIDTH - 1):
        y = y + w_dw_ref[k:k + 1, :] * hist_ref[k]
    y = _ln(y, lng_ref[...], lnb_ref[...])
    y = y * jax.nn.sigmoid(y)
    out = _dot(y.astype(BF16), w_out_ref[...])
    _post_mixer(x, out, ada, ln1g_ref[...], ln1b_ref[...], wr_ref, br_ref, x1_ref, h2_ref, lg_ref)
    new_ref[...] = u


def _mix_b_sample_kernel(x_ref, ada_ref, hist_ref, w_grp_ref, scale_ref, ln1g_ref, ln1b_ref,
                         wr_ref, br_ref, x1_ref, h2_ref, lg_ref, new_ref):
    x = x_ref[...]
    ada = ada_ref[...]
    sh1, sc1 = ada[:, 0:D], ada[:, D:2 * D]
    h = x * (1.0 + sc1) + sh1
    outs = []
    for gi, w in enumerate(POOL_WINDOWS):
        lo = gi * POOL_GROUP
        s = h[:, lo:lo + POOL_GROUP]
        for j in range(1, w):
            s = s + hist_ref[POOL_HIST - j][:, lo:lo + POOL_GROUP]
        pooled = s / float(w) - h[:, lo:lo + POOL_GROUP]
        outs.append(_dot_split(pooled, w_grp_ref[gi].astype(BF16)))
    out = jnp.concatenate(outs, axis=-1) * scale_ref[...]
    _post_mixer(x, out, ada, ln1g_ref[...], ln1b_ref[...], wr_ref, br_ref, x1_ref, h2_ref, lg_ref)
    new_ref[...] = h


def _mix_c_sample_kernel(x_ref, ada_ref, hist_ref, w_in_ref, w_conv_ref, w_out_ref, ln1g_ref, ln1b_ref,
                         wr_ref, br_ref, x1_ref, h2_ref, lg_ref, new_ref):
    x = x_ref[...]
    ada = ada_ref[...]
    sh1, sc1 = ada[:, 0:D], ada[:, D:2 * D]
    h = (x * (1.0 + sc1) + sh1).astype(BF16)
    z = _dot(h, w_in_ref[...])
    bg = z[:, :D]
    v = z[:, D:2 * D] * z[:, 2 * D:]
    y = w_conv_ref[CONV_C_WIDTH - 1:CONV_C_WIDTH, :] * v
    for k in range(CONV_C_WIDTH - 1):
        y = y + w_conv_ref[k:k + 1, :] * hist_ref[k]
    out = _dot((bg * y).astype(BF16), w_out_ref[...])
    _post_mixer(x, out, ada, ln1g_ref[...], ln1b_ref[...], wr_ref, br_ref, x1_ref, h2_ref, lg_ref)
    new_ref[...] = v


def _mix_d_sample_kernel(x_ref, ada_ref, w_in_ref, b_in_ref, lng_ref, lnb_ref, w_s0_ref, b_s0_ref,
                         w_out_ref, ln1g_ref, ln1b_ref, wr_ref, br_ref, x1_ref, h2_ref, lg_ref, new_ref):
    x = x_ref[...]
    ada = ada_ref[...]
    sh1, sc1 = ada[:, 0:D], ada[:, D:2 * D]
    h = (x * (1.0 + sc1) + sh1).astype(BF16)
    z = _gelu_exact(_dot(h, w_in_ref[...]) + b_in_ref[...])
    u = z[:, :D]
    v = _ln(z[:, D:], lng_ref[...], lnb_ref[...])
    mixed = w_s0_ref[...] * v + b_s0_ref[...]
    out = _dot((u * mixed).astype(BF16), w_out_ref[...])
    _post_mixer(x, out, ada, ln1g_ref[...], ln1b_ref[...], wr_ref, br_ref, x1_ref, h2_ref, lg_ref)
    new_ref[...] = v


def _mix_sample(kernel_fn, name, x, ada_s, arrays):
    n = x.shape[0]
    return pl.pallas_call(
        kernel_fn,
        out_shape=(jax.ShapeDtypeStruct((n, D), F32), jax.ShapeDtypeStruct((n, D), BF16),
                   jax.ShapeDtypeStruct((n, LANES), F32), jax.ShapeDtypeStruct((n, D), F32)),
        compiler_params=_cparams(),
        name=name,
    )(x, ada_s, *arrays)


def _route_kernel(hp_ref, hs_ref, lgp_ref, lgs_ref, xs_ref, meta_ref, cnt_ref, *, n_valid):
    i = pl.program_id(0)
    tt = hp_ref.shape[0]
    ns = hs_ref.shape[0]
    cap = xs_ref.shape[0]
    is_sample = i == pl.num_programs(0) - 1
    hs = jnp.concatenate([hs_ref[...], jnp.zeros((tt - ns, D), BF16)], axis=0)
    lgs = jnp.concatenate([lgs_ref[...], jnp.zeros((tt - ns, LANES), F32)], axis=0)
    hb = jnp.where(is_sample, hs, hp_ref[...])
    logits = jnp.where(is_sample, lgs, lgp_ref[...]).T[:N_EXPERTS, :]
    e_iota = lax.broadcasted_iota(jnp.int32, (N_EXPERTS, tt), 0)
    valid = (i * tt + lax.broadcasted_iota(jnp.int32, (1, tt), 1)) < n_valid
    work = logits
    sel, top = [], []
    for _ in range(TOP_K):
        m = jnp.max(work, axis=0, keepdims=True)
        idx = jnp.min(jnp.where(work == m, e_iota, N_EXPERTS), axis=0, keepdims=True)
        oh = e_iota == idx
        sel.append(oh)
        top.append(m)
        work = jnp.where(oh, -jnp.inf, work)
    ex = [jnp.exp(v - top[0]) for v in top]
    denom = ex[0] + ex[1] + ex[2] + ex[3]
    gates = [e / denom for e in ex]

    member = jnp.where(sel[0] | sel[1] | sel[2] | sel[3], 1.0, 0.0)
    member = jnp.where(valid, member, 0.0)
    r_i = lax.broadcasted_iota(jnp.int32, (tt, tt), 0)
    c_i = lax.broadcasted_iota(jnp.int32, (tt, tt), 1)
    before = jnp.where(r_i < c_i, 1.0, 0.0).astype(BF16)
    rank = _dot(member.astype(BF16), before)
    count = jnp.sum(member, axis=1, keepdims=True)
    chunks = jnp.floor((count + float(ROW_CHUNK - 1)) * (1.0 / ROW_CHUNK))
    chunks_b = jnp.broadcast_to(chunks, (N_EXPERTS, LANES))
    er = lax.broadcasted_iota(jnp.int32, (N_EXPERTS, N_EXPERTS), 0)
    ec = lax.broadcasted_iota(jnp.int32, (N_EXPERTS, N_EXPERTS), 1)
    lower = jnp.where(ec < er, 1.0, 0.0).astype(BF16)
    chunk_off = _dot(lower, chunks_b.astype(BF16))
    base = chunk_off[:, 0:1] * float(ROW_CHUNK)
    slot = base + rank
    pos = [jnp.where(valid, jnp.sum(jnp.where(s, slot, 0.0), axis=0, keepdims=True), -1.0) for s in sel]

    row = lax.broadcasted_iota(jnp.int32, (cap, tt), 0).astype(jnp.int16)
    pos16 = [p.astype(jnp.int32).astype(jnp.int16) for p in pos]
    one, zero = jnp.ones((), BF16), jnp.zeros((), BF16)
    onehot = jnp.where(row == pos16[0], one, zero)
    for k in range(1, TOP_K):
        onehot = jnp.where(row == pos16[k], one, onehot)
    xs_ref[...] = _dot(onehot, hb).astype(BF16)

    meta = jnp.concatenate(pos + gates + [jnp.zeros((LANES - 2 * TOP_K, tt), F32)], axis=0)
    meta_ref[...] = meta.T
    cnt_ref[0] = chunks_b.astype(jnp.int32)


def _route(h2_p, h2_s, lg_p, lg_s):
    n_p, n_s = h2_p.shape[0], h2_s.shape[0]
    ntp = n_p // MOE_TILE
    nt = ntp + 1
    cap = _tile_capacity(MOE_TILE)
    prompt_tile = lambda i: (jnp.minimum(i, ntp - 1), 0)
    return pl.pallas_call(
        functools.partial(_route_kernel, n_valid=n_p + n_s),
        out_shape=(jax.ShapeDtypeStruct((nt * cap, D), BF16),
                   jax.ShapeDtypeStruct((nt * MOE_TILE, LANES), F32),
                   jax.ShapeDtypeStruct((nt, N_EXPERTS, LANES), jnp.int32)),
        grid=(nt,),
        in_specs=[pl.BlockSpec((MOE_TILE, D), prompt_tile),
                  pl.BlockSpec((n_s, D), lambda i: (0, 0)),
                  pl.BlockSpec((MOE_TILE, LANES), prompt_tile),
                  pl.BlockSpec((n_s, LANES), lambda i: (0, 0))],
        out_specs=(pl.BlockSpec((cap, D), lambda i: (i, 0)),
                   pl.BlockSpec((MOE_TILE, LANES), lambda i: (i, 0)),
                   pl.BlockSpec((1, N_EXPERTS, LANES), lambda i: (i, 0, 0))),
        compiler_params=_cparams(("arbitrary",)),
        name="route_dispatch",
    )(h2_p, h2_s, lg_p, lg_s)


def _plan_kernel(cnt_ref, be_ref, nx_ref, half_ref, nb_ref, src_ref, dst_ref, z_ref, nz_ref, tile_off, blk0_s,
                 nblk_s, npad_s, *, nt, cap_chunks, nb_max, max_pad, nz_max):
    del nb_max, nz_max
    i32 = jnp.int32
    trash_base = nt * cap_chunks
    shift = BLOCK_CHUNKS.bit_length() - 1

    def index_grid(ref):
        r, c = ref.shape
        return lax.broadcasted_iota(i32, (r, c), 0) * c + lax.broadcasted_iota(i32, (r, c), 1)

    def in_run(idx, start, length):
        return jnp.logical_and(idx >= start, idx < start + length)

    for t in range(nt):
        tile_off[t] = 0

    d = index_grid(src_ref)

    def expert_body(e, carry):
        p0, blk0, pad0, src, dst = carry

        def tile_body(t, inner):
            p, src, dst = inner
            c = cnt_ref[t * N_EXPERTS + e]
            off = tile_off[t]
            tile_off[t] = off + c
            run = in_run(d, p, c)
            val = d + (t * cap_chunks + off - p)
            return p + c, jnp.where(run, val, src), jnp.where(run, val, dst)

        p1, src, dst = lax.fori_loop(0, nt, tile_body, (p0, src, dst))
        nblk = lax.shift_right_logical(p1 - p0 + (BLOCK_CHUNKS - 1), shift)
        p2 = p0 + nblk * BLOCK_CHUNKS
        pad = in_run(d, p1, p2 - p1)
        src = jnp.where(pad, 0, src)
        dst = jnp.where(pad, d + (trash_base + pad0 - p1), dst)
        blk0_s[e] = blk0
        nblk_s[e] = nblk
        npad_s[e] = p2 - p1
        return p2, blk0 + nblk, pad0 + (p2 - p1), src, dst

    zeros = jnp.zeros(src_ref.shape, i32)
    zero = jnp.int32(0)
    _, nb, n_pad, src, dst = lax.fori_loop(0, N_EXPERTS, expert_body, (zero, zero, zero, zeros, zeros))
    src_ref[...] = src
    dst_ref[...] = dst
    nb_ref[0] = nb

    bi = index_grid(be_ref)

    def block_body(k, carry):
        be, nx, half, nxt = carry
        e = N_EXPERTS - 1 - k
        b0, n = blk0_s[e], nblk_s[e]
        mine = in_run(bi, b0, n)
        last = jnp.logical_and(bi == b0 + n - 1, jnp.logical_and(n > 0, npad_s[e] >= HALF_CHUNKS))
        return (jnp.where(mine, e, be), jnp.where(mine, nxt, nx), jnp.where(last, 1, half),
                jnp.where(n > 0, e, nxt))

    init = (jnp.full(be_ref.shape, N_EXPERTS - 1, i32), jnp.full(be_ref.shape, -1, i32),
            jnp.zeros(be_ref.shape, i32), jnp.int32(-1))
    be, nx, half, _ = lax.fori_loop(0, N_EXPERTS, block_body, init)
    be_ref[...] = be
    nx_ref[...] = nx
    half_ref[...] = half

    zi = index_grid(z_ref)

    def tail_body(t, carry):
        z0, z = carry
        used = tile_off[t]
        n = cap_chunks - used
        return z0 + n, jnp.where(in_run(zi, z0, n), zi + (t * cap_chunks + used - z0), z)

    z1, z = lax.fori_loop(0, nt, tail_body, (zero, jnp.zeros(z_ref.shape, i32)))
    n_trash = max_pad - n_pad
    z_ref[...] = jnp.where(in_run(zi, z1, n_trash), zi + (trash_base + n_pad - z1), z)
    nz_ref[0] = z1 + n_trash


def _plan(cnt_flat, nt, cap_chunks, nb_max, max_pad, nz_max):
    smem = pl.BlockSpec(memory_space=pltpu.SMEM)
    vmem = pl.BlockSpec(memory_space=pltpu.VMEM)
    i32 = jnp.int32

    def table(n):
        return jax.ShapeDtypeStruct((-(-n // (8 * LANES)) * 8, LANES), i32)

    out = pl.pallas_call(
        functools.partial(_plan_kernel, nt=nt, cap_chunks=cap_chunks, nb_max=nb_max, max_pad=max_pad, nz_max=nz_max),
        out_shape=(table(nb_max), table(nb_max), table(nb_max), jax.ShapeDtypeStruct((1,), i32),
                   table(nb_max * BLOCK_CHUNKS), table(nb_max * BLOCK_CHUNKS),
                   table(nz_max), jax.ShapeDtypeStruct((1,), i32)),
        in_specs=[smem],
        out_specs=(vmem, vmem, vmem, smem, vmem, vmem, vmem, smem),
        scratch_shapes=[pltpu.SMEM((nt,), i32), pltpu.SMEM((N_EXPERTS,), i32), pltpu.SMEM((N_EXPERTS,), i32),
                        pltpu.SMEM((N_EXPERTS,), i32)],
        name="plan",
    )(cnt_flat)
    return tuple(o.reshape(-1) for o in out)


def _expert_kernel(be_ref, nx_ref, half_ref, nb_ref, src_ref, dst_ref, z_ref, nz_ref, xs_hbm, w_in_hbm, b_in_ref,
                   w_out_hbm, b_out_ref, ys_hbm, xbuf, ybuf, w_in_f32, w_out_f32, w_in_bf, w_out_bf, zbuf,
                   sem_in, sem_out, sem_z, sem_w, *, layer):
    b = pl.program_id(0)
    nb = nb_ref[0]
    nz = nz_ref[0]
    slot = lax.rem(b, 2)

    def weight_copies(e):
        return (pltpu.make_async_copy(w_in_hbm.at[layer, e], w_in_f32, sem_w.at[0]),
                pltpu.make_async_copy(w_out_hbm.at[layer, e], w_out_f32, sem_w.at[1]))

    def ffn(rows):
        e = be_ref[b]
        x = xbuf[slot, 0:rows, :]
        h = _dot(x, w_in_bf[...]) + b_in_ref[layer, e]
        g = jnp.minimum(h[:, :D], SWIGLU_LIMIT)
        u = jnp.clip(h[:, D:], -SWIGLU_LIMIT, SWIGLU_LIMIT)
        a = (u + 1.0) * (g * jax.nn.sigmoid(SWIGLU_ALPHA * g))
        y = _dot(a.astype(BF16), w_out_bf[...]) + b_out_ref[layer, e]
        ybuf[slot, 0:rows, :] = y.astype(BF16)

    def chunk_rows(c):
        return pl.ds(pl.multiple_of(c * ROW_CHUNK, ROW_CHUNK), ROW_CHUNK)

    def gather(blk, s):
        for j in range(BLOCK_CHUNKS):
            c = src_ref[blk * BLOCK_CHUNKS + j]
            pltpu.make_async_copy(xs_hbm.at[chunk_rows(c)], xbuf.at[s, pl.ds(j * ROW_CHUNK, ROW_CHUNK)],
                                  sem_in.at[s]).start()

    def gather_wait(s):
        pltpu.make_async_copy(xs_hbm.at[pl.ds(0, BLOCK_ROWS)], xbuf.at[s], sem_in.at[s]).wait()

    def scatter(blk, s):
        for j in range(BLOCK_CHUNKS):
            c = dst_ref[blk * BLOCK_CHUNKS + j]
            pltpu.make_async_copy(ybuf.at[s, pl.ds(j * ROW_CHUNK, ROW_CHUNK)], ys_hbm.at[chunk_rows(c)],
                                  sem_out.at[s]).start()

    def scatter_wait(s):
        pltpu.make_async_copy(ybuf.at[s], ys_hbm.at[pl.ds(0, BLOCK_ROWS)], sem_out.at[s]).wait()

    @pl.when(b == 0)
    def _():
        for c in weight_copies(be_ref[0]):
            c.start()
        gather(0, 0)
        zbuf[...] = jnp.zeros(zbuf.shape, zbuf.dtype)
        ybuf[...] = jnp.zeros(ybuf.shape, ybuf.dtype)

        def zero_one(i, _):
            pltpu.make_async_copy(zbuf, ys_hbm.at[chunk_rows(z_ref[i])], sem_z).start()
            return 0

        lax.fori_loop(0, nz, zero_one, 0)

    @pl.when(b < nb)
    def _():
        new_expert = jnp.logical_or(b == 0, be_ref[b] != be_ref[jnp.maximum(b - 1, 0)])

        @pl.when(new_expert)
        def _():
            for c in weight_copies(be_ref[b]):
                c.wait()
            w_in_bf[...] = w_in_f32[...].astype(BF16)
            w_out_bf[...] = w_out_f32[...].astype(BF16)

            @pl.when(nx_ref[b] >= 0)
            def _():
                for c in weight_copies(nx_ref[b]):
                    c.start()

        gather_wait(slot)

        @pl.when(b + 1 < nb)
        def _():
            gather(b + 1, 1 - slot)

        @pl.when(b >= 2)
        def _():
            scatter_wait(slot)

        @pl.when(half_ref[b] == 0)
        def _():
            ffn(BLOCK_ROWS)

        @pl.when(half_ref[b] != 0)
        def _():
            ffn(HALF_ROWS)

        scatter(b, slot)

        @pl.when(b == 0)
        def _():
            def zero_wait(i, _):
                pltpu.make_async_copy(zbuf, ys_hbm.at[pl.ds(0, ROW_CHUNK)], sem_z).wait()
                return 0

            lax.fori_loop(0, nz, zero_wait, 0)

        @pl.when(b == nb - 1)
        def _():
            scatter_wait(slot)

            @pl.when(b >= 1)
            def _():
                scatter_wait(1 - slot)


def _experts(xs, plan, layer, w_in, b_in, w_out, b_out, nb_max, ys_rows):
    resident = lambda b, *_: (0, 0, 0, 0)
    grid_spec = pltpu.PrefetchScalarGridSpec(
        num_scalar_prefetch=8,
        grid=(nb_max,),
        in_specs=[pl.BlockSpec(memory_space=pl.ANY),
                  pl.BlockSpec(memory_space=pl.ANY),
                  pl.BlockSpec(b_in.shape, resident),
                  pl.BlockSpec(memory_space=pl.ANY),
                  pl.BlockSpec(b_out.shape, resident)],
        out_specs=pl.BlockSpec(memory_space=pl.ANY),
        scratch_shapes=[pltpu.VMEM((2, BLOCK_ROWS, D), BF16), pltpu.VMEM((2, BLOCK_ROWS, D), BF16),
                        pltpu.VMEM((D, 2 * D), F32), pltpu.VMEM((D, D), F32),
                        pltpu.VMEM((D, 2 * D), BF16), pltpu.VMEM((D, D), BF16), pltpu.VMEM((ROW_CHUNK, D), BF16),
                        pltpu.SemaphoreType.DMA((2,)), pltpu.SemaphoreType.DMA((2,)), pltpu.SemaphoreType.DMA(()),
                        pltpu.SemaphoreType.DMA((2,))],
    )
    return pl.pallas_call(
        functools.partial(_expert_kernel, layer=layer),
        out_shape=jax.ShapeDtypeStruct((ys_rows, D), BF16),
        grid_spec=grid_spec,
        compiler_params=_cparams(("arbitrary",)),
        name="experts",
    )(*plan, xs, w_in, b_in, w_out, b_out)


def _combine_kernel(ys_ref, meta_ref, x1_ref, ada_ref, lng_ref, lnb_ref, o_ref):
    tt = x1_ref.shape[0]
    cap = ys_ref.shape[0]
    meta = meta_ref[...]
    col = lax.broadcasted_iota(jnp.int32, (tt, cap), 1).astype(jnp.int16)
    gmat = jnp.zeros((tt, cap), BF16)
    for k in range(TOP_K):
        pos_k = meta[:, k:k + 1].astype(jnp.int32).astype(jnp.int16)
        gate_k = meta[:, TOP_K + k:TOP_K + k + 1].astype(BF16)
        gmat = jnp.where(col == pos_k, gate_k, gmat)
    f = _dot(gmat, ys_ref[...])
    ada = ada_ref[0] if len(ada_ref.shape) == 3 else ada_ref[...]
    g2 = ada[:, 5 * D:6 * D]
    o_ref[...] = _ln(ALPHA * x1_ref[...] + (1.0 + g2) * f, lng_ref[...], lnb_ref[...])


def _combine_prompt(ys, meta, x1, ada_p, ln_g, ln_b, seq):
    n = x1.shape[0]
    cap = _tile_capacity(MOE_TILE)
    per_seq = seq // MOE_TILE
    return pl.pallas_call(
        _combine_kernel,
        out_shape=jax.ShapeDtypeStruct((n, D), F32),
        grid=(n // MOE_TILE,),
        in_specs=[pl.BlockSpec((cap, D), lambda i: (i, 0)),
                  pl.BlockSpec((MOE_TILE, LANES), lambda i: (i, 0)),
                  pl.BlockSpec((MOE_TILE, D), lambda i: (i, 0)),
                  pl.BlockSpec((1, 1, 6 * D), lambda i: (i // per_seq, 0, 0)),
                  pl.BlockSpec((1, D), lambda i: (0, 0)),
                  pl.BlockSpec((1, D), lambda i: (0, 0))],
        out_specs=pl.BlockSpec((MOE_TILE, D), lambda i: (i, 0)),
        compiler_params=_cparams(("arbitrary",)),
        name="combine_prompt",
    )(ys, meta, x1, ada_p, ln_g, ln_b)


def _combine_sample(ys, meta, x1, ada_s, ln_g, ln_b, tile):
    n = x1.shape[0]
    cap = _tile_capacity(MOE_TILE)
    return pl.pallas_call(
        _combine_kernel,
        out_shape=jax.ShapeDtypeStruct((n, D), F32),
        grid=(1,),
        in_specs=[pl.BlockSpec((cap, D), lambda i: (tile, 0)),
                  pl.BlockSpec((n, LANES), lambda i: (tile * (MOE_TILE // n), 0)),
                  pl.BlockSpec((n, D), lambda i: (0, 0)),
                  pl.BlockSpec((n, 6 * D), lambda i: (0, 0)),
                  pl.BlockSpec((1, D), lambda i: (0, 0)),
                  pl.BlockSpec((1, D), lambda i: (0, 0))],
        out_specs=pl.BlockSpec((n, D), lambda i: (0, 0)),
        compiler_params=_cparams(("arbitrary",)),
        name="combine_sample",
    )(ys, meta, x1, ada_s, ln_g, ln_b)


def _moe_and_norm(h2_p, h2_s, lg_p, lg_s, x1_p, x1_s, ada_p, ada_s, ln_g, ln_b,
                  layer, w_in, b_in, w_out, b_out, seq):
    n_p, n_s = h2_p.shape[0], h2_s.shape[0]
    n_valid = n_p + n_s
    nt = n_p // MOE_TILE + 1
    cap = _tile_capacity(MOE_TILE)
    cap_chunks = cap // ROW_CHUNK
    xs, meta, cnt = _route(h2_p, h2_s, lg_p, lg_s)
    total_chunks_max = (TOP_K * n_valid + nt * N_EXPERTS * (ROW_CHUNK - 1)) // ROW_CHUNK
    nb_max = -(-total_chunks_max // BLOCK_CHUNKS) + N_EXPERTS
    max_pad = N_EXPERTS * (BLOCK_CHUNKS - 1)
    nz_max = nt * cap_chunks - (TOP_K * n_valid) // ROW_CHUNK + max_pad
    plan = _plan(cnt[:, :, 0].reshape(-1), nt, cap_chunks, nb_max, max_pad, nz_max)
    ys = _experts(xs, plan, layer, w_in, b_in, w_out, b_out, nb_max, (nt * cap_chunks + max_pad) * ROW_CHUNK)
    x2_p = _combine_prompt(ys, meta, x1_p, ada_p, ln_g, ln_b, seq)
    x2_s = _combine_sample(ys, meta, x1_s, ada_s, ln_g, ln_b, n_p // MOE_TILE)
    return x2_p, x2_s


def _row(v):
    return v.reshape(1, -1)


def kernel(x_prompt, x_sample, state_conv_a, state_pool_b, state_conv_c, c_prompt, c_sample,
           w_ada, b_ada, ln1_g, ln1_b, ln2_g, ln2_b,
           a_w_in, a_b_in, a_w_dw, a_b_dw, a_ln_g, a_ln_b, a_w_out,
           b_w_grp, b_scale, c_w_in, c_w_conv, c_w_out,
           d_w_in, d_b_in, d_ln_g, d_ln_b, d_w_s, d_b_s, d_w_out,
           w_router, b_router, w_moe_in, b_moe_in, w_moe_out, b_moe_out):
    bp, seq, _ = x_prompt.shape
    bs = x_sample.shape[0]
    assert seq % TOK_TILE == 0 and seq % MOE_TILE == 0 and MOE_TILE % bs == 0
    assert x_sample.shape[1] == 1 and w_ada.shape[0] == DEPTH == 4

    ada = _ada(jnp.concatenate([c_prompt, c_sample], axis=0), w_ada, b_ada)
    xp = x_prompt.reshape(bp * seq, D)
    xs = x_sample.reshape(bs, D)
    b_moe_in4 = b_moe_in.reshape(DEPTH, N_EXPERTS, 1, 2 * D)
    b_moe_out4 = b_moe_out.reshape(DEPTH, N_EXPERTS, 1, D)
    states = {}

    for i in range(DEPTH):
        ada_p = ada[i, :bp].reshape(bp, 1, 6 * D)
        ada_s = ada[i, bp:]
        wr = jnp.pad(w_router[i], ((0, 0), (0, LANES - N_EXPERTS)))
        ln1 = [_row(ln1_g[i]), _row(ln1_b[i]), wr, _row(jnp.pad(b_router[i], (0, LANES - N_EXPERTS)))]
        if i == 0:
            wts = [a_w_in[0].astype(BF16), _row(a_b_in[0]), a_w_dw[0], _row(a_b_dw[0]), _row(a_ln_g[0]),
                   _row(a_ln_b[0]), a_w_out[0].astype(BF16)] + ln1
            x1_p, h2_p, lg_p, st =_mix_prompt(_mix_a_prompt_kernel, "mix_a_prompt", xp, ada_p, wts, A_HALO,
                                         [pltpu.VMEM((A_HALO + TOK_TILE, D), F32),
                                          pltpu.VMEM((8, A_HALO + TOK_TILE, LANES), F32),
                                          pltpu.VMEM((TOK_TILE, D), F32)], seq)
            states["a_p"] = st[:, A_HALO - (CONV_A_WIDTH - 1):][None]
            hist = jnp.transpose(state_conv_a[0], (1, 0, 2))
            x1_s, h2_s, lg_s, new =_mix_sample(_mix_a_sample_kernel, "mix_a_sample", xs, ada_s, [hist] + wts)
            states["a_s"] = jnp.concatenate([state_conv_a[0][:, 1:], new[:, None]], axis=1)[None]
        elif i == 1:
            wts = [b_w_grp[0], _row(b_scale[0])] + ln1
            x1_p, h2_p, lg_p, st =_mix_prompt(_mix_b_prompt_kernel, "mix_b_prompt", xp, ada_p, wts, B_HALO,
                                         [pltpu.VMEM((B_HALO + TOK_TILE, D), F32)], seq)
            states["b_p"] = st[:, B_HALO - POOL_HIST:][None]
            hist = jnp.transpose(state_pool_b[0], (1, 0, 2))
            x1_s, h2_s, lg_s, new =_mix_sample(_mix_b_sample_kernel, "mix_b_sample", xs, ada_s, [hist] + wts)
            states["b_s"] = jnp.concatenate([state_pool_b[0][:, 1:], new[:, None]], axis=1)[None]
        elif i == 2:
            wts = [c_w_in[0].astype(BF16), c_w_conv[0], c_w_out[0].astype(BF16)] + ln1
            x1_p, h2_p, lg_p, st =_mix_prompt(_mix_c_prompt_kernel, "mix_c_prompt", xp, ada_p, wts, C_HALO,
                                         [pltpu.VMEM((C_HALO + TOK_TILE, D), F32)], seq)
            states["c_p"] = st[:, C_HALO - (CONV_C_WIDTH - 1):][None]
            hist = jnp.transpose(state_conv_c[0], (1, 0, 2))
            x1_s, h2_s, lg_s, new =_mix_sample(_mix_c_sample_kernel, "mix_c_sample", xs, ada_s, [hist] + wts)
            states["c_s"] = jnp.concatenate([state_conv_c[0][:, 1:], new[:, None]], axis=1)[None]
        else:
            common = [d_w_in[0].astype(BF16), _row(d_b_in[0]), _row(d_ln_g[0]), _row(d_ln_b[0])]
            wts = common + [d_w_s[0], d_b_s[0].T, d_w_out[0].astype(BF16)] + ln1
            x1_p, h2_p, lg_p, st =_mix_prompt(_mix_d_prompt_kernel, "mix_d_prompt", xp, ada_p, wts, CHUNK, [], seq)
            states["d_p"] = st[None]
            w_s0 = jnp.repeat(d_w_s[0][:, 0, 0], SGU_GROUP).reshape(1, D)
            b_s0 = jnp.repeat(d_b_s[0][:, 0], SGU_GROUP).reshape(1, D)
            wts_s = common + [w_s0, b_s0, d_w_out[0].astype(BF16)] + ln1
            x1_s, h2_s, lg_s, new =_mix_sample(_mix_d_sample_kernel, "mix_d_sample", xs, ada_s, wts_s)
            states["d_s"] = new[:, None][None]

        xp, xs = _moe_and_norm(h2_p, h2_s, lg_p, lg_s, x1_p, x1_s, ada_p, ada_s, _row(ln2_g[i]), _row(ln2_b[i]),
                               i, w_moe_in, b_moe_in4, w_moe_out, b_moe_out4, seq)

    return (xp.reshape(bp, seq, D), xs.reshape(bs, 1, D),
            states["a_p"], states["a_s"], states["b_p"], states["b_s"],
            states["c_p"], states["c_s"], states["d_p"], states["d_s"])
```

```python
import functools

import jax
import jax.numpy as jnp
from jax import lax
from jax.experimental import pallas as pl
from jax.experimental.pallas import tpu as pltpu

F32 = jnp.float32
BF16 = jnp.bfloat16

D = 1024
DEPTH = 4
N_EXPERTS = 32
TOP_K = 4
CONV_A_WIDTH = 31
POOL_WINDOWS = (2, 4, 8, 16)
POOL_GROUP = D // 4
POOL_HIST = 15
CONV_C_WIDTH = 3
CHUNK = 128
N_SGU_GROUPS = 4
SGU_GROUP = D // N_SGU_GROUPS
SWIGLU_LIMIT = 7.0
SWIGLU_ALPHA = 1.702
ALPHA = (2 * DEPTH) ** 0.25
LN_EPS = 1e-5

LANES = 128
SUBLANES_BF16 = 16
VMEM_LIMIT = 56 * 1024 * 1024

TOK_TILE = 512
MOE_TILE = 512
ROW_CHUNK = SUBLANES_BF16
BLOCK_CHUNKS = 32
BLOCK_ROWS = BLOCK_CHUNKS * ROW_CHUNK
HALF_CHUNKS = BLOCK_CHUNKS // 2
HALF_ROWS = BLOCK_ROWS // 2


def _tile_capacity(tt):
    cap = TOP_K * tt + N_EXPERTS * (ROW_CHUNK - 1)
    return -(-cap // 256) * 256


def _cparams(sem=None):
    return pltpu.CompilerParams(dimension_semantics=sem, vmem_limit_bytes=VMEM_LIMIT)


def _dot(a, b):
    return jnp.dot(a, b, preferred_element_type=F32)


def _dot_split(a, b):
    hi = a.astype(BF16)
    lo = (a - hi.astype(F32)).astype(BF16)
    return _dot(hi, b) + _dot(lo, b)


def _ln(x, g, b):
    mu = jnp.mean(x, axis=-1, keepdims=True)
    xc = x - mu
    var = jnp.mean(xc * xc, axis=-1, keepdims=True)
    return xc * lax.rsqrt(var + LN_EPS) * g + b


def _split_ada(ada):
    return [ada[:, i * D:(i + 1) * D] for i in range(6)]


def _post_mixer(x, out, ada, ln_g, ln_b, wr_ref, br_ref, x1_ref, h2_ref, lg_ref):
    _, _, g1, sh2, sc2, _ = _split_ada(ada)
    x1 = _ln(ALPHA * x + (1.0 + g1) * out, ln_g, ln_b)
    x1_ref[...] = x1
    h2 = x1 * (1.0 + sc2) + sh2
    hi = h2.astype(BF16)
    h2_ref[...] = hi
    lo = (h2 - hi.astype(F32)).astype(BF16)
    wr = wr_ref[...]
    w_hi = wr.astype(BF16)
    w_lo = (wr - w_hi.astype(F32)).astype(BF16)
    both = _dot(hi, jnp.concatenate([w_hi, w_lo], axis=1))
    lg_ref[...] = both[:, :LANES] + both[:, LANES:] + _dot(lo, w_hi) + br_ref[...]


def _ada_kernel(c_ref, w_ref, b_ref, o_ref):
    c = c_ref[...]
    sc = (c * jax.nn.sigmoid(c)).astype(BF16)
    o_ref[0] = _dot(sc, w_ref[0].astype(BF16)) + b_ref[0]


def _ada(c_all, w_ada, b_ada):
    rows = c_all.shape[0]
    return pl.pallas_call(
        _ada_kernel,
        out_shape=jax.ShapeDtypeStruct((DEPTH, rows, 6 * D), F32),
        grid=(DEPTH, 6),
        in_specs=[
            pl.BlockSpec((rows, D), lambda i, j: (0, 0)),
            pl.BlockSpec((1, D, D), lambda i, j: (i, 0, j)),
            pl.BlockSpec((1, 1, D), lambda i, j: (i, 0, j)),
        ],
        out_specs=pl.BlockSpec((1, rows, D), lambda i, j: (i, 0, j)),
        compiler_params=_cparams(("arbitrary", "arbitrary")),
        name="ada",
    )(c_all, w_ada, b_ada.reshape(DEPTH, 1, 6 * D))


A_HALO = 32
B_HALO = 32
C_HALO = 8
CONV_ROWS = 64


def _mix_a_prompt_kernel(x_ref, ada_ref, w_in_ref, b_in_ref, w_dw_ref, b_dw_ref, lng_ref, lnb_ref,
                         w_out_ref, ln1g_ref, ln1b_ref, wr_ref, br_ref, x1_ref, h2_ref, lg_ref, st_ref,
                         u_scr, shift_scr, y_scr):
    t = pl.program_id(1)
    tt = x_ref.shape[0]
    x = x_ref[...]
    ada = ada_ref[0]
    sh1, sc1 = ada[:, 0:D], ada[:, D:2 * D]
    h = (x * (1.0 + sc1) + sh1).astype(BF16)
    z = _dot(h, w_in_ref[...]) + b_in_ref[...]
    u = z[:, :D] * jax.nn.sigmoid(z[:, D:])

    @pl.when(t == 0)
    def _():
        u_scr[0:A_HALO, :] = jnp.zeros((A_HALO, D), F32)

    u_scr[A_HALO:A_HALO + tt, :] = u
    first = A_HALO - (CONV_A_WIDTH - 1)
    for lb in range(D // LANES):
        lanes = slice(lb * LANES, (lb + 1) * LANES)
        for s in range(8):
            rows = tt + A_HALO - (0 if s == 0 else 8)
            shift_scr[s, 0:rows, :] = u_scr[s:s + rows, lanes]

        def row_block(rb, carry, lanes=lanes):
            r0 = rb * CONV_ROWS
            acc = jnp.broadcast_to(b_dw_ref[:, lanes], (CONV_ROWS, LANES))
            for k in range(CONV_A_WIDTH):
                q, s = divmod(first + k, 8)
                start = pl.multiple_of(r0 + 8 * q, 8)
                acc = acc + w_dw_ref[k:k + 1, lanes] * shift_scr[s, pl.ds(start, CONV_ROWS), :]
            y_scr[pl.ds(pl.multiple_of(r0, CONV_ROWS), CONV_ROWS), lanes] = acc
            return carry

        lax.fori_loop(0, tt // CONV_ROWS, row_block, 0)
    y = _ln(y_scr[...], lng_ref[...], lnb_ref[...])
    y = y * jax.nn.sigmoid(y)
    out = _dot(y.astype(BF16), w_out_ref[...])
    _post_mixer(x, out, ada, ln1g_ref[...], ln1b_ref[...], wr_ref, br_ref, x1_ref, h2_ref, lg_ref)
    tail = u_scr[tt:tt + A_HALO, :]
    u_scr[0:A_HALO, :] = tail

    @pl.when(t == pl.num_programs(1) - 1)
    def _():
        st_ref[0] = tail


def _mix_b_prompt_kernel(x_ref, ada_ref, w_grp_ref, scale_ref, ln1g_ref, ln1b_ref,
                         wr_ref, br_ref, x1_ref, h2_ref, lg_ref, st_ref, h_scr, pa_scr, pb_scr):
    t = pl.program_id(1)
    tt = x_ref.shape[0]
    x = x_ref[...]
    ada = ada_ref[0]
    sh1, sc1 = ada[:, 0:D], ada[:, D:2 * D]
    h = x * (1.0 + sc1) + sh1

    @pl.when(t == 0)
    def _():
        h_scr[0:B_HALO, :] = jnp.zeros((B_HALO, D), F32)

    h_scr[B_HALO:B_HALO + tt, :] = h
    pos = (t * tt + lax.broadcasted_iota(jnp.int32, (tt, POOL_GROUP), 0)).astype(F32)
    outs = []
    for gi, w in enumerate(POOL_WINDOWS):
        lo = gi * POOL_GROUP
        end = B_HALO + tt
        start, width, level = 8, 2, 0
        cur = h_scr[start:end, lo:lo + POOL_GROUP] + h_scr[start - 1:end - 1, lo:lo + POOL_GROUP]
        while width < w:
            buf = (pa_scr, pb_scr)[level % 2]
            buf[start:end, :] = cur
            cur = buf[start + 8:end, :] + buf[start + 8 - width:end - width, :]
            start, width, level = start + 8, 2 * width, level + 1
        s = cur[B_HALO - start:, :]
        cnt = jnp.minimum(float(w), pos + 1.0)
        pooled = s / cnt - h[:, lo:lo + POOL_GROUP]
        outs.append(_dot_split(pooled, w_grp_ref[gi].astype(BF16)))
    out = jnp.concatenate(outs, axis=-1) * scale_ref[...]
    _post_mixer(x, out, ada, ln1g_ref[...], ln1b_ref[...], wr_ref, br_ref, x1_ref, h2_ref, lg_ref)
    tail = h_scr[tt:tt + B_HALO, :]
    h_scr[0:B_HALO, :] = tail

    @pl.when(t == pl.num_programs(1) - 1)
    def _():
        st_ref[0] = tail


def _mix_c_prompt_kernel(x_ref, ada_ref, w_in_ref, w_conv_ref, w_out_ref, ln1g_ref, ln1b_ref,
                         wr_ref, br_ref, x1_ref, h2_ref, lg_ref, st_ref, v_scr):
    t = pl.program_id(1)
    tt = x_ref.shape[0]
    x = x_ref[...]
    ada = ada_ref[0]
    sh1, sc1 = ada[:, 0:D], ada[:, D:2 * D]
    h = (x * (1.0 + sc1) + sh1).astype(BF16)
    z = _dot(h, w_in_ref[...])
    bg = z[:, :D]
    v = z[:, D:2 * D] * z[:, 2 * D:]

    @pl.when(t == 0)
    def _():
        v_scr[0:C_HALO, :] = jnp.zeros((C_HALO, D), F32)

    v_scr[C_HALO:C_HALO + tt, :] = v
    first = C_HALO - (CONV_C_WIDTH - 1)
    y = w_conv_ref[CONV_C_WIDTH - 1:CONV_C_WIDTH, :] * v
    for k in range(CONV_C_WIDTH - 1):
        y = y + w_conv_ref[k:k + 1, :] * v_scr[first + k:first + k + tt, :]
    out = _dot((bg * y).astype(BF16), w_out_ref[...])
    _post_mixer(x, out, ada, ln1g_ref[...], ln1b_ref[...], wr_ref, br_ref, x1_ref, h2_ref, lg_ref)
    tail = v_scr[tt:tt + C_HALO, :]
    v_scr[0:C_HALO, :] = tail

    @pl.when(t == pl.num_programs(1) - 1)
    def _():
        st_ref[0] = tail


def _gelu_exact(x):
    return 0.5 * x * (1.0 + lax.erf(x * (2.0 ** -0.5)))


def _mix_d_prompt_kernel(x_ref, ada_ref, w_in_ref, b_in_ref, lng_ref, lnb_ref, w_s_ref, b_st_ref,
                         w_out_ref, ln1g_ref, ln1b_ref, wr_ref, br_ref, x1_ref, h2_ref, lg_ref, st_ref):
    t = pl.program_id(1)
    tt = x_ref.shape[0]
    x = x_ref[...]
    ada = ada_ref[0]
    sh1, sc1 = ada[:, 0:D], ada[:, D:2 * D]
    h = (x * (1.0 + sc1) + sh1).astype(BF16)
    z = _gelu_exact(_dot(h, w_in_ref[...]) + b_in_ref[...])
    u = z[:, :D]
    v = _ln(z[:, D:], lng_ref[...], lnb_ref[...])
    vb = v.astype(BF16)
    row = lax.broadcasted_iota(jnp.int32, (CHUNK, CHUNK), 0)
    col = lax.broadcasted_iota(jnp.int32, (CHUNK, CHUNK), 1)
    causal = col <= row
    chunks = []
    for c in range(tt // CHUNK):
        groups = []
        for g in range(N_SGU_GROUPS):
            w = jnp.where(causal, w_s_ref[g], 0.0).astype(BF16)
            m = _dot(w, vb[c * CHUNK:(c + 1) * CHUNK, g * SGU_GROUP:(g + 1) * SGU_GROUP])
            groups.append(m + b_st_ref[:, g:g + 1])
        chunks.append(jnp.concatenate(groups, axis=-1))
    mixed = jnp.concatenate(chunks, axis=0)
    out = _dot((u * mixed).astype(BF16), w_out_ref[...])
    _post_mixer(x, out, ada, ln1g_ref[...], ln1b_ref[...], wr_ref, br_ref, x1_ref, h2_ref, lg_ref)

    @pl.when(t == pl.num_programs(1) - 1)
    def _():
        st_ref[0] = v[tt - CHUNK:, :]


def _full(shape):
    nd = len(shape)
    return pl.BlockSpec(shape, lambda b, t: (0,) * nd)


def _mix_prompt(kernel_fn, name, x, ada_p, weights, state_rows, scratch, seq):
    n = x.shape[0]
    batch = n // seq
    nt = seq // TOK_TILE
    tok = pl.BlockSpec((TOK_TILE, D), lambda b, t: (b * nt + t, 0))
    lgt = pl.BlockSpec((TOK_TILE, LANES), lambda b, t: (b * nt + t, 0))
    return pl.pallas_call(
        kernel_fn,
        out_shape=(jax.ShapeDtypeStruct((n, D), F32), jax.ShapeDtypeStruct((n, D), BF16),
                   jax.ShapeDtypeStruct((n, LANES), F32), jax.ShapeDtypeStruct((batch, state_rows, D), F32)),
        grid=(batch, nt),
        in_specs=[tok, pl.BlockSpec((1, 1, 6 * D), lambda b, t: (b, 0, 0))] + [_full(w.shape) for w in weights],
        out_specs=(tok, tok, lgt, pl.BlockSpec((1, state_rows, D), lambda b, t: (b, 0, 0))),
        scratch_shapes=scratch,
        compiler_params=_cparams(("arbitrary", "arbitrary")),
        name=name,
    )(x, ada_p, *weights)


def _mix_a_sample_kernel(x_ref, ada_ref, hist_ref, w_in_ref, b_in_ref, w_dw_ref, b_dw_ref, lng_ref, lnb_ref,
                         w_out_ref, ln1g_ref, ln1b_ref, wr_ref, br_ref, x1_ref, h2_ref, lg_ref, new_ref):
    x = x_ref[...]
    ada = ada_ref[...]
    sh1, sc1 = ada[:, 0:D], ada[:, D:2 * D]
    h = (x * (1.0 + sc1) + sh1).astype(BF16)
    z = _dot(h, w_in_ref[...]) + b_in_ref[...]
    u = z[:, :D] * jax.nn.sigmoid(z[:, D:])
    y = b_dw_ref[...] + w_dw_ref[CONV_A_WIDTH - 1:CONV_A_WIDTH, :] * u
    for k in range(CONV_A_WIDTH - 1):
        y = y + w_dw_ref[k:k + 1, :] * hist_ref[k]
    y = _ln(y, lng_ref[...], lnb_ref[...])
    y = y * jax.nn.sigmoid(y)
    out = _dot(y.astype(BF16), w_out_ref[...])
    _post_mixer(x, out, ada, ln1g_ref[...], ln1b_ref[...], wr_ref, br_ref, x1_ref, h2_ref, lg_ref)
    new_ref[...] = u


def _mix_b_sample_kernel(x_ref, ada_ref, hist_ref, w_grp_ref, scale_ref, ln1g_ref, ln1b_ref,
                         wr_ref, br_ref, x1_ref, h2_ref, lg_ref, new_ref):
    x = x_ref[...]
    ada = ada_ref[...]
    sh1, sc1 = ada[:, 0:D], ada[:, D:2 * D]
    h = x * (1.0 + sc1) + sh1
    outs = []
    for gi, w in enumerate(POOL_WINDOWS):
        lo = gi * POOL_GROUP
        s = h[:, lo:lo + POOL_GROUP]
        for j in range(1, w):
            s = s + hist_ref[POOL_HIST - j][:, lo:lo + POOL_GROUP]
        pooled = s / float(w) - h[:, lo:lo + POOL_GROUP]
        outs.append(_dot_split(pooled, w_grp_ref[gi].astype(BF16)))
    out = jnp.concatenate(outs, axis=-1) * scale_ref[...]
    _post_mixer(x, out, ada, ln1g_ref[...], ln1b_ref[...], wr_ref, br_ref, x1_ref, h2_ref, lg_ref)
    new_ref[...] = h


def _mix_c_sample_kernel(x_ref, ada_ref, hist_ref, w_in_ref, w_conv_ref, w_out_ref, ln1g_ref, ln1b_ref,
                         wr_ref, br_ref, x1_ref, h2_ref, lg_ref, new_ref):
    x = x_ref[...]
    ada = ada_ref[...]
    sh1, sc1 = ada[:, 0:D], ada[:, D:2 * D]
    h = (x * (1.0 + sc1) + sh1).astype(BF16)
    z = _dot(h, w_in_ref[...])
    bg = z[:, :D]
    v = z[:, D:2 * D] * z[:, 2 * D:]
    y = w_conv_ref[CONV_C_WIDTH - 1:CONV_C_WIDTH, :] * v
    for k in range(CONV_C_WIDTH - 1):
        y = y + w_conv_ref[k:k + 1, :] * hist_ref[k]
    out = _dot((bg * y).astype(BF16), w_out_ref[...])
    _post_mixer(x, out, ada, ln1g_ref[...], ln1b_ref[...], wr_ref, br_ref, x1_ref, h2_ref, lg_ref)
    new_ref[...] = v


def _mix_d_sample_kernel(x_ref, ada_ref, w_in_ref, b_in_ref, lng_ref, lnb_ref, w_s0_ref, b_s0_ref,
                         w_out_ref, ln1g_ref, ln1b_ref, wr_ref, br_ref, x1_ref, h2_ref, lg_ref, new_ref):
    x = x_ref[...]
    ada = ada_ref[...]
    sh1, sc1 = ada[:, 0:D], ada[:, D:2 * D]
    h = (x * (1.0 + sc1) + sh1).astype(BF16)
    z = _gelu_exact(_dot(h, w_in_ref[...]) + b_in_ref[...])
    u = z[:, :D]
    v = _ln(z[:, D:], lng_ref[...], lnb_ref[...])
    mixed = w_s0_ref[...] * v + b_s0_ref[...]
    out = _dot((u * mixed).astype(BF16), w_out_ref[...])
    _post_mixer(x, out, ada, ln1g_ref[...], ln1b_ref[...], wr_ref, br_ref, x1_ref, h2_ref, lg_ref)
    new_ref[...] = v


def _mix_sample(kernel_fn, name, x, ada_s, arrays):
    n = x.shape[0]
    return pl.pallas_call(
        kernel_fn,
        out_shape=(jax.ShapeDtypeStruct((n, D), F32), jax.ShapeDtypeStruct((n, D), BF16),
                   jax.ShapeDtypeStruct((n, LANES), F32), jax.ShapeDtypeStruct((n, D), F32)),
        compiler_params=_cparams(),
        name=name,
    )(x, ada_s, *arrays)


def _route_kernel(hp_ref, hs_ref, lgp_ref, lgs_ref, xs_ref, meta_ref, cnt_ref, *, n_valid):
    i = pl.program_id(0)
    tt = hp_ref.shape[0]
    ns = hs_ref.shape[0]
    cap = xs_ref.shape[0]
    is_sample = i == pl.num_programs(0) - 1
    hs = jnp.concatenate([hs_ref[...], jnp.zeros((tt - ns, D), BF16)], axis=0)
    lgs = jnp.concatenate([lgs_ref[...], jnp.zeros((tt - ns, LANES), F32)], axis=0)
    hb = jnp.where(is_sample, hs, hp_ref[...])
    logits = jnp.where(is_sample, lgs, lgp_ref[...]).T[:N_EXPERTS, :]
    e_iota = lax.broadcasted_iota(jnp.int32, (N_EXPERTS, tt), 0)
    valid = (i * tt + lax.broadcasted_iota(jnp.int32, (1, tt), 1)) < n_valid
    work = logits
    sel, top = [], []
    for _ in range(TOP_K):
        m = jnp.max(work, axis=0, keepdims=True)
        idx = jnp.min(jnp.where(work == m, e_iota, N_EXPERTS), axis=0, keepdims=True)
        oh = e_iota == idx
        sel.append(oh)
        top.append(m)
        work = jnp.where(oh, -jnp.inf, work)
    ex = [jnp.exp(v - top[0]) for v in top]
    denom = ex[0] + ex[1] + ex[2] + ex[3]
    gates = [e / denom for e in ex]

    member = jnp.where(sel[0] | sel[1] | sel[2] | sel[3], 1.0, 0.0)
    member = jnp.where(valid, member, 0.0)
    r_i = lax.broadcasted_iota(jnp.int32, (tt, tt), 0)
    c_i = lax.broadcasted_iota(jnp.int32, (tt, tt), 1)
    before = jnp.where(r_i < c_i, 1.0, 0.0).astype(BF16)
    rank = _dot(member.astype(BF16), before)
    count = jnp.sum(member, axis=1, keepdims=True)
    chunks = jnp.floor((count + float(ROW_CHUNK - 1)) * (1.0 / ROW_CHUNK))
    chunks_b = jnp.broadcast_to(chunks, (N_EXPERTS, LANES))
    er = lax.broadcasted_iota(jnp.int32, (N_EXPERTS, N_EXPERTS), 0)
    ec = lax.broadcasted_iota(jnp.int32, (N_EXPERTS, N_EXPERTS), 1)
    lower = jnp.where(ec < er, 1.0, 0.0).astype(BF16)
    chunk_off = _dot(lower, chunks_b.astype(BF16))
    base = chunk_off[:, 0:1] * float(ROW_CHUNK)
    slot = base + rank
    pos = [jnp.where(valid, jnp.sum(jnp.where(s, slot, 0.0), axis=0, keepdims=True), -1.0) for s in sel]

    row = lax.broadcasted_iota(jnp.int32, (cap, tt), 0).astype(jnp.int16)
    pos16 = [p.astype(jnp.int32).astype(jnp.int16) for p in pos]
    one, zero = jnp.ones((), BF16), jnp.zeros((), BF16)
    onehot = jnp.where(row == pos16[0], one, zero)
    for k in range(1, TOP_K):
        onehot = jnp.where(row == pos16[k], one, onehot)
    xs_ref[...] = _dot(onehot, hb).astype(BF16)

    meta = jnp.concatenate(pos + gates + [jnp.zeros((LANES - 2 * TOP_K, tt), F32)], axis=0)
    meta_ref[...] = meta.T
    cnt_ref[0] = chunks_b.astype(jnp.int32)


def _route(h2_p, h2_s, lg_p, lg_s):
    n_p, n_s = h2_p.shape[0], h2_s.shape[0]
    ntp = n_p // MOE_TILE
    nt = ntp + 1
    cap = _tile_capacity(MOE_TILE)
    prompt_tile = lambda i: (jnp.minimum(i, ntp - 1), 0)
    return pl.pallas_call(
        functools.partial(_route_kernel, n_valid=n_p + n_s),
        out_shape=(jax.ShapeDtypeStruct((nt * cap, D), BF16),
                   jax.ShapeDtypeStruct((nt * MOE_TILE, LANES), F32),
                   jax.ShapeDtypeStruct((nt, N_EXPERTS, LANES), jnp.int32)),
        grid=(nt,),
        in_specs=[pl.BlockSpec((MOE_TILE, D), prompt_tile),
                  pl.BlockSpec((n_s, D), lambda i: (0, 0)),
                  pl.BlockSpec((MOE_TILE, LANES), prompt_tile),
                  pl.BlockSpec((n_s, LANES), lambda i: (0, 0))],
        out_specs=(pl.BlockSpec((cap, D), lambda i: (i, 0)),
                   pl.BlockSpec((MOE_TILE, LANES), lambda i: (i, 0)),
                   pl.BlockSpec((1, N_EXPERTS, LANES), lambda i: (i, 0, 0))),
        compiler_params=_cparams(("arbitrary",)),
        name="route_dispatch",
    )(h2_p, h2_s, lg_p, lg_s)


def _plan_kernel(cnt_ref, be_ref, nx_ref, half_ref, nb_ref, src_ref, dst_ref, z_ref, nz_ref, tile_off, blk0_s,
                 nblk_s, npad_s, *, nt, cap_chunks, nb_max, max_pad, nz_max):
    del nb_max, nz_max
    i32 = jnp.int32
    trash_base = nt * cap_chunks
    shift = BLOCK_CHUNKS.bit_length() - 1

    def index_grid(ref):
        r, c = ref.shape
        return lax.broadcasted_iota(i32, (r, c), 0) * c + lax.broadcasted_iota(i32, (r, c), 1)

    def in_run(idx, start, length):
        return jnp.logical_and(idx >= start, idx < start + length)

    for t in range(nt):
        tile_off[t] = 0

    d = index_grid(src_ref)

    def expert_body(e, carry):
        p0, blk0, pad0, src, dst = carry

        def tile_body(t, inner):
            p, src, dst = inner
            c = cnt_ref[t * N_EXPERTS + e]
            off = tile_off[t]
            tile_off[t] = off + c
            run = in_run(d, p, c)
            val = d + (t * cap_chunks + off - p)
            return p + c, jnp.where(run, val, src), jnp.where(run, val, dst)

        p1, src, dst = lax.fori_loop(0, nt, tile_body, (p0, src, dst))
        nblk = lax.shift_right_logical(p1 - p0 + (BLOCK_CHUNKS - 1), shift)
        p2 = p0 + nblk * BLOCK_CHUNKS
        pad = in_run(d, p1, p2 - p1)
        src = jnp.where(pad, 0, src)
        dst = jnp.where(pad, d + (trash_base + pad0 - p1), dst)
        blk0_s[e] = blk0
        nblk_s[e] = nblk
        npad_s[e] = p2 - p1
        return p2, blk0 + nblk, pad0 + (p2 - p1), src, dst

    zeros = jnp.zeros(src_ref.shape, i32)
    zero = jnp.int32(0)
    _, nb, n_pad, src, dst = lax.fori_loop(0, N_EXPERTS, expert_body, (zero, zero, zero, zeros, zeros))
    src_ref[...] = src
    dst_ref[...] = dst
    nb_ref[0] = nb

    bi = index_grid(be_ref)

    def block_body(k, carry):
        be, nx, half, nxt = carry
        e = N_EXPERTS - 1 - k
        b0, n = blk0_s[e], nblk_s[e]
        mine = in_run(bi, b0, n)
        last = jnp.logical_and(bi == b0 + n - 1, jnp.logical_and(n > 0, npad_s[e] >= HALF_CHUNKS))
        return (jnp.where(mine, e, be), jnp.where(mine, nxt, nx), jnp.where(last, 1, half),
                jnp.where(n > 0, e, nxt))

    init = (jnp.full(be_ref.shape, N_EXPERTS - 1, i32), jnp.full(be_ref.shape, -1, i32),
            jnp.zeros(be_ref.shape, i32), jnp.int32(-1))
    be, nx, half, _ = lax.fori_loop(0, N_EXPERTS, block_body, init)
    be_ref[...] = be
    nx_ref[...] = nx
    half_ref[...] = half

    zi = index_grid(z_ref)

    def tail_body(t, carry):
        z0, z = carry
        used = tile_off[t]
        n = cap_chunks - used
        return z0 + n, jnp.where(in_run(zi, z0, n), zi + (t * cap_chunks + used - z0), z)

    z1, z = lax.fori_loop(0, nt, tail_body, (zero, jnp.zeros(z_ref.shape, i32)))
    n_trash = max_pad - n_pad
    z_ref[...] = jnp.where(in_run(zi, z1, n_trash), zi + (trash_base + n_pad - z1), z)
    nz_ref[0] = z1 + n_trash


def _plan(cnt_flat, nt, cap_chunks, nb_max, max_pad, nz_max):
    smem = pl.BlockSpec(memory_space=pltpu.SMEM)
    vmem = pl.BlockSpec(memory_space=pltpu.VMEM)
    i32 = jnp.int32

    def table(n):
        return jax.ShapeDtypeStruct((-(-n // (8 * LANES)) * 8, LANES), i32)

    out = pl.pallas_call(
        functools.partial(_plan_kernel, nt=nt, cap_chunks=cap_chunks, nb_max=nb_max, max_pad=max_pad, nz_max=nz_max),
        out_shape=(table(nb_max), table(nb_max), table(nb_max), jax.ShapeDtypeStruct((1,), i32),
                   table(nb_max * BLOCK_CHUNKS), table(nb_max * BLOCK_CHUNKS),
                   table(nz_max), jax.ShapeDtypeStruct((1,), i32)),
        in_specs=[smem],
        out_specs=(vmem, vmem, vmem, smem, vmem, vmem, vmem, smem),
        scratch_shapes=[pltpu.SMEM((nt,), i32), pltpu.SMEM((N_EXPERTS,), i32), pltpu.SMEM((N_EXPERTS,), i32),
                        pltpu.SMEM((N_EXPERTS,), i32)],
        name="plan",
    )(cnt_flat)
    return tuple(o.reshape(-1) for o in out)


def _expert_kernel(be_ref, nx_ref, half_ref, nb_ref, src_ref, dst_ref, z_ref, nz_ref, xs_hbm, w_in_hbm, b_in_ref,
                   w_out_hbm, b_out_ref, ys_hbm, xbuf, ybuf, w_in_f32, w_out_f32, w_in_bf, w_out_bf, zbuf,
                   sem_in, sem_out, sem_z, sem_w, *, layer):
    b = pl.program_id(0)
    nb = nb_ref[0]
    nz = nz_ref[0]
    slot = lax.rem(b, 2)

    def weight_copies(e):
        return (pltpu.make_async_copy(w_in_hbm.at[layer, e], w_in_f32, sem_w.at[0]),
                pltpu.make_async_copy(w_out_hbm.at[layer, e], w_out_f32, sem_w.at[1]))

    def ffn(rows):
        e = be_ref[b]
        x = xbuf[slot, 0:rows, :]
        h = _dot(x, w_in_bf[...]) + b_in_ref[layer, e]
        g = jnp.minimum(h[:, :D], SWIGLU_LIMIT)
        u = jnp.clip(h[:, D:], -SWIGLU_LIMIT, SWIGLU_LIMIT)
        a = (u + 1.0) * (g * jax.nn.sigmoid(SWIGLU_ALPHA * g))
        y = _dot(a.astype(BF16), w_out_bf[...]) + b_out_ref[layer, e]
        ybuf[slot, 0:rows, :] = y.astype(BF16)

    def chunk_rows(c):
        return pl.ds(pl.multiple_of(c * ROW_CHUNK, ROW_CHUNK), ROW_CHUNK)

    def gather(blk, s):
        for j in range(BLOCK_CHUNKS):
            c = src_ref[blk * BLOCK_CHUNKS + j]
            pltpu.make_async_copy(xs_hbm.at[chunk_rows(c)], xbuf.at[s, pl.ds(j * ROW_CHUNK, ROW_CHUNK)],
                                  sem_in.at[s]).start()

    def gather_wait(s):
        pltpu.make_async_copy(xs_hbm.at[pl.ds(0, BLOCK_ROWS)], xbuf.at[s], sem_in.at[s]).wait()

    def scatter(blk, s):
        for j in range(BLOCK_CHUNKS):
            c = dst_ref[blk * BLOCK_CHUNKS + j]
            pltpu.make_async_copy(ybuf.at[s, pl.ds(j * ROW_CHUNK, ROW_CHUNK)], ys_hbm.at[chunk_rows(c)],
                                  sem_out.at[s]).start()

    def scatter_wait(s):
        pltpu.make_async_copy(ybuf.at[s], ys_hbm.at[pl.ds(0, BLOCK_ROWS)], sem_out.at[s]).wait()

    @pl.when(b == 0)
    def _():
        for c in weight_copies(be_ref[0]):
            c.start()
        gather(0, 0)
        zbuf[...] = jnp.zeros(zbuf.shape, zbuf.dtype)
        ybuf[...] = jnp.zeros(ybuf.shape, ybuf.dtype)

        def zero_one(i, _):
            pltpu.make_async_copy(zbuf, ys_hbm.at[chunk_rows(z_ref[i])], sem_z).start()
            return 0

        lax.fori_loop(0, nz, zero_one, 0)

    @pl.when(b < nb)
    def _():
        new_expert = jnp.logical_or(b == 0, be_ref[b] != be_ref[jnp.maximum(b - 1, 0)])

        @pl.when(new_expert)
        def _():
            for c in weight_copies(be_ref[b]):
                c.wait()
            w_in_bf[...] = w_in_f32[...].astype(BF16)
            w_out_bf[...] = w_out_f32[...].astype(BF16)

            @pl.when(nx_ref[b] >= 0)
            def _():
                for c in weight_copies(nx_ref[b]):
                    c.start()

        gather_wait(slot)

        @pl.when(b + 1 < nb)
        def _():
            gather(b + 1, 1 - slot)

        @pl.when(b >= 2)
        def _():
            scatter_wait(slot)

        @pl.when(half_ref[b] == 0)
        def _():
            ffn(BLOCK_ROWS)

        @pl.when(half_ref[b] != 0)
        def _():
            ffn(HALF_ROWS)

        scatter(b, slot)

        @pl.when(b == 0)
        def _():
            def zero_wait(i, _):
                pltpu.make_async_copy(zbuf, ys_hbm.at[pl.ds(0, ROW_CHUNK)], sem_z).wait()
                return 0

            lax.fori_loop(0, nz, zero_wait, 0)

        @pl.when(b == nb - 1)
        def _():
            scatter_wait(slot)

            @pl.when(b >= 1)
            def _():
                scatter_wait(1 - slot)


def _experts(xs, plan, layer, w_in, b_in, w_out, b_out, nb_max, ys_rows):
    resident = lambda b, *_: (0, 0, 0, 0)
    grid_spec = pltpu.PrefetchScalarGridSpec(
        num_scalar_prefetch=8,
        grid=(nb_max,),
        in_specs=[pl.BlockSpec(memory_space=pl.ANY),
                  pl.BlockSpec(memory_space=pl.ANY),
                  pl.BlockSpec(b_in.shape, resident),
                  pl.BlockSpec(memory_space=pl.ANY),
                  pl.BlockSpec(b_out.shape, resident)],
        out_specs=pl.BlockSpec(memory_space=pl.ANY),
        scratch_shapes=[pltpu.VMEM((2, BLOCK_ROWS, D), BF16), pltpu.VMEM((2, BLOCK_ROWS, D), BF16),
                        pltpu.VMEM((D, 2 * D), F32), pltpu.VMEM((D, D), F32),
                        pltpu.VMEM((D, 2 * D), BF16), pltpu.VMEM((D, D), BF16), pltpu.VMEM((ROW_CHUNK, D), BF16),
                        pltpu.SemaphoreType.DMA((2,)), pltpu.SemaphoreType.DMA((2,)), pltpu.SemaphoreType.DMA(()),
                        pltpu.SemaphoreType.DMA((2,))],
    )
    return pl.pallas_call(
        functools.partial(_expert_kernel, layer=layer),
        out_shape=jax.ShapeDtypeStruct((ys_rows, D), BF16),
        grid_spec=grid_spec,
        compiler_params=_cparams(("arbitrary",)),
        name="experts",
    )(*plan, xs, w_in, b_in, w_out, b_out)


def _combine_kernel(ys_ref, meta_ref, x1_ref, ada_ref, lng_ref, lnb_ref, o_ref):
    tt = x1_ref.shape[0]
    cap = ys_ref.shape[0]
    meta = meta_ref[...]
    col = lax.broadcasted_iota(jnp.int32, (tt, cap), 1).astype(jnp.int16)
    gmat = jnp.zeros((tt, cap), BF16)
    for k in range(TOP_K):
        pos_k = meta[:, k:k + 1].astype(jnp.int32).astype(jnp.int16)
        gate_k = meta[:, TOP_K + k:TOP_K + k + 1].astype(BF16)
        gmat = jnp.where(col == pos_k, gate_k, gmat)
    f = _dot(gmat, ys_ref[...])
    ada = ada_ref[0] if len(ada_ref.shape) == 3 else ada_ref[...]
    g2 = ada[:, 5 * D:6 * D]
    o_ref[...] = _ln(ALPHA * x1_ref[...] + (1.0 + g2) * f, lng_ref[...], lnb_ref[...])


def _combine_prompt(ys, meta, x1, ada_p, ln_g, ln_b, seq):
    n = x1.shape[0]
    cap = _tile_capacity(MOE_TILE)
    per_seq = seq // MOE_TILE
    return pl.pallas_call(
        _combine_kernel,
        out_shape=jax.ShapeDtypeStruct((n, D), F32),
        grid=(n // MOE_TILE,),
        in_specs=[pl.BlockSpec((cap, D), lambda i: (i, 0)),
                  pl.BlockSpec((MOE_TILE, LANES), lambda i: (i, 0)),
                  pl.BlockSpec((MOE_TILE, D), lambda i: (i, 0)),
                  pl.BlockSpec((1, 1, 6 * D), lambda i: (i // per_seq, 0, 0)),
                  pl.BlockSpec((1, D), lambda i: (0, 0)),
                  pl.BlockSpec((1, D), lambda i: (0, 0))],
        out_specs=pl.BlockSpec((MOE_TILE, D), lambda i: (i, 0)),
        compiler_params=_cparams(("arbitrary",)),
        name="combine_prompt",
    )(ys, meta, x1, ada_p, ln_g, ln_b)


def _combine_sample(ys, meta, x1, ada_s, ln_g, ln_b, tile):
    n = x1.shape[0]
    cap = _tile_capacity(MOE_TILE)
    return pl.pallas_call(
        _combine_kernel,
        out_shape=jax.ShapeDtypeStruct((n, D), F32),
        grid=(1,),
        in_specs=[pl.BlockSpec((cap, D), lambda i: (tile, 0)),
                  pl.BlockSpec((n, LANES), lambda i: (tile * (MOE_TILE // n), 0)),
                  pl.BlockSpec((n, D), lambda i: (0, 0)),
                  pl.BlockSpec((n, 6 * D), lambda i: (0, 0)),
                  pl.BlockSpec((1, D), lambda i: (0, 0)),
                  pl.BlockSpec((1, D), lambda i: (0, 0))],
        out_specs=pl.BlockSpec((n, D), lambda i: (0, 0)),
        compiler_params=_cparams(("arbitrary",)),
        name="combine_sample",
    )(ys, meta, x1, ada_s, ln_g, ln_b)


def _moe_and_norm(h2_p, h2_s, lg_p, lg_s, x1_p, x1_s, ada_p, ada_s, ln_g, ln_b,
                  layer, w_in, b_in, w_out, b_out, seq):
    n_p, n_s = h2_p.shape[0], h2_s.shape[0]
    n_valid = n_p + n_s
    nt = n_p // MOE_TILE + 1
    cap = _tile_capacity(MOE_TILE)
    cap_chunks = cap // ROW_CHUNK
    xs, meta, cnt = _route(h2_p, h2_s, lg_p, lg_s)
    total_chunks_max = (TOP_K * n_valid + nt * N_EXPERTS * (ROW_CHUNK - 1)) // ROW_CHUNK
    nb_max = -(-total_chunks_max // BLOCK_CHUNKS) + N_EXPERTS
    max_pad = N_EXPERTS * (BLOCK_CHUNKS - 1)
    nz_max = nt * cap_chunks - (TOP_K * n_valid) // ROW_CHUNK + max_pad
    plan = _plan(cnt[:, :, 0].reshape(-1), nt, cap_chunks, nb_max, max_pad, nz_max)
    ys = _experts(xs, plan, layer, w_in, b_in, w_out, b_out, nb_max, (nt * cap_chunks + max_pad) * ROW_CHUNK)
    x2_p = _combine_prompt(ys, meta, x1_p, ada_p, ln_g, ln_b, seq)
    x2_s = _combine_sample(ys, meta, x1_s, ada_s, ln_g, ln_b, n_p // MOE_TILE)
    return x2_p, x2_s


def _row(v):
    return v.reshape(1, -1)


def kernel(x_prompt, x_sample, state_conv_a, state_pool_b, state_conv_c, c_prompt, c_sample,
           w_ada, b_ada, ln1_g, ln1_b, ln2_g, ln2_b,
           a_w_in, a_b_in, a_w_dw, a_b_dw, a_ln_g, a_ln_b, a_w_out,
           b_w_grp, b_scale, c_w_in, c_w_conv, c_w_out,
           d_w_in, d_b_in, d_ln_g, d_ln_b, d_w_s, d_b_s, d_w_out,
           w_router, b_router, w_moe_in, b_moe_in, w_moe_out, b_moe_out):
    bp, seq, _ = x_prompt.shape
    bs = x_sample.shape[0]
    assert seq % TOK_TILE == 0 and seq % MOE_TILE == 0 and MOE_TILE % bs == 0
    assert x_sample.shape[1] == 1 and w_ada.shape[0] == DEPTH == 4

    ada = _ada(jnp.concatenate([c_prompt, c_sample], axis=0), w_ada, b_ada)
    xp = x_prompt.reshape(bp * seq, D)
    xs = x_sample.reshape(bs, D)
    b_moe_in4 = b_moe_in.reshape(DEPTH, N_EXPERTS, 1, 2 * D)
    b_moe_out4 = b_moe_out.reshape(DEPTH, N_EXPERTS, 1, D)
    states = {}

    for i in range(DEPTH):
        ada_p = ada[i, :bp].reshape(bp, 1, 6 * D)
        ada_s = ada[i, bp:]
        wr = jnp.pad(w_router[i], ((0, 0), (0, LANES - N_EXPERTS)))
        ln1 = [_row(ln1_g[i]), _row(ln1_b[i]), wr, _row(jnp.pad(b_router[i], (0, LANES - N_EXPERTS)))]
        if i == 0:
            wts = [a_w_in[0].astype(BF16), _row(a_b_in[0]), a_w_dw[0], _row(a_b_dw[0]), _row(a_ln_g[0]),
                   _row(a_ln_b[0]), a_w_out[0].astype(BF16)] + ln1
            x1_p, h2_p, lg_p, st =_mix_prompt(_mix_a_prompt_kernel, "mix_a_prompt", xp, ada_p, wts, A_HALO,
                                         [pltpu.VMEM((A_HALO + TOK_TILE, D), F32),
                                          pltpu.VMEM((8, A_HALO + TOK_TILE, LANES), F32),
                                          pltpu.VMEM((TOK_TILE, D), F32)], seq)
            states["a_p"] = st[:, A_HALO - (CONV_A_WIDTH - 1):][None]
            hist = jnp.transpose(state_conv_a[0], (1, 0, 2))
            x1_s, h2_s, lg_s, new =_mix_sample(_mix_a_sample_kernel, "mix_a_sample", xs, ada_s, [hist] + wts)
            states["a_s"] = jnp.concatenate([state_conv_a[0][:, 1:], new[:, None]], axis=1)[None]
        elif i == 1:
            wts = [b_w_grp[0], _row(b_scale[0])] + ln1
            x1_p, h2_p, lg_p, st =_mix_prompt(_mix_b_prompt_kernel, "mix_b_prompt", xp, ada_p, wts, B_HALO,
                                         [pltpu.VMEM((B_HALO + TOK_TILE, D), F32),
                                          pltpu.VMEM((B_HALO + TOK_TILE, POOL_GROUP), F32),
                                          pltpu.VMEM((B_HALO + TOK_TILE, POOL_GROUP), F32)], seq)
            states["b_p"] = st[:, B_HALO - POOL_HIST:][None]
            hist = jnp.transpose(state_pool_b[0], (1, 0, 2))
            x1_s, h2_s, lg_s, new =_mix_sample(_mix_b_sample_kernel, "mix_b_sample", xs, ada_s, [hist] + wts)
            states["b_s"] = jnp.concatenate([state_pool_b[0][:, 1:], new[:, None]], axis=1)[None]
        elif i == 2:
            wts = [c_w_in[0].astype(BF16), c_w_conv[0], c_w_out[0].astype(BF16)] + ln1
            x1_p, h2_p, lg_p, st =_mix_prompt(_mix_c_prompt_kernel, "mix_c_prompt", xp, ada_p, wts, C_HALO,
                                         [pltpu.VMEM((C_HALO + TOK_TILE, D), F32)], seq)
            states["c_p"] = st[:, C_HALO - (CONV_C_WIDTH - 1):][None]
            hist = jnp.transpose(state_conv_c[0], (1, 0, 2))
            x1_s, h2_s, lg_s, new =_mix_sample(_mix_c_sample_kernel, "mix_c_sample", xs, ada_s, [hist] + wts)
            states["c_s"] = jnp.concatenate([state_conv_c[0][:, 1:], new[:, None]], axis=1)[None]
        else:
            common = [d_w_in[0].astype(BF16), _row(d_b_in[0]), _row(d_ln_g[0]), _row(d_ln_b[0])]
            wts = common + [d_w_s[0], d_b_s[0].T, d_w_out[0].astype(BF16)] + ln1
            x1_p, h2_p, lg_p, st =_mix_prompt(_mix_d_prompt_kernel, "mix_d_prompt", xp, ada_p, wts, CHUNK, [], seq)
            states["d_p"] = st[None]
            w_s0 = jnp.repeat(d_w_s[0][:, 0, 0], SGU_GROUP).reshape(1, D)
            b_s0 = jnp.repeat(d_b_s[0][:, 0], SGU_GROUP).reshape(1, D)
            wts_s = common + [w_s0, b_s0, d_w_out[0].astype(BF16)] + ln1
            x1_s, h2_s, lg_s, new =_mix_sample(_mix_d_sample_kernel, "mix_d_sample", xs, ada_s, wts_s)
            states["d_s"] = new[:, None][None]

        xp, xs = _moe_and_norm(h2_p, h2_s, lg_p, lg_s, x1_p, x1_s, ada_p, ada_s, _row(ln2_g[i]), _row(ln2_b[i]),
                               i, w_moe_in, b_moe_in4, w_moe_out, b_moe_out4, seq)

    return (xp.reshape(bp, seq, D), xs.reshape(bs, 1, D),
            states["a_p"], states["a_s"], states["b_p"], states["b_s"],
            states["c_p"], states["c_s"], states["d_p"], states["d_s"])
```

```python
import functools

import jax
import jax.numpy as jnp
from jax import lax
from jax.experimental import pallas as pl
from jax.experimental.pallas import tpu as pltpu

F32 = jnp.float32
BF16 = jnp.bfloat16

D = 1024
DEPTH = 4
N_EXPERTS = 32
TOP_K = 4
CONV_A_WIDTH = 31
POOL_WINDOWS = (2, 4, 8, 16)
POOL_GROUP = D // 4
POOL_HIST = 15
CONV_C_WIDTH = 3
CHUNK = 128
N_SGU_GROUPS = 4
SGU_GROUP = D // N_SGU_GROUPS
SWIGLU_LIMIT = 7.0
SWIGLU_ALPHA = 1.702
ALPHA = (2 * DEPTH) ** 0.25
LN_EPS = 1e-5

LANES = 128
SUBLANES_BF16 = 16
VMEM_LIMIT = 56 * 1024 * 1024

TOK_TILE = 512
MOE_TILE = 512
ROW_CHUNK = SUBLANES_BF16
BLOCK_CHUNKS = 32
BLOCK_ROWS = BLOCK_CHUNKS * ROW_CHUNK
QUARTER_CHUNKS = BLOCK_CHUNKS // 4
QUARTER_ROWS = BLOCK_ROWS // 4


def _tile_capacity(tt):
    cap = TOP_K * tt + N_EXPERTS * (ROW_CHUNK - 1)
    return -(-cap // 256) * 256


def _cparams(sem=None):
    return pltpu.CompilerParams(dimension_semantics=sem, vmem_limit_bytes=VMEM_LIMIT)


def _dot(a, b):
    return jnp.dot(a, b, preferred_element_type=F32)


def _dot_split(a, b):
    hi = a.astype(BF16)
    lo = (a - hi.astype(F32)).astype(BF16)
    return _dot(hi, b) + _dot(lo, b)


def _ln(x, g, b):
    mu = jnp.mean(x, axis=-1, keepdims=True)
    xc = x - mu
    var = jnp.mean(xc * xc, axis=-1, keepdims=True)
    return xc * lax.rsqrt(var + LN_EPS) * g + b


def _split_ada(ada):
    return [ada[:, i * D:(i + 1) * D] for i in range(6)]


def _post_mixer(x, out, ada, ln_g, ln_b, wr_ref, br_ref, x1_ref, h2_ref, lg_ref):
    _, _, g1, sh2, sc2, _ = _split_ada(ada)
    x1 = _ln(ALPHA * x + (1.0 + g1) * out, ln_g, ln_b)
    x1_ref[...] = x1
    h2 = x1 * (1.0 + sc2) + sh2
    hi = h2.astype(BF16)
    h2_ref[...] = hi
    lo = (h2 - hi.astype(F32)).astype(BF16)
    wr = wr_ref[...]
    w_hi = wr.astype(BF16)
    w_lo = (wr - w_hi.astype(F32)).astype(BF16)
    both = _dot(hi, jnp.concatenate([w_hi, w_lo], axis=1))
    lg_ref[...] = both[:, :LANES] + both[:, LANES:] + _dot(lo, w_hi) + br_ref[...]


def _ada_kernel(c_ref, w_ref, b_ref, o_ref):
    c = c_ref[...]
    sc = (c * jax.nn.sigmoid(c)).astype(BF16)
    o_ref[0] = _dot(sc, w_ref[0].astype(BF16)) + b_ref[0]


def _ada(c_all, w_ada, b_ada):
    rows = c_all.shape[0]
    return pl.pallas_call(
        _ada_kernel,
        out_shape=jax.ShapeDtypeStruct((DEPTH, rows, 6 * D), F32),
        grid=(DEPTH, 6),
        in_specs=[
            pl.BlockSpec((rows, D), lambda i, j: (0, 0)),
            pl.BlockSpec((1, D, D), lambda i, j: (i, 0, j)),
            pl.BlockSpec((1, 1, D), lambda i, j: (i, 0, j)),
        ],
        out_specs=pl.BlockSpec((1, rows, D), lambda i, j: (i, 0, j)),
        compiler_params=_cparams(("arbitrary", "arbitrary")),
        name="ada",
    )(c_all, w_ada, b_ada.reshape(DEPTH, 1, 6 * D))


A_HALO = 32
B_HALO = 32
C_HALO = 8
CONV_ROWS = 64


def _mix_a_prompt_kernel(x_ref, ada_ref, w_in_ref, b_in_ref, w_dw_ref, b_dw_ref, lng_ref, lnb_ref,
                         w_out_ref, ln1g_ref, ln1b_ref, wr_ref, br_ref, x1_ref, h2_ref, lg_ref, st_ref,
                         u_scr, shift_scr, y_scr):
    t = pl.program_id(1)
    tt = x_ref.shape[0]
    x = x_ref[...]
    ada = ada_ref[0]
    sh1, sc1 = ada[:, 0:D], ada[:, D:2 * D]
    h = (x * (1.0 + sc1) + sh1).astype(BF16)
    z = _dot(h, w_in_ref[...]) + b_in_ref[...]
    u = z[:, :D] * jax.nn.sigmoid(z[:, D:])

    @pl.when(t == 0)
    def _():
        u_scr[0:A_HALO, :] = jnp.zeros((A_HALO, D), F32)

    u_scr[A_HALO:A_HALO + tt, :] = u
    first = A_HALO - (CONV_A_WIDTH - 1)
    for lb in range(D // LANES):
        lanes = slice(lb * LANES, (lb + 1) * LANES)
        for s in range(8):
            rows = tt + A_HALO - (0 if s == 0 else 8)
            shift_scr[s, 0:rows, :] = u_scr[s:s + rows, lanes]

        def row_block(rb, carry, lanes=lanes):
            r0 = rb * CONV_ROWS
            acc = jnp.broadcast_to(b_dw_ref[:, lanes], (CONV_ROWS, LANES))
            for k in range(CONV_A_WIDTH):
                q, s = divmod(first + k, 8)
                start = pl.multiple_of(r0 + 8 * q, 8)
                acc = acc + w_dw_ref[k:k + 1, lanes] * shift_scr[s, pl.ds(start, CONV_ROWS), :]
            y_scr[pl.ds(pl.multiple_of(r0, CONV_ROWS), CONV_ROWS), lanes] = acc
            return carry

        lax.fori_loop(0, tt // CONV_ROWS, row_block, 0)
    y = _ln(y_scr[...], lng_ref[...], lnb_ref[...])
    y = y * jax.nn.sigmoid(y)
    out = _dot(y.astype(BF16), w_out_ref[...])
    _post_mixer(x, out, ada, ln1g_ref[...], ln1b_ref[...], wr_ref, br_ref, x1_ref, h2_ref, lg_ref)
    tail = u_scr[tt:tt + A_HALO, :]
    u_scr[0:A_HALO, :] = tail

    @pl.when(t == pl.num_programs(1) - 1)
    def _():
        st_ref[0] = tail


def _mix_b_prompt_kernel(x_ref, ada_ref, w_grp_ref, scale_ref, ln1g_ref, ln1b_ref,
                         wr_ref, br_ref, x1_ref, h2_ref, lg_ref, st_ref, h_scr, pa_scr, pb_scr):
    t = pl.program_id(1)
    tt = x_ref.shape[0]
    x = x_ref[...]
    ada = ada_ref[0]
    sh1, sc1 = ada[:, 0:D], ada[:, D:2 * D]
    h = x * (1.0 + sc1) + sh1

    @pl.when(t == 0)
    def _():
        h_scr[0:B_HALO, :] = jnp.zeros((B_HALO, D), F32)

    h_scr[B_HALO:B_HALO + tt, :] = h
    pos = (t * tt + lax.broadcasted_iota(jnp.int32, (tt, POOL_GROUP), 0)).astype(F32)
    outs = []
    for gi, w in enumerate(POOL_WINDOWS):
        lo = gi * POOL_GROUP
        end = B_HALO + tt
        start, width, level = 8, 2, 0
        cur = h_scr[start:end, lo:lo + POOL_GROUP] + h_scr[start - 1:end - 1, lo:lo + POOL_GROUP]
        while width < w:
            buf = (pa_scr, pb_scr)[level % 2]
            buf[start:end, :] = cur
            cur = buf[start + 8:end, :] + buf[start + 8 - width:end - width, :]
            start, width, level = start + 8, 2 * width, level + 1
        s = cur[B_HALO - start:, :]
        cnt = jnp.minimum(float(w), pos + 1.0)
        pooled = s / cnt - h[:, lo:lo + POOL_GROUP]
        outs.append(_dot_split(pooled, w_grp_ref[gi].astype(BF16)))
    out = jnp.concatenate(outs, axis=-1) * scale_ref[...]
    _post_mixer(x, out, ada, ln1g_ref[...], ln1b_ref[...], wr_ref, br_ref, x1_ref, h2_ref, lg_ref)
    tail = h_scr[tt:tt + B_HALO, :]
    h_scr[0:B_HALO, :] = tail

    @pl.when(t == pl.num_programs(1) - 1)
    def _():
        st_ref[0] = tail


def _mix_c_prompt_kernel(x_ref, ada_ref, w_in_ref, w_conv_ref, w_out_ref, ln1g_ref, ln1b_ref,
                         wr_ref, br_ref, x1_ref, h2_ref, lg_ref, st_ref, v_scr):
    t = pl.program_id(1)
    tt = x_ref.shape[0]
    x = x_ref[...]
    ada = ada_ref[0]
    sh1, sc1 = ada[:, 0:D], ada[:, D:2 * D]
    h = (x * (1.0 + sc1) + sh1).astype(BF16)
    z = _dot(h, w_in_ref[...])
    bg = z[:, :D]
    v = z[:, D:2 * D] * z[:, 2 * D:]

    @pl.when(t == 0)
    def _():
        v_scr[0:C_HALO, :] = jnp.zeros((C_HALO, D), F32)

    v_scr[C_HALO:C_HALO + tt, :] = v
    first = C_HALO - (CONV_C_WIDTH - 1)
    y = w_conv_ref[CONV_C_WIDTH - 1:CONV_C_WIDTH, :] * v
    for k in range(CONV_C_WIDTH - 1):
        y = y + w_conv_ref[k:k + 1, :] * v_scr[first + k:first + k + tt, :]
    out = _dot((bg * y).astype(BF16), w_out_ref[...])
    _post_mixer(x, out, ada, ln1g_ref[...], ln1b_ref[...], wr_ref, br_ref, x1_ref, h2_ref, lg_ref)
    tail = v_scr[tt:tt + C_HALO, :]
    v_scr[0:C_HALO, :] = tail

    @pl.when(t == pl.num_programs(1) - 1)
    def _():
        st_ref[0] = tail


def _gelu_exact(x):
    return 0.5 * x * (1.0 + lax.erf(x * (2.0 ** -0.5)))


def _mix_d_prompt_kernel(x_ref, ada_ref, w_in_ref, b_in_ref, lng_ref, lnb_ref, w_s_ref, b_st_ref,
                         w_out_ref, ln1g_ref, ln1b_ref, wr_ref, br_ref, x1_ref, h2_ref, lg_ref, st_ref):
    t = pl.program_id(1)
    tt = x_ref.shape[0]
    x = x_ref[...]
    ada = ada_ref[0]
    sh1, sc1 = ada[:, 0:D], ada[:, D:2 * D]
    h = (x * (1.0 + sc1) + sh1).astype(BF16)
    z = _gelu_exact(_dot(h, w_in_ref[...]) + b_in_ref[...])
    u = z[:, :D]
    v = _ln(z[:, D:], lng_ref[...], lnb_ref[...])
    vb = v.astype(BF16)
    row = lax.broadcasted_iota(jnp.int32, (CHUNK, CHUNK), 0)
    col = lax.broadcasted_iota(jnp.int32, (CHUNK, CHUNK), 1)
    causal = col <= row
    chunks = []
    for c in range(tt // CHUNK):
        groups = []
        for g in range(N_SGU_GROUPS):
            w = jnp.where(causal, w_s_ref[g], 0.0).astype(BF16)
            m = _dot(w, vb[c * CHUNK:(c + 1) * CHUNK, g * SGU_GROUP:(g + 1) * SGU_GROUP])
            groups.append(m + b_st_ref[:, g:g + 1])
        chunks.append(jnp.concatenate(groups, axis=-1))
    mixed = jnp.concatenate(chunks, axis=0)
    out = _dot((u * mixed).astype(BF16), w_out_ref[...])
    _post_mixer(x, out, ada, ln1g_ref[...], ln1b_ref[...], wr_ref, br_ref, x1_ref, h2_ref, lg_ref)

    @pl.when(t == pl.num_programs(1) - 1)
    def _():
        st_ref[0] = v[tt - CHUNK:, :]


def _full(shape):
    nd = len(shape)
    return pl.BlockSpec(shape, lambda b, t: (0,) * nd)


def _mix_prompt(kernel_fn, name, x, ada_p, weights, state_rows, scratch, seq):
    n = x.shape[0]
    batch = n // seq
    nt = seq // TOK_TILE
    tok = pl.BlockSpec((TOK_TILE, D), lambda b, t: (b * nt + t, 0))
    lgt = pl.BlockSpec((TOK_TILE, LANES), lambda b, t: (b * nt + t, 0))
    return pl.pallas_call(
        kernel_fn,
        out_shape=(jax.ShapeDtypeStruct((n, D), F32), jax.ShapeDtypeStruct((n, D), BF16),
                   jax.ShapeDtypeStruct((n, LANES), F32), jax.ShapeDtypeStruct((batch, state_rows, D), F32)),
        grid=(batch, nt),
        in_specs=[tok, pl.BlockSpec((1, 1, 6 * D), lambda b, t: (b, 0, 0))] + [_full(w.shape) for w in weights],
        out_specs=(tok, tok, lgt, pl.BlockSpec((1, state_rows, D), lambda b, t: (b, 0, 0))),
        scratch_shapes=scratch,
        compiler_params=_cparams(("arbitrary", "arbitrary")),
        name=name,
    )(x, ada_p, *weights)


def _mix_a_sample_kernel(x_ref, ada_ref, hist_ref, w_in_ref, b_in_ref, w_dw_ref, b_dw_ref, lng_ref, lnb_ref,
                         w_out_ref, ln1g_ref, ln1b_ref, wr_ref, br_ref, x1_ref, h2_ref, lg_ref, new_ref):
    x = x_ref[...]
    ada = ada_ref[...]
    sh1, sc1 = ada[:, 0:D], ada[:, D:2 * D]
    h = (x * (1.0 + sc1) + sh1).astype(BF16)
    z = _dot(h, w_in_ref[...]) + b_in_ref[...]
    u = z[:, :D] * jax.nn.sigmoid(z[:, D:])
    y = b_dw_ref[...] + w_dw_ref[CONV_A_WIDTH - 1:CONV_A_WIDTH, :] * u
    for k in range(CONV_A_WIDTH - 1):
        y = y + w_dw_ref[k:k + 1, :] * hist_ref[k]
    y = _ln(y, lng_ref[...], lnb_ref[...])
    y = y * jax.nn.sigmoid(y)
    out = _dot(y.astype(BF16), w_out_ref[...])
    _post_mixer(x, out, ada, ln1g_ref[...], ln1b_ref[...], wr_ref, br_ref, x1_ref, h2_ref, lg_ref)
    new_ref[...] = u


def _mix_b_sample_kernel(x_ref, ada_ref, hist_ref, w_grp_ref, scale_ref, ln1g_ref, ln1b_ref,
                         wr_ref, br_ref, x1_ref, h2_ref, lg_ref, new_ref):
    x = x_ref[...]
    ada = ada_ref[...]
    sh1, sc1 = ada[:, 0:D], ada[:, D:2 * D]
    h = x * (1.0 + sc1) + sh1
    outs = []
    for gi, w in enumerate(POOL_WINDOWS):
        lo = gi * POOL_GROUP
        s = h[:, lo:lo + POOL_GROUP]
        for j in range(1, w):
            s = s + hist_ref[POOL_HIST - j][:, lo:lo + POOL_GROUP]
        pooled = s / float(w) - h[:, lo:lo + POOL_GROUP]
        outs.append(_dot_split(pooled, w_grp_ref[gi].astype(BF16)))
    out = jnp.concatenate(outs, axis=-1) * scale_ref[...]
    _post_mixer(x, out, ada, ln1g_ref[...], ln1b_ref[...], wr_ref, br_ref, x1_ref, h2_ref, lg_ref)
    new_ref[...] = h


def _mix_c_sample_kernel(x_ref, ada_ref, hist_ref, w_in_ref, w_conv_ref, w_out_ref, ln1g_ref, ln1b_ref,
                         wr_ref, br_ref, x1_ref, h2_ref, lg_ref, new_ref):
    x = x_ref[...]
    ada = ada_ref[...]
    sh1, sc1 = ada[:, 0:D], ada[:, D:2 * D]
    h = (x * (1.0 + sc1) + sh1).astype(BF16)
    z = _dot(h, w_in_ref[...])
    bg = z[:, :D]
    v = z[:, D:2 * D] * z[:, 2 * D:]
    y = w_conv_ref[CONV_C_WIDTH - 1:CONV_C_WIDTH, :] * v
    for k in range(CONV_C_WIDTH - 1):
        y = y + w_conv_ref[k:k + 1, :] * hist_ref[k]
    out = _dot((bg * y).astype(BF16), w_out_ref[...])
    _post_mixer(x, out, ada, ln1g_ref[...], ln1b_ref[...], wr_ref, br_ref, x1_ref, h2_ref, lg_ref)
    new_ref[...] = v


def _mix_d_sample_kernel(x_ref, ada_ref, w_in_ref, b_in_ref, lng_ref, lnb_ref, w_s0_ref, b_s0_ref,
                         w_out_ref, ln1g_ref, ln1b_ref, wr_ref, br_ref, x1_ref, h2_ref, lg_ref, new_ref):
    x = x_ref[...]
    ada = ada_ref[...]
    sh1, sc1 = ada[:, 0:D], ada[:, D:2 * D]
    h = (x * (1.0 + sc1) + sh1).astype(BF16)
    z = _gelu_exact(_dot(h, w_in_ref[...]) + b_in_ref[...])
    u = z[:, :D]
    v = _ln(z[:, D:], lng_ref[...], lnb_ref[...])
    mixed = w_s0_ref[...] * v + b_s0_ref[...]
    out = _dot((u * mixed).astype(BF16), w_out_ref[...])
    _post_mixer(x, out, ada, ln1g_ref[...], ln1b_ref[...], wr_ref, br_ref, x1_ref, h2_ref, lg_ref)
    new_ref[...] = v


def _mix_sample(kernel_fn, name, x, ada_s, arrays):
    n = x.shape[0]
    return pl.pallas_call(
        kernel_fn,
        out_shape=(jax.ShapeDtypeStruct((n, D), F32), jax.ShapeDtypeStruct((n, D), BF16),
                   jax.ShapeDtypeStruct((n, LANES), F32), jax.ShapeDtypeStruct((n, D), F32)),
        compiler_params=_cparams(),
        name=name,
    )(x, ada_s, *arrays)


def _route_kernel(hp_ref, hs_ref, lgp_ref, lgs_ref, xs_ref, meta_ref, cnt_ref, *, n_valid):
    i = pl.program_id(0)
    tt = hp_ref.shape[0]
    ns = hs_ref.shape[0]
    cap = xs_ref.shape[0]
    is_sample = i == pl.num_programs(0) - 1
    hs = jnp.concatenate([hs_ref[...], jnp.zeros((tt - ns, D), BF16)], axis=0)
    lgs = jnp.concatenate([lgs_ref[...], jnp.zeros((tt - ns, LANES), F32)], axis=0)
    hb = jnp.where(is_sample, hs, hp_ref[...])
    logits = jnp.where(is_sample, lgs, lgp_ref[...]).T[:N_EXPERTS, :]
    e_iota = lax.broadcasted_iota(jnp.int32, (N_EXPERTS, tt), 0)
    valid = (i * tt + lax.broadcasted_iota(jnp.int32, (1, tt), 1)) < n_valid
    work = logits
    sel, top = [], []
    for _ in range(TOP_K):
        m = jnp.max(work, axis=0, keepdims=True)
        idx = jnp.min(jnp.where(work == m, e_iota, N_EXPERTS), axis=0, keepdims=True)
        oh = e_iota == idx
        sel.append(oh)
        top.append(m)
        work = jnp.where(oh, -jnp.inf, work)
    ex = [jnp.exp(v - top[0]) for v in top]
    denom = ex[0] + ex[1] + ex[2] + ex[3]
    gates = [e / denom for e in ex]

    member = jnp.where(sel[0] | sel[1] | sel[2] | sel[3], 1.0, 0.0)
    member = jnp.where(valid, member, 0.0)
    r_i = lax.broadcasted_iota(jnp.int32, (tt, tt), 0)
    c_i = lax.broadcasted_iota(jnp.int32, (tt, tt), 1)
    before = jnp.where(r_i < c_i, 1.0, 0.0).astype(BF16)
    rank = _dot(member.astype(BF16), before)
    count = jnp.sum(member, axis=1, keepdims=True)
    chunks = jnp.floor((count + float(ROW_CHUNK - 1)) * (1.0 / ROW_CHUNK))
    chunks_b = jnp.broadcast_to(chunks, (N_EXPERTS, LANES))
    er = lax.broadcasted_iota(jnp.int32, (N_EXPERTS, N_EXPERTS), 0)
    ec = lax.broadcasted_iota(jnp.int32, (N_EXPERTS, N_EXPERTS), 1)
    lower = jnp.where(ec < er, 1.0, 0.0).astype(BF16)
    chunk_off = _dot(lower, chunks_b.astype(BF16))
    base = chunk_off[:, 0:1] * float(ROW_CHUNK)
    slot = base + rank
    pos = [jnp.where(valid, jnp.sum(jnp.where(s, slot, 0.0), axis=0, keepdims=True), -1.0) for s in sel]

    row = lax.broadcasted_iota(jnp.int32, (cap, tt), 0).astype(jnp.int16)
    pos16 = [p.astype(jnp.int32).astype(jnp.int16) for p in pos]
    one, zero = jnp.ones((), BF16), jnp.zeros((), BF16)
    onehot = jnp.where(row == pos16[0], one, zero)
    for k in range(1, TOP_K):
        onehot = jnp.where(row == pos16[k], one, onehot)
    xs_ref[...] = _dot(onehot, hb).astype(BF16)

    meta = jnp.concatenate(pos + gates + [jnp.zeros((LANES - 2 * TOP_K, tt), F32)], axis=0)
    meta_ref[...] = meta.T
    cnt_ref[0] = chunks_b.astype(jnp.int32)


def _route(h2_p, h2_s, lg_p, lg_s):
    n_p, n_s = h2_p.shape[0], h2_s.shape[0]
    ntp = n_p // MOE_TILE
    nt = ntp + 1
    cap = _tile_capacity(MOE_TILE)
    prompt_tile = lambda i: (jnp.minimum(i, ntp - 1), 0)
    return pl.pallas_call(
        functools.partial(_route_kernel, n_valid=n_p + n_s),
        out_shape=(jax.ShapeDtypeStruct((nt * cap, D), BF16),
                   jax.ShapeDtypeStruct((nt * MOE_TILE, LANES), F32),
                   jax.ShapeDtypeStruct((nt, N_EXPERTS, LANES), jnp.int32)),
        grid=(nt,),
        in_specs=[pl.BlockSpec((MOE_TILE, D), prompt_tile),
                  pl.BlockSpec((n_s, D), lambda i: (0, 0)),
                  pl.BlockSpec((MOE_TILE, LANES), prompt_tile),
                  pl.BlockSpec((n_s, LANES), lambda i: (0, 0))],
        out_specs=(pl.BlockSpec((cap, D), lambda i: (i, 0)),
                   pl.BlockSpec((MOE_TILE, LANES), lambda i: (i, 0)),
                   pl.BlockSpec((1, N_EXPERTS, LANES), lambda i: (i, 0, 0))),
        compiler_params=_cparams(("arbitrary",)),
        name="route_dispatch",
    )(h2_p, h2_s, lg_p, lg_s)


def _plan_kernel(cnt_ref, be_ref, nx_ref, half_ref, nb_ref, src_ref, dst_ref, z_ref, nz_ref, tile_off, blk0_s,
                 nblk_s, npad_s, *, nt, cap_chunks, nb_max, max_pad, nz_max):
    del nb_max, nz_max
    i32 = jnp.int32
    trash_base = nt * cap_chunks
    shift = BLOCK_CHUNKS.bit_length() - 1

    def index_grid(ref):
        r, c = ref.shape
        return lax.broadcasted_iota(i32, (r, c), 0) * c + lax.broadcasted_iota(i32, (r, c), 1)

    def in_run(idx, start, length):
        return jnp.logical_and(idx >= start, idx < start + length)

    for t in range(nt):
        tile_off[t] = 0

    d = index_grid(src_ref)

    def expert_body(e, carry):
        p0, blk0, pad0, src, dst = carry

        def tile_body(t, inner):
            p, src, dst = inner
            c = cnt_ref[t * N_EXPERTS + e]
            off = tile_off[t]
            tile_off[t] = off + c
            run = in_run(d, p, c)
            val = d + (t * cap_chunks + off - p)
            return p + c, jnp.where(run, val, src), jnp.where(run, val, dst)

        p1, src, dst = lax.fori_loop(0, nt, tile_body, (p0, src, dst))
        nblk = lax.shift_right_logical(p1 - p0 + (BLOCK_CHUNKS - 1), shift)
        p2 = p0 + nblk * BLOCK_CHUNKS
        pad = in_run(d, p1, p2 - p1)
        src = jnp.where(pad, 0, src)
        dst = jnp.where(pad, d + (trash_base + pad0 - p1), dst)
        blk0_s[e] = blk0
        nblk_s[e] = nblk
        npad_s[e] = p2 - p1
        return p2, blk0 + nblk, pad0 + (p2 - p1), src, dst

    zeros = jnp.zeros(src_ref.shape, i32)
    zero = jnp.int32(0)
    _, nb, n_pad, src, dst = lax.fori_loop(0, N_EXPERTS, expert_body, (zero, zero, zero, zeros, zeros))
    src_ref[...] = src
    dst_ref[...] = dst
    nb_ref[0] = nb

    bi = index_grid(be_ref)

    def block_body(k, carry):
        be, nx, half, nxt = carry
        e = N_EXPERTS - 1 - k
        b0, n = blk0_s[e], nblk_s[e]
        mine = in_run(bi, b0, n)
        last = jnp.logical_and(bi == b0 + n - 1, n > 0)
        empty_quarters = lax.shift_right_logical(npad_s[e], QUARTER_CHUNKS.bit_length() - 1)
        return (jnp.where(mine, e, be), jnp.where(mine, nxt, nx), jnp.where(last, empty_quarters, half),
                jnp.where(n > 0, e, nxt))

    init = (jnp.full(be_ref.shape, N_EXPERTS - 1, i32), jnp.full(be_ref.shape, -1, i32),
            jnp.zeros(be_ref.shape, i32), jnp.int32(-1))
    be, nx, half, _ = lax.fori_loop(0, N_EXPERTS, block_body, init)
    be_ref[...] = be
    nx_ref[...] = nx
    half_ref[...] = half

    zi = index_grid(z_ref)

    def tail_body(t, carry):
        z0, z = carry
        used = tile_off[t]
        n = cap_chunks - used
        return z0 + n, jnp.where(in_run(zi, z0, n), zi + (t * cap_chunks + used - z0), z)

    z1, z = lax.fori_loop(0, nt, tail_body, (zero, jnp.zeros(z_ref.shape, i32)))
    n_trash = max_pad - n_pad
    z_ref[...] = jnp.where(in_run(zi, z1, n_trash), zi + (trash_base + n_pad - z1), z)
    nz_ref[0] = z1 + n_trash


def _plan(cnt_flat, nt, cap_chunks, nb_max, max_pad, nz_max):
    smem = pl.BlockSpec(memory_space=pltpu.SMEM)
    vmem = pl.BlockSpec(memory_space=pltpu.VMEM)
    i32 = jnp.int32

    def table(n):
        return jax.ShapeDtypeStruct((-(-n // (8 * LANES)) * 8, LANES), i32)

    out = pl.pallas_call(
        functools.partial(_plan_kernel, nt=nt, cap_chunks=cap_chunks, nb_max=nb_max, max_pad=max_pad, nz_max=nz_max),
        out_shape=(table(nb_max), table(nb_max), table(nb_max), jax.ShapeDtypeStruct((1,), i32),
                   table(nb_max * BLOCK_CHUNKS), table(nb_max * BLOCK_CHUNKS),
                   table(nz_max), jax.ShapeDtypeStruct((1,), i32)),
        in_specs=[smem],
        out_specs=(vmem, vmem, vmem, smem, vmem, vmem, vmem, smem),
        scratch_shapes=[pltpu.SMEM((nt,), i32), pltpu.SMEM((N_EXPERTS,), i32), pltpu.SMEM((N_EXPERTS,), i32),
                        pltpu.SMEM((N_EXPERTS,), i32)],
        name="plan",
    )(cnt_flat)
    return tuple(o.reshape(-1) for o in out)


def _expert_kernel(be_ref, nx_ref, half_ref, nb_ref, src_ref, dst_ref, z_ref, nz_ref, xs_hbm, w_in_hbm, b_in_ref,
                   w_out_hbm, b_out_ref, ys_hbm, xbuf, ybuf, w_in_f32, w_out_f32, w_in_bf, w_out_bf, zbuf,
                   sem_in, sem_out, sem_z, sem_w, *, layer):
    b = pl.program_id(0)
    nb = nb_ref[0]
    nz = nz_ref[0]
    slot = lax.rem(b, 2)

    def weight_copies(e):
        return (pltpu.make_async_copy(w_in_hbm.at[layer, e], w_in_f32, sem_w.at[0]),
                pltpu.make_async_copy(w_out_hbm.at[layer, e], w_out_f32, sem_w.at[1]))

    def ffn(rows):
        e = be_ref[b]
        x = xbuf[slot, 0:rows, :]
        h = _dot(x, w_in_bf[...]) + b_in_ref[layer, e]
        g = jnp.minimum(h[:, :D], SWIGLU_LIMIT)
        u = jnp.clip(h[:, D:], -SWIGLU_LIMIT, SWIGLU_LIMIT)
        a = (u + 1.0) * (g * jax.nn.sigmoid(SWIGLU_ALPHA * g))
        y = _dot(a.astype(BF16), w_out_bf[...]) + b_out_ref[layer, e]
        ybuf[slot, 0:rows, :] = y.astype(BF16)

    def chunk_rows(c):
        return pl.ds(pl.multiple_of(c * ROW_CHUNK, ROW_CHUNK), ROW_CHUNK)

    def gather(blk, s):
        for j in range(BLOCK_CHUNKS):
            c = src_ref[blk * BLOCK_CHUNKS + j]
            pltpu.make_async_copy(xs_hbm.at[chunk_rows(c)], xbuf.at[s, pl.ds(j * ROW_CHUNK, ROW_CHUNK)],
                                  sem_in.at[s]).start()

    def gather_wait(s):
        pltpu.make_async_copy(xs_hbm.at[pl.ds(0, BLOCK_ROWS)], xbuf.at[s], sem_in.at[s]).wait()

    def scatter(blk, s):
        for j in range(BLOCK_CHUNKS):
            c = dst_ref[blk * BLOCK_CHUNKS + j]
            pltpu.make_async_copy(ybuf.at[s, pl.ds(j * ROW_CHUNK, ROW_CHUNK)], ys_hbm.at[chunk_rows(c)],
                                  sem_out.at[s]).start()

    def scatter_wait(s):
        pltpu.make_async_copy(ybuf.at[s], ys_hbm.at[pl.ds(0, BLOCK_ROWS)], sem_out.at[s]).wait()

    @pl.when(b == 0)
    def _():
        for c in weight_copies(be_ref[0]):
            c.start()
        gather(0, 0)
        zbuf[...] = jnp.zeros(zbuf.shape, zbuf.dtype)
        ybuf[...] = jnp.zeros(ybuf.shape, ybuf.dtype)

        def zero_one(i, _):
            pltpu.make_async_copy(zbuf, ys_hbm.at[chunk_rows(z_ref[i])], sem_z).start()
            return 0

        lax.fori_loop(0, nz, zero_one, 0)

    @pl.when(b < nb)
    def _():
        new_expert = jnp.logical_or(b == 0, be_ref[b] != be_ref[jnp.maximum(b - 1, 0)])

        @pl.when(new_expert)
        def _():
            for c in weight_copies(be_ref[b]):
                c.wait()
            w_in_bf[...] = w_in_f32[...].astype(BF16)
            w_out_bf[...] = w_out_f32[...].astype(BF16)

            @pl.when(nx_ref[b] >= 0)
            def _():
                for c in weight_copies(nx_ref[b]):
                    c.start()

        gather_wait(slot)

        @pl.when(b + 1 < nb)
        def _():
            gather(b + 1, 1 - slot)

        @pl.when(b >= 2)
        def _():
            scatter_wait(slot)

        for empty in range(4):
            @pl.when(half_ref[b] == empty)
            def _(empty=empty):
                ffn(BLOCK_ROWS - empty * QUARTER_ROWS)

        scatter(b, slot)

        @pl.when(b == 0)
        def _():
            def zero_wait(i, _):
                pltpu.make_async_copy(zbuf, ys_hbm.at[pl.ds(0, ROW_CHUNK)], sem_z).wait()
                return 0

            lax.fori_loop(0, nz, zero_wait, 0)

        @pl.when(b == nb - 1)
        def _():
            scatter_wait(slot)

            @pl.when(b >= 1)
            def _():
                scatter_wait(1 - slot)


def _experts(xs, plan, layer, w_in, b_in, w_out, b_out, nb_max, ys_rows):
    resident = lambda b, *_: (0, 0, 0, 0)
    grid_spec = pltpu.PrefetchScalarGridSpec(
        num_scalar_prefetch=8,
        grid=(nb_max,),
        in_specs=[pl.BlockSpec(memory_space=pl.ANY),
                  pl.BlockSpec(memory_space=pl.ANY),
                  pl.BlockSpec(b_in.shape, resident),
                  pl.BlockSpec(memory_space=pl.ANY),
                  pl.BlockSpec(b_out.shape, resident)],
        out_specs=pl.BlockSpec(memory_space=pl.ANY),
        scratch_shapes=[pltpu.VMEM((2, BLOCK_ROWS, D), BF16), pltpu.VMEM((2, BLOCK_ROWS, D), BF16),
                        pltpu.VMEM((D, 2 * D), F32), pltpu.VMEM((D, D), F32),
                        pltpu.VMEM((D, 2 * D), BF16), pltpu.VMEM((D, D), BF16), pltpu.VMEM((ROW_CHUNK, D), BF16),
                        pltpu.SemaphoreType.DMA((2,)), pltpu.SemaphoreType.DMA((2,)), pltpu.SemaphoreType.DMA(()),
                        pltpu.SemaphoreType.DMA((2,))],
    )
    return pl.pallas_call(
        functools.partial(_expert_kernel, layer=layer),
        out_shape=jax.ShapeDtypeStruct((ys_rows, D), BF16),
        grid_spec=grid_spec,
        compiler_params=_cparams(("arbitrary",)),
        name="experts",
    )(*plan, xs, w_in, b_in, w_out, b_out)


def _combine_kernel(ys_ref, meta_ref, x1_ref, ada_ref, lng_ref, lnb_ref, o_ref):
    tt = x1_ref.shape[0]
    cap = ys_ref.shape[0]
    meta = meta_ref[...]
    col = lax.broadcasted_iota(jnp.int32, (tt, cap), 1).astype(jnp.int16)
    gmat = jnp.zeros((tt, cap), BF16)
    for k in range(TOP_K):
        pos_k = meta[:, k:k + 1].astype(jnp.int32).astype(jnp.int16)
        gate_k = meta[:, TOP_K + k:TOP_K + k + 1].astype(BF16)
        gmat = jnp.where(col == pos_k, gate_k, gmat)
    f = _dot(gmat, ys_ref[...])
    ada = ada_ref[0] if len(ada_ref.shape) == 3 else ada_ref[...]
    g2 = ada[:, 5 * D:6 * D]
    o_ref[...] = _ln(ALPHA * x1_ref[...] + (1.0 + g2) * f, lng_ref[...], lnb_ref[...])


def _combine_prompt(ys, meta, x1, ada_p, ln_g, ln_b, seq):
    n = x1.shape[0]
    cap = _tile_capacity(MOE_TILE)
    per_seq = seq // MOE_TILE
    return pl.pallas_call(
        _combine_kernel,
        out_shape=jax.ShapeDtypeStruct((n, D), F32),
        grid=(n // MOE_TILE,),
        in_specs=[pl.BlockSpec((cap, D), lambda i: (i, 0)),
                  pl.BlockSpec((MOE_TILE, LANES), lambda i: (i, 0)),
                  pl.BlockSpec((MOE_TILE, D), lambda i: (i, 0)),
                  pl.BlockSpec((1, 1, 6 * D), lambda i: (i // per_seq, 0, 0)),
                  pl.BlockSpec((1, D), lambda i: (0, 0)),
                  pl.BlockSpec((1, D), lambda i: (0, 0))],
        out_specs=pl.BlockSpec((MOE_TILE, D), lambda i: (i, 0)),
        compiler_params=_cparams(("arbitrary",)),
        name="combine_prompt",
    )(ys, meta, x1, ada_p, ln_g, ln_b)


def _combine_sample(ys, meta, x1, ada_s, ln_g, ln_b, tile):
    n = x1.shape[0]
    cap = _tile_capacity(MOE_TILE)
    return pl.pallas_call(
        _combine_kernel,
        out_shape=jax.ShapeDtypeStruct((n, D), F32),
        grid=(1,),
        in_specs=[pl.BlockSpec((cap, D), lambda i: (tile, 0)),
                  pl.BlockSpec((n, LANES), lambda i: (tile * (MOE_TILE // n), 0)),
                  pl.BlockSpec((n, D), lambda i: (0, 0)),
                  pl.BlockSpec((n, 6 * D), lambda i: (0, 0)),
                  pl.BlockSpec((1, D), lambda i: (0, 0)),
                  pl.BlockSpec((1, D), lambda i: (0, 0))],
        out_specs=pl.BlockSpec((n, D), lambda i: (0, 0)),
        compiler_params=_cparams(("arbitrary",)),
        name="combine_sample",
    )(ys, meta, x1, ada_s, ln_g, ln_b)


def _moe_and_norm(h2_p, h2_s, lg_p, lg_s, x1_p, x1_s, ada_p, ada_s, ln_g, ln_b,
                  layer, w_in, b_in, w_out, b_out, seq):
    n_p, n_s = h2_p.shape[0], h2_s.shape[0]
    n_valid = n_p + n_s
    nt = n_p // MOE_TILE + 1
    cap = _tile_capacity(MOE_TILE)
    cap_chunks = cap // ROW_CHUNK
    xs, meta, cnt = _route(h2_p, h2_s, lg_p, lg_s)
    total_chunks_max = (TOP_K * n_valid + nt * N_EXPERTS * (ROW_CHUNK - 1)) // ROW_CHUNK
    nb_max = -(-total_chunks_max // BLOCK_CHUNKS) + N_EXPERTS
    max_pad = N_EXPERTS * (BLOCK_CHUNKS - 1)
    nz_max = nt * cap_chunks - (TOP_K * n_valid) // ROW_CHUNK + max_pad
    plan = _plan(cnt[:, :, 0].reshape(-1), nt, cap_chunks, nb_max, max_pad, nz_max)
    ys = _experts(xs, plan, layer, w_in, b_in, w_out, b_out, nb_max, (nt * cap_chunks + max_pad) * ROW_CHUNK)
    x2_p = _combine_prompt(ys, meta, x1_p, ada_p, ln_g, ln_b, seq)
    x2_s = _combine_sample(ys, meta, x1_s, ada_s, ln_g, ln_b, n_p // MOE_TILE)
    return x2_p, x2_s


def _row(v):
    return v.reshape(1, -1)


def kernel(x_prompt, x_sample, state_conv_a, state_pool_b, state_conv_c, c_prompt, c_sample,
           w_ada, b_ada, ln1_g, ln1_b, ln2_g, ln2_b,
           a_w_in, a_b_in, a_w_dw, a_b_dw, a_ln_g, a_ln_b, a_w_out,
           b_w_grp, b_scale, c_w_in, c_w_conv, c_w_out,
           d_w_in, d_b_in, d_ln_g, d_ln_b, d_w_s, d_b_s, d_w_out,
           w_router, b_router, w_moe_in, b_moe_in, w_moe_out, b_moe_out):
    bp, seq, _ = x_prompt.shape
    bs = x_sample.shape[0]
    assert seq % TOK_TILE == 0 and seq % MOE_TILE == 0 and MOE_TILE % bs == 0
    assert x_sample.shape[1] == 1 and w_ada.shape[0] == DEPTH == 4

    ada = _ada(jnp.concatenate([c_prompt, c_sample], axis=0), w_ada, b_ada)
    xp = x_prompt.reshape(bp * seq, D)
    xs = x_sample.reshape(bs, D)
    b_moe_in4 = b_moe_in.reshape(DEPTH, N_EXPERTS, 1, 2 * D)
    b_moe_out4 = b_moe_out.reshape(DEPTH, N_EXPERTS, 1, D)
    states = {}

    for i in range(DEPTH):
        ada_p = ada[i, :bp].reshape(bp, 1, 6 * D)
        ada_s = ada[i, bp:]
        wr = jnp.pad(w_router[i], ((0, 0), (0, LANES - N_EXPERTS)))
        ln1 = [_row(ln1_g[i]), _row(ln1_b[i]), wr, _row(jnp.pad(b_router[i], (0, LANES - N_EXPERTS)))]
        if i == 0:
            wts = [a_w_in[0].astype(BF16), _row(a_b_in[0]), a_w_dw[0], _row(a_b_dw[0]), _row(a_ln_g[0]),
                   _row(a_ln_b[0]), a_w_out[0].astype(BF16)] + ln1
            x1_p, h2_p, lg_p, st =_mix_prompt(_mix_a_prompt_kernel, "mix_a_prompt", xp, ada_p, wts, A_HALO,
                                         [pltpu.VMEM((A_HALO + TOK_TILE, D), F32),
                                          pltpu.VMEM((8, A_HALO + TOK_TILE, LANES), F32),
                                          pltpu.VMEM((TOK_TILE, D), F32)], seq)
            states["a_p"] = st[:, A_HALO - (CONV_A_WIDTH - 1):][None]
            hist = jnp.transpose(state_conv_a[0], (1, 0, 2))
            x1_s, h2_s, lg_s, new =_mix_sample(_mix_a_sample_kernel, "mix_a_sample", xs, ada_s, [hist] + wts)
            states["a_s"] = jnp.concatenate([state_conv_a[0][:, 1:], new[:, None]], axis=1)[None]
        elif i == 1:
            wts = [b_w_grp[0], _row(b_scale[0])] + ln1
            x1_p, h2_p, lg_p, st =_mix_prompt(_mix_b_prompt_kernel, "mix_b_prompt", xp, ada_p, wts, B_HALO,
                                         [pltpu.VMEM((B_HALO + TOK_TILE, D), F32),
                                          pltpu.VMEM((B_HALO + TOK_TILE, POOL_GROUP), F32),
                                          pltpu.VMEM((B_HALO + TOK_TILE, POOL_GROUP), F32)], seq)
            states["b_p"] = st[:, B_HALO - POOL_HIST:][None]
            hist = jnp.transpose(state_pool_b[0], (1, 0, 2))
            x1_s, h2_s, lg_s, new =_mix_sample(_mix_b_sample_kernel, "mix_b_sample", xs, ada_s, [hist] + wts)
            states["b_s"] = jnp.concatenate([state_pool_b[0][:, 1:], new[:, None]], axis=1)[None]
        elif i == 2:
            wts = [c_w_in[0].astype(BF16), c_w_conv[0], c_w_out[0].astype(BF16)] + ln1
            x1_p, h2_p, lg_p, st =_mix_prompt(_mix_c_prompt_kernel, "mix_c_prompt", xp, ada_p, wts, C_HALO,
                                         [pltpu.VMEM((C_HALO + TOK_TILE, D), F32)], seq)
            states["c_p"] = st[:, C_HALO - (CONV_C_WIDTH - 1):][None]
            hist = jnp.transpose(state_conv_c[0], (1, 0, 2))
            x1_s, h2_s, lg_s, new =_mix_sample(_mix_c_sample_kernel, "mix_c_sample", xs, ada_s, [hist] + wts)
            states["c_s"] = jnp.concatenate([state_conv_c[0][:, 1:], new[:, None]], axis=1)[None]
        else:
            common = [d_w_in[0].astype(BF16), _row(d_b_in[0]), _row(d_ln_g[0]), _row(d_ln_b[0])]
            wts = common + [d_w_s[0], d_b_s[0].T, d_w_out[0].astype(BF16)] + ln1
            x1_p, h2_p, lg_p, st =_mix_prompt(_mix_d_prompt_kernel, "mix_d_prompt", xp, ada_p, wts, CHUNK, [], seq)
            states["d_p"] = st[None]
            w_s0 = jnp.repeat(d_w_s[0][:, 0, 0], SGU_GROUP).reshape(1, D)
            b_s0 = jnp.repeat(d_b_s[0][:, 0], SGU_GROUP).reshape(1, D)
            wts_s = common + [w_s0, b_s0, d_w_out[0].astype(BF16)] + ln1
            x1_s, h2_s, lg_s, new =_mix_sample(_mix_d_sample_kernel, "mix_d_sample", xs, ada_s, wts_s)
            states["d_s"] = new[:, None][None]

        xp, xs = _moe_and_norm(h2_p, h2_s, lg_p, lg_s, x1_p, x1_s, ada_p, ada_s, _row(ln2_g[i]), _row(ln2_b[i]),
                               i, w_moe_in, b_moe_in4, w_moe_out, b_moe_out4, seq)

    return (xp.reshape(bp, seq, D), xs.reshape(bs, 1, D),
            states["a_p"], states["a_s"], states["b_p"], states["b_s"],
            states["c_p"], states["c_s"], states["d_p"], states["d_s"])
```

```python
import functools

import jax
import jax.numpy as jnp
from jax import lax
from jax.experimental import pallas as pl
from jax.experimental.pallas import tpu as pltpu

F32 = jnp.float32
BF16 = jnp.bfloat16

D = 1024
DEPTH = 4
N_EXPERTS = 32
TOP_K = 4
CONV_A_WIDTH = 31
POOL_WINDOWS = (2, 4, 8, 16)
POOL_GROUP = D // 4
POOL_HIST = 15
CONV_C_WIDTH = 3
CHUNK = 128
N_SGU_GROUPS = 4
SGU_GROUP = D // N_SGU_GROUPS
SWIGLU_LIMIT = 7.0
SWIGLU_ALPHA = 1.702
ALPHA = (2 * DEPTH) ** 0.25
LN_EPS = 1e-5

LANES = 128
SUBLANES_BF16 = 16
VMEM_LIMIT = 56 * 1024 * 1024

TOK_TILE = 512
MOE_TILE = 512
ROW_CHUNK = SUBLANES_BF16
BLOCK_CHUNKS = 32
BLOCK_ROWS = BLOCK_CHUNKS * ROW_CHUNK
QUARTER_CHUNKS = BLOCK_CHUNKS // 4
QUARTER_ROWS = BLOCK_ROWS // 4


def _tile_capacity(tt):
    cap = TOP_K * tt + N_EXPERTS * (ROW_CHUNK - 1)
    return -(-cap // 256) * 256


def _cparams(sem=None):
    return pltpu.CompilerParams(dimension_semantics=sem, vmem_limit_bytes=VMEM_LIMIT)


def _dot(a, b):
    return jnp.dot(a, b, preferred_element_type=F32)


def _dot_split(a, b):
    hi = a.astype(BF16)
    lo = (a - hi.astype(F32)).astype(BF16)
    return _dot(hi, b) + _dot(lo, b)


def _ln(x, g, b):
    mu = jnp.mean(x, axis=-1, keepdims=True)
    xc = x - mu
    var = jnp.mean(xc * xc, axis=-1, keepdims=True)
    return xc * lax.rsqrt(var + LN_EPS) * g + b


def _split_ada(ada):
    return [ada[:, i * D:(i + 1) * D] for i in range(6)]


def _post_mixer(x, out, ada, ln_g, ln_b, wr_ref, br_ref, x1_ref, h2_ref, lg_ref):
    _, _, g1, sh2, sc2, _ = _split_ada(ada)
    x1 = _ln(ALPHA * x + (1.0 + g1) * out, ln_g, ln_b)
    x1_ref[...] = x1
    h2 = x1 * (1.0 + sc2) + sh2
    hi = h2.astype(BF16)
    h2_ref[...] = hi
    lo = (h2 - hi.astype(F32)).astype(BF16)
    wr = wr_ref[...]
    w_hi = wr.astype(BF16)
    w_lo = (wr - w_hi.astype(F32)).astype(BF16)
    both = _dot(hi, jnp.concatenate([w_hi, w_lo], axis=1))
    lg_ref[...] = both[:, :LANES] + both[:, LANES:] + _dot(lo, w_hi) + br_ref[...]


def _ada_kernel(c_ref, w_ref, b_ref, o_ref):
    c = c_ref[...]
    sc = (c * jax.nn.sigmoid(c)).astype(BF16)
    o_ref[0] = _dot(sc, w_ref[0].astype(BF16)) + b_ref[0]


def _ada(c_all, w_ada, b_ada):
    rows = c_all.shape[0]
    return pl.pallas_call(
        _ada_kernel,
        out_shape=jax.ShapeDtypeStruct((DEPTH, rows, 6 * D), F32),
        grid=(DEPTH, 6),
        in_specs=[
            pl.BlockSpec((rows, D), lambda i, j: (0, 0)),
            pl.BlockSpec((1, D, D), lambda i, j: (i, 0, j)),
            pl.BlockSpec((1, 1, D), lambda i, j: (i, 0, j)),
        ],
        out_specs=pl.BlockSpec((1, rows, D), lambda i, j: (i, 0, j)),
        compiler_params=_cparams(("arbitrary", "arbitrary")),
        name="ada",
    )(c_all, w_ada, b_ada.reshape(DEPTH, 1, 6 * D))


A_HALO = 32
B_HALO = 32
C_HALO = 8
CONV_ROWS = 128


def _mix_a_prompt_kernel(x_ref, ada_ref, w_in_ref, b_in_ref, w_dw_ref, b_dw_ref, lng_ref, lnb_ref,
                         w_out_ref, ln1g_ref, ln1b_ref, wr_ref, br_ref, x1_ref, h2_ref, lg_ref, st_ref,
                         u_scr, shift_scr, y_scr):
    t = pl.program_id(1)
    tt = x_ref.shape[0]
    x = x_ref[...]
    ada = ada_ref[0]
    sh1, sc1 = ada[:, 0:D], ada[:, D:2 * D]
    h = (x * (1.0 + sc1) + sh1).astype(BF16)
    z = _dot(h, w_in_ref[...]) + b_in_ref[...]
    u = z[:, :D] * jax.nn.sigmoid(z[:, D:])

    @pl.when(t == 0)
    def _():
        u_scr[0:A_HALO, :] = jnp.zeros((A_HALO, D), F32)

    u_scr[A_HALO:A_HALO + tt, :] = u
    first = A_HALO - (CONV_A_WIDTH - 1)
    for lb in range(D // LANES):
        lanes = slice(lb * LANES, (lb + 1) * LANES)
        for s in range(8):
            rows = tt + A_HALO - (0 if s == 0 else 8)
            shift_scr[s, 0:rows, :] = u_scr[s:s + rows, lanes]

        def row_block(rb, carry, lanes=lanes):
            r0 = rb * CONV_ROWS
            acc = jnp.broadcast_to(b_dw_ref[:, lanes], (CONV_ROWS, LANES))
            for k in range(CONV_A_WIDTH):
                q, s = divmod(first + k, 8)
                start = pl.multiple_of(r0 + 8 * q, 8)
                acc = acc + w_dw_ref[k:k + 1, lanes] * shift_scr[s, pl.ds(start, CONV_ROWS), :]
            y_scr[pl.ds(pl.multiple_of(r0, CONV_ROWS), CONV_ROWS), lanes] = acc
            return carry

        lax.fori_loop(0, tt // CONV_ROWS, row_block, 0)
    y = _ln(y_scr[...], lng_ref[...], lnb_ref[...])
    y = y * jax.nn.sigmoid(y)
    out = _dot(y.astype(BF16), w_out_ref[...])
    _post_mixer(x, out, ada, ln1g_ref[...], ln1b_ref[...], wr_ref, br_ref, x1_ref, h2_ref, lg_ref)
    tail = u_scr[tt:tt + A_HALO, :]
    u_scr[0:A_HALO, :] = tail

    @pl.when(t == pl.num_programs(1) - 1)
    def _():
        st_ref[0] = tail


def _mix_b_prompt_kernel(x_ref, ada_ref, w_grp_ref, scale_ref, ln1g_ref, ln1b_ref,
                         wr_ref, br_ref, x1_ref, h2_ref, lg_ref, st_ref, h_scr, pa_scr, pb_scr):
    t = pl.program_id(1)
    tt = x_ref.shape[0]
    x = x_ref[...]
    ada = ada_ref[0]
    sh1, sc1 = ada[:, 0:D], ada[:, D:2 * D]
    h = x * (1.0 + sc1) + sh1

    @pl.when(t == 0)
    def _():
        h_scr[0:B_HALO, :] = jnp.zeros((B_HALO, D), F32)

    h_scr[B_HALO:B_HALO + tt, :] = h
    pos = (t * tt + lax.broadcasted_iota(jnp.int32, (tt, POOL_GROUP), 0)).astype(F32)
    outs = []
    for gi, w in enumerate(POOL_WINDOWS):
        lo = gi * POOL_GROUP
        end = B_HALO + tt
        start, width, level = 8, 2, 0
        cur = h_scr[start:end, lo:lo + POOL_GROUP] + h_scr[start - 1:end - 1, lo:lo + POOL_GROUP]
        while width < w:
            buf = (pa_scr, pb_scr)[level % 2]
            buf[start:end, :] = cur
            cur = buf[start + 8:end, :] + buf[start + 8 - width:end - width, :]
            start, width, level = start + 8, 2 * width, level + 1
        s = cur[B_HALO - start:, :]
        cnt = jnp.minimum(float(w), pos + 1.0)
        pooled = s / cnt - h[:, lo:lo + POOL_GROUP]
        outs.append(_dot_split(pooled, w_grp_ref[gi].astype(BF16)))
    out = jnp.concatenate(outs, axis=-1) * scale_ref[...]
    _post_mixer(x, out, ada, ln1g_ref[...], ln1b_ref[...], wr_ref, br_ref, x1_ref, h2_ref, lg_ref)
    tail = h_scr[tt:tt + B_HALO, :]
    h_scr[0:B_HALO, :] = tail

    @pl.when(t == pl.num_programs(1) - 1)
    def _():
        st_ref[0] = tail


def _mix_c_prompt_kernel(x_ref, ada_ref, w_in_ref, w_conv_ref, w_out_ref, ln1g_ref, ln1b_ref,
                         wr_ref, br_ref, x1_ref, h2_ref, lg_ref, st_ref, v_scr):
    t = pl.program_id(1)
    tt = x_ref.shape[0]
    x = x_ref[...]
    ada = ada_ref[0]
    sh1, sc1 = ada[:, 0:D], ada[:, D:2 * D]
    h = (x * (1.0 + sc1) + sh1).astype(BF16)
    z = _dot(h, w_in_ref[...])
    bg = z[:, :D]
    v = z[:, D:2 * D] * z[:, 2 * D:]

    @pl.when(t == 0)
    def _():
        v_scr[0:C_HALO, :] = jnp.zeros((C_HALO, D), F32)

    v_scr[C_HALO:C_HALO + tt, :] = v
    first = C_HALO - (CONV_C_WIDTH - 1)
    y = w_conv_ref[CONV_C_WIDTH - 1:CONV_C_WIDTH, :] * v
    for k in range(CONV_C_WIDTH - 1):
        y = y + w_conv_ref[k:k + 1, :] * v_scr[first + k:first + k + tt, :]
    out = _dot((bg * y).astype(BF16), w_out_ref[...])
    _post_mixer(x, out, ada, ln1g_ref[...], ln1b_ref[...], wr_ref, br_ref, x1_ref, h2_ref, lg_ref)
    tail = v_scr[tt:tt + C_HALO, :]
    v_scr[0:C_HALO, :] = tail

    @pl.when(t == pl.num_programs(1) - 1)
    def _():
        st_ref[0] = tail


def _gelu_exact(x):
    return 0.5 * x * (1.0 + lax.erf(x * (2.0 ** -0.5)))


def _mix_d_prompt_kernel(x_ref, ada_ref, w_in_ref, b_in_ref, lng_ref, lnb_ref, w_s_ref, b_st_ref,
                         w_out_ref, ln1g_ref, ln1b_ref, wr_ref, br_ref, x1_ref, h2_ref, lg_ref, st_ref):
    t = pl.program_id(1)
    tt = x_ref.shape[0]
    x = x_ref[...]
    ada = ada_ref[0]
    sh1, sc1 = ada[:, 0:D], ada[:, D:2 * D]
    h = (x * (1.0 + sc1) + sh1).astype(BF16)
    z = _gelu_exact(_dot(h, w_in_ref[...]) + b_in_ref[...])
    u = z[:, :D]
    v = _ln(z[:, D:], lng_ref[...], lnb_ref[...])
    vb = v.astype(BF16)
    row = lax.broadcasted_iota(jnp.int32, (CHUNK, CHUNK), 0)
    col = lax.broadcasted_iota(jnp.int32, (CHUNK, CHUNK), 1)
    causal = col <= row
    chunks = []
    for c in range(tt // CHUNK):
        groups = []
        for g in range(N_SGU_GROUPS):
            w = jnp.where(causal, w_s_ref[g], 0.0).astype(BF16)
            m = _dot(w, vb[c * CHUNK:(c + 1) * CHUNK, g * SGU_GROUP:(g + 1) * SGU_GROUP])
            groups.append(m + b_st_ref[:, g:g + 1])
        chunks.append(jnp.concatenate(groups, axis=-1))
    mixed = jnp.concatenate(chunks, axis=0)
    out = _dot((u * mixed).astype(BF16), w_out_ref[...])
    _post_mixer(x, out, ada, ln1g_ref[...], ln1b_ref[...], wr_ref, br_ref, x1_ref, h2_ref, lg_ref)

    @pl.when(t == pl.num_programs(1) - 1)
    def _():
        st_ref[0] = v[tt - CHUNK:, :]


def _full(shape):
    nd = len(shape)
    return pl.BlockSpec(shape, lambda b, t: (0,) * nd)


def _mix_prompt(kernel_fn, name, x, ada_p, weights, state_rows, scratch, seq):
    n = x.shape[0]
    batch = n // seq
    nt = seq // TOK_TILE
    tok = pl.BlockSpec((TOK_TILE, D), lambda b, t: (b * nt + t, 0))
    lgt = pl.BlockSpec((TOK_TILE, LANES), lambda b, t: (b * nt + t, 0))
    return pl.pallas_call(
        kernel_fn,
        out_shape=(jax.ShapeDtypeStruct((n, D), F32), jax.ShapeDtypeStruct((n, D), BF16),
                   jax.ShapeDtypeStruct((n, LANES), F32), jax.ShapeDtypeStruct((batch, state_rows, D), F32)),
        grid=(batch, nt),
        in_specs=[tok, pl.BlockSpec((1, 1, 6 * D), lambda b, t: (b, 0, 0))] + [_full(w.shape) for w in weights],
        out_specs=(tok, tok, lgt, pl.BlockSpec((1, state_rows, D), lambda b, t: (b, 0, 0))),
        scratch_shapes=scratch,
        compiler_params=_cparams(("arbitrary", "arbitrary")),
        name=name,
    )(x, ada_p, *weights)


def _mix_a_sample_kernel(x_ref, ada_ref, hist_ref, w_in_ref, b_in_ref, w_dw_ref, b_dw_ref, lng_ref, lnb_ref,
                         w_out_ref, ln1g_ref, ln1b_ref, wr_ref, br_ref, x1_ref, h2_ref, lg_ref, new_ref):
    x = x_ref[...]
    ada = ada_ref[...]
    sh1, sc1 = ada[:, 0:D], ada[:, D:2 * D]
    h = (x * (1.0 + sc1) + sh1).astype(BF16)
    z = _dot(h, w_in_ref[...]) + b_in_ref[...]
    u = z[:, :D] * jax.nn.sigmoid(z[:, D:])
    y = b_dw_ref[...] + w_dw_ref[CONV_A_WIDTH - 1:CONV_A_WIDTH, :] * u
    for k in range(CONV_A_WIDTH - 1):
        y = y + w_dw_ref[k:k + 1, :] * hist_ref[k]
    y = _ln(y, lng_ref[...], lnb_ref[...])
    y = y * jax.nn.sigmoid(y)
    out = _dot(y.astype(BF16), w_out_ref[...])
    _post_mixer(x, out, ada, ln1g_ref[...], ln1b_ref[...], wr_ref, br_ref, x1_ref, h2_ref, lg_ref)
    new_ref[...] = u


def _mix_b_sample_kernel(x_ref, ada_ref, hist_ref, w_grp_ref, scale_ref, ln1g_ref, ln1b_ref,
                         wr_ref, br_ref, x1_ref, h2_ref, lg_ref, new_ref):
    x = x_ref[...]
    ada = ada_ref[...]
    sh1, sc1 = ada[:, 0:D], ada[:, D:2 * D]
    h = x * (1.0 + sc1) + sh1
    outs = []
    for gi, w in enumerate(POOL_WINDOWS):
        lo = gi * POOL_GROUP
        s = h[:, lo:lo + POOL_GROUP]
        for j in range(1, w):
            s = s + hist_ref[POOL_HIST - j][:, lo:lo + POOL_GROUP]
        pooled = s / float(w) - h[:, lo:lo + POOL_GROUP]
        outs.append(_dot_split(pooled, w_grp_ref[gi].astype(BF16)))
    out = jnp.concatenate(outs, axis=-1) * scale_ref[...]
    _post_mixer(x, out, ada, ln1g_ref[...], ln1b_ref[...], wr_ref, br_ref, x1_ref, h2_ref, lg_ref)
    new_ref[...] = h


def _mix_c_sample_kernel(x_ref, ada_ref, hist_ref, w_in_ref, w_conv_ref, w_out_ref, ln1g_ref, ln1b_ref,
                         wr_ref, br_ref, x1_ref, h2_ref, lg_ref, new_ref):
    x = x_ref[...]
    ada = ada_ref[...]
    sh1, sc1 = ada[:, 0:D], ada[:, D:2 * D]
    h = (x * (1.0 + sc1) + sh1).astype(BF16)
    z = _dot(h, w_in_ref[...])
    bg = z[:, :D]
    v = z[:, D:2 * D] * z[:, 2 * D:]
    y = w_conv_ref[CONV_C_WIDTH - 1:CONV_C_WIDTH, :] * v
    for k in range(CONV_C_WIDTH - 1):
        y = y + w_conv_ref[k:k + 1, :] * hist_ref[k]
    out = _dot((bg * y).astype(BF16), w_out_ref[...])
    _post_mixer(x, out, ada, ln1g_ref[...], ln1b_ref[...], wr_ref, br_ref, x1_ref, h2_ref, lg_ref)
    new_ref[...] = v


def _mix_d_sample_kernel(x_ref, ada_ref, w_in_ref, b_in_ref, lng_ref, lnb_ref, w_s0_ref, b_s0_ref,
                         w_out_ref, ln1g_ref, ln1b_ref, wr_ref, br_ref, x1_ref, h2_ref, lg_ref, new_ref):
    x = x_ref[...]
    ada = ada_ref[...]
    sh1, sc1 = ada[:, 0:D], ada[:, D:2 * D]
    h = (x * (1.0 + sc1) + sh1).astype(BF16)
    z = _gelu_exact(_dot(h, w_in_ref[...]) + b_in_ref[...])
    u = z[:, :D]
    v = _ln(z[:, D:], lng_ref[...], lnb_ref[...])
    mixed = w_s0_ref[...] * v + b_s0_ref[...]
    out = _dot((u * mixed).astype(BF16), w_out_ref[...])
    _post_mixer(x, out, ada, ln1g_ref[...], ln1b_ref[...], wr_ref, br_ref, x1_ref, h2_ref, lg_ref)
    new_ref[...] = v


def _mix_sample(kernel_fn, name, x, ada_s, arrays):
    n = x.shape[0]
    return pl.pallas_call(
        kernel_fn,
        out_shape=(jax.ShapeDtypeStruct((n, D), F32), jax.ShapeDtypeStruct((n, D), BF16),
                   jax.ShapeDtypeStruct((n, LANES), F32), jax.ShapeDtypeStruct((n, D), F32)),
        compiler_params=_cparams(),
        name=name,
    )(x, ada_s, *arrays)


def _route_kernel(hp_ref, hs_ref, lgp_ref, lgs_ref, xs_ref, meta_ref, cnt_ref, *, n_valid):
    i = pl.program_id(0)
    tt = hp_ref.shape[0]
    ns = hs_ref.shape[0]
    cap = xs_ref.shape[0]
    is_sample = i == pl.num_programs(0) - 1
    hs = jnp.concatenate([hs_ref[...], jnp.zeros((tt - ns, D), BF16)], axis=0)
    lgs = jnp.concatenate([lgs_ref[...], jnp.zeros((tt - ns, LANES), F32)], axis=0)
    hb = jnp.where(is_sample, hs, hp_ref[...])
    logits = jnp.where(is_sample, lgs, lgp_ref[...]).T[:N_EXPERTS, :]
    e_iota = lax.broadcasted_iota(jnp.int32, (N_EXPERTS, tt), 0)
    valid = (i * tt + lax.broadcasted_iota(jnp.int32, (1, tt), 1)) < n_valid
    work = logits
    sel, top = [], []
    for _ in range(TOP_K):
        m = jnp.max(work, axis=0, keepdims=True)
        idx = jnp.min(jnp.where(work == m, e_iota, N_EXPERTS), axis=0, keepdims=True)
        oh = e_iota == idx
        sel.append(oh)
        top.append(m)
        work = jnp.where(oh, -jnp.inf, work)
    ex = [jnp.exp(v - top[0]) for v in top]
    denom = ex[0] + ex[1] + ex[2] + ex[3]
    gates = [e / denom for e in ex]

    member = jnp.where(sel[0] | sel[1] | sel[2] | sel[3], 1.0, 0.0)
    member = jnp.where(valid, member, 0.0)
    r_i = lax.broadcasted_iota(jnp.int32, (tt, tt), 0)
    c_i = lax.broadcasted_iota(jnp.int32, (tt, tt), 1)
    before = jnp.where(r_i < c_i, 1.0, 0.0).astype(BF16)
    rank = _dot(member.astype(BF16), before)
    count = jnp.sum(member, axis=1, keepdims=True)
    chunks = jnp.floor((count + float(ROW_CHUNK - 1)) * (1.0 / ROW_CHUNK))
    chunks_b = jnp.broadcast_to(chunks, (N_EXPERTS, LANES))
    er = lax.broadcasted_iota(jnp.int32, (N_EXPERTS, N_EXPERTS), 0)
    ec = lax.broadcasted_iota(jnp.int32, (N_EXPERTS, N_EXPERTS), 1)
    lower = jnp.where(ec < er, 1.0, 0.0).astype(BF16)
    chunk_off = _dot(lower, chunks_b.astype(BF16))
    base = chunk_off[:, 0:1] * float(ROW_CHUNK)
    slot = base + rank
    pos = [jnp.where(valid, jnp.sum(jnp.where(s, slot, 0.0), axis=0, keepdims=True), -1.0) for s in sel]

    row = lax.broadcasted_iota(jnp.int32, (cap, tt), 0).astype(jnp.int16)
    pos16 = [p.astype(jnp.int32).astype(jnp.int16) for p in pos]
    one, zero = jnp.ones((), BF16), jnp.zeros((), BF16)
    onehot = jnp.where(row == pos16[0], one, zero)
    for k in range(1, TOP_K):
        onehot = jnp.where(row == pos16[k], one, onehot)
    xs_ref[...] = _dot(onehot, hb).astype(BF16)

    meta = jnp.concatenate(pos + gates + [jnp.zeros((LANES - 2 * TOP_K, tt), F32)], axis=0)
    meta_ref[...] = meta.T
    cnt_ref[0] = chunks_b.astype(jnp.int32)


def _route(h2_p, h2_s, lg_p, lg_s):
    n_p, n_s = h2_p.shape[0], h2_s.shape[0]
    ntp = n_p // MOE_TILE
    nt = ntp + 1
    cap = _tile_capacity(MOE_TILE)
    prompt_tile = lambda i: (jnp.minimum(i, ntp - 1), 0)
    return pl.pallas_call(
        functools.partial(_route_kernel, n_valid=n_p + n_s),
        out_shape=(jax.ShapeDtypeStruct((nt * cap, D), BF16),
                   jax.ShapeDtypeStruct((nt * MOE_TILE, LANES), F32),
                   jax.ShapeDtypeStruct((nt, N_EXPERTS, LANES), jnp.int32)),
        grid=(nt,),
        in_specs=[pl.BlockSpec((MOE_TILE, D), prompt_tile),
                  pl.BlockSpec((n_s, D), lambda i: (0, 0)),
                  pl.BlockSpec((MOE_TILE, LANES), prompt_tile),
                  pl.BlockSpec((n_s, LANES), lambda i: (0, 0))],
        out_specs=(pl.BlockSpec((cap, D), lambda i: (i, 0)),
                   pl.BlockSpec((MOE_TILE, LANES), lambda i: (i, 0)),
                   pl.BlockSpec((1, N_EXPERTS, LANES), lambda i: (i, 0, 0))),
        compiler_params=_cparams(("arbitrary",)),
        name="route_dispatch",
    )(h2_p, h2_s, lg_p, lg_s)


def _plan_kernel(cnt_ref, be_ref, nx_ref, half_ref, nb_ref, src_ref, dst_ref, z_ref, nz_ref, tile_off, blk0_s,
                 nblk_s, npad_s, *, nt, cap_chunks, nb_max, max_pad, nz_max):
    del nb_max, nz_max
    i32 = jnp.int32
    trash_base = nt * cap_chunks
    shift = BLOCK_CHUNKS.bit_length() - 1

    def index_grid(ref):
        r, c = ref.shape
        return lax.broadcasted_iota(i32, (r, c), 0) * c + lax.broadcasted_iota(i32, (r, c), 1)

    def in_run(idx, start, length):
        return jnp.logical_and(idx >= start, idx < start + length)

    for t in range(nt):
        tile_off[t] = 0

    d = index_grid(src_ref)

    def expert_body(e, carry):
        p0, blk0, pad0, src, dst = carry

        def tile_body(t, inner):
            p, src, dst = inner
            c = cnt_ref[t * N_EXPERTS + e]
            off = tile_off[t]
            tile_off[t] = off + c
            run = in_run(d, p, c)
            val = d + (t * cap_chunks + off - p)
            return p + c, jnp.where(run, val, src), jnp.where(run, val, dst)

        p1, src, dst = lax.fori_loop(0, nt, tile_body, (p0, src, dst))
        nblk = lax.shift_right_logical(p1 - p0 + (BLOCK_CHUNKS - 1), shift)
        p2 = p0 + nblk * BLOCK_CHUNKS
        pad = in_run(d, p1, p2 - p1)
        src = jnp.where(pad, 0, src)
        dst = jnp.where(pad, d + (trash_base + pad0 - p1), dst)
        blk0_s[e] = blk0
        nblk_s[e] = nblk
        npad_s[e] = p2 - p1
        return p2, blk0 + nblk, pad0 + (p2 - p1), src, dst

    zeros = jnp.zeros(src_ref.shape, i32)
    zero = jnp.int32(0)
    _, nb, n_pad, src, dst = lax.fori_loop(0, N_EXPERTS, expert_body, (zero, zero, zero, zeros, zeros))
    src_ref[...] = src
    dst_ref[...] = dst
    nb_ref[0] = nb

    bi = index_grid(be_ref)

    def block_body(k, carry):
        be, nx, half, nxt = carry
        e = N_EXPERTS - 1 - k
        b0, n = blk0_s[e], nblk_s[e]
        mine = in_run(bi, b0, n)
        last = jnp.logical_and(bi == b0 + n - 1, n > 0)
        empty_quarters = lax.shift_right_logical(npad_s[e], QUARTER_CHUNKS.bit_length() - 1)
        return (jnp.where(mine, e, be), jnp.where(mine, nxt, nx), jnp.where(last, empty_quarters, half),
                jnp.where(n > 0, e, nxt))

    init = (jnp.full(be_ref.shape, N_EXPERTS - 1, i32), jnp.full(be_ref.shape, -1, i32),
            jnp.zeros(be_ref.shape, i32), jnp.int32(-1))
    be, nx, half, _ = lax.fori_loop(0, N_EXPERTS, block_body, init)
    be_ref[...] = be
    nx_ref[...] = nx
    half_ref[...] = half

    zi = index_grid(z_ref)

    def tail_body(t, carry):
        z0, z = carry
        used = tile_off[t]
        n = cap_chunks - used
        return z0 + n, jnp.where(in_run(zi, z0, n), zi + (t * cap_chunks + used - z0), z)

    z1, z = lax.fori_loop(0, nt, tail_body, (zero, jnp.zeros(z_ref.shape, i32)))
    n_trash = max_pad - n_pad
    z_ref[...] = jnp.where(in_run(zi, z1, n_trash), zi + (trash_base + n_pad - z1), z)
    nz_ref[0] = z1 + n_trash


def _plan(cnt_flat, nt, cap_chunks, nb_max, max_pad, nz_max):
    smem = pl.BlockSpec(memory_space=pltpu.SMEM)
    vmem = pl.BlockSpec(memory_space=pltpu.VMEM)
    i32 = jnp.int32

    def table(n):
        return jax.ShapeDtypeStruct((-(-n // (8 * LANES)) * 8, LANES), i32)

    out = pl.pallas_call(
        functools.partial(_plan_kernel, nt=nt, cap_chunks=cap_chunks, nb_max=nb_max, max_pad=max_pad, nz_max=nz_max),
        out_shape=(table(nb_max), table(nb_max), table(nb_max), jax.ShapeDtypeStruct((1,), i32),
                   table(nb_max * BLOCK_CHUNKS), table(nb_max * BLOCK_CHUNKS),
                   table(nz_max), jax.ShapeDtypeStruct((1,), i32)),
        in_specs=[smem],
        out_specs=(vmem, vmem, vmem, smem, vmem, vmem, vmem, smem),
        scratch_shapes=[pltpu.SMEM((nt,), i32), pltpu.SMEM((N_EXPERTS,), i32), pltpu.SMEM((N_EXPERTS,), i32),
                        pltpu.SMEM((N_EXPERTS,), i32)],
        name="plan",
    )(cnt_flat)
    return tuple(o.reshape(-1) for o in out)


def _expert_kernel(be_ref, nx_ref, half_ref, nb_ref, src_ref, dst_ref, z_ref, nz_ref, xs_hbm, w_in_hbm, b_in_ref,
                   w_out_hbm, b_out_ref, ys_hbm, xbuf, ybuf, w_in_f32, w_out_f32, w_in_bf, w_out_bf, zbuf,
                   sem_in, sem_out, sem_z, sem_w, *, layer):
    b = pl.program_id(0)
    nb = nb_ref[0]
    nz = nz_ref[0]
    slot = lax.rem(b, 2)

    def weight_copies(e):
        return (pltpu.make_async_copy(w_in_hbm.at[layer, e], w_in_f32, sem_w.at[0]),
                pltpu.make_async_copy(w_out_hbm.at[layer, e], w_out_f32, sem_w.at[1]))

    def ffn(rows):
        e = be_ref[b]
        x = xbuf[slot, 0:rows, :]
        h = _dot(x, w_in_bf[...]) + b_in_ref[layer, e]
        g = jnp.minimum(h[:, :D], SWIGLU_LIMIT)
        u = jnp.clip(h[:, D:], -SWIGLU_LIMIT, SWIGLU_LIMIT)
        a = (u + 1.0) * (g * jax.nn.sigmoid(SWIGLU_ALPHA * g))
        y = _dot(a.astype(BF16), w_out_bf[...]) + b_out_ref[layer, e]
        ybuf[slot, 0:rows, :] = y.astype(BF16)

    def chunk_rows(c):
        return pl.ds(pl.multiple_of(c * ROW_CHUNK, ROW_CHUNK), ROW_CHUNK)

    def gather(blk, s):
        for j in range(BLOCK_CHUNKS):
            c = src_ref[blk * BLOCK_CHUNKS + j]
            pltpu.make_async_copy(xs_hbm.at[chunk_rows(c)], xbuf.at[s, pl.ds(j * ROW_CHUNK, ROW_CHUNK)],
                                  sem_in.at[s]).start()

    def gather_wait(s):
        pltpu.make_async_copy(xs_hbm.at[pl.ds(0, BLOCK_ROWS)], xbuf.at[s], sem_in.at[s]).wait()

    def scatter(blk, s):
        for j in range(BLOCK_CHUNKS):
            c = dst_ref[blk * BLOCK_CHUNKS + j]
            pltpu.make_async_copy(ybuf.at[s, pl.ds(j * ROW_CHUNK, ROW_CHUNK)], ys_hbm.at[chunk_rows(c)],
                                  sem_out.at[s]).start()

    def scatter_wait(s):
        pltpu.make_async_copy(ybuf.at[s], ys_hbm.at[pl.ds(0, BLOCK_ROWS)], sem_out.at[s]).wait()

    @pl.when(b == 0)
    def _():
        for c in weight_copies(be_ref[0]):
            c.start()
        gather(0, 0)
        zbuf[...] = jnp.zeros(zbuf.shape, zbuf.dtype)
        ybuf[...] = jnp.zeros(ybuf.shape, ybuf.dtype)

        def zero_one(i, _):
            pltpu.make_async_copy(zbuf, ys_hbm.at[chunk_rows(z_ref[i])], sem_z).start()
            return 0

        lax.fori_loop(0, nz, zero_one, 0)

    @pl.when(b < nb)
    def _():
        new_expert = jnp.logical_or(b == 0, be_ref[b] != be_ref[jnp.maximum(b - 1, 0)])

        @pl.when(new_expert)
        def _():
            for c in weight_copies(be_ref[b]):
                c.wait()
            w_in_bf[...] = w_in_f32[...].astype(BF16)
            w_out_bf[...] = w_out_f32[...].astype(BF16)

            @pl.when(nx_ref[b] >= 0)
            def _():
                for c in weight_copies(nx_ref[b]):
                    c.start()

        gather_wait(slot)

        @pl.when(b + 1 < nb)
        def _():
            gather(b + 1, 1 - slot)

        @pl.when(b >= 2)
        def _():
            scatter_wait(slot)

        for empty in range(4):
            @pl.when(half_ref[b] == empty)
            def _(empty=empty):
                ffn(BLOCK_ROWS - empty * QUARTER_ROWS)

        scatter(b, slot)

        @pl.when(b == 0)
        def _():
            def zero_wait(i, _):
                pltpu.make_async_copy(zbuf, ys_hbm.at[pl.ds(0, ROW_CHUNK)], sem_z).wait()
                return 0

            lax.fori_loop(0, nz, zero_wait, 0)

        @pl.when(b == nb - 1)
        def _():
            scatter_wait(slot)

            @pl.when(b >= 1)
            def _():
                scatter_wait(1 - slot)


def _experts(xs, plan, layer, w_in, b_in, w_out, b_out, nb_max, ys_rows):
    resident = lambda b, *_: (0, 0, 0, 0)
    grid_spec = pltpu.PrefetchScalarGridSpec(
        num_scalar_prefetch=8,
        grid=(nb_max,),
        in_specs=[pl.BlockSpec(memory_space=pl.ANY),
                  pl.BlockSpec(memory_space=pl.ANY),
                  pl.BlockSpec(b_in.shape, resident),
                  pl.BlockSpec(memory_space=pl.ANY),
                  pl.BlockSpec(b_out.shape, resident)],
        out_specs=pl.BlockSpec(memory_space=pl.ANY),
        scratch_shapes=[pltpu.VMEM((2, BLOCK_ROWS, D), BF16), pltpu.VMEM((2, BLOCK_ROWS, D), BF16),
                        pltpu.VMEM((D, 2 * D), F32), pltpu.VMEM((D, D), F32),
                        pltpu.VMEM((D, 2 * D), BF16), pltpu.VMEM((D, D), BF16), pltpu.VMEM((ROW_CHUNK, D), BF16),
                        pltpu.SemaphoreType.DMA((2,)), pltpu.SemaphoreType.DMA((2,)), pltpu.SemaphoreType.DMA(()),
                        pltpu.SemaphoreType.DMA((2,))],
    )
    return pl.pallas_call(
        functools.partial(_expert_kernel, layer=layer),
        out_shape=jax.ShapeDtypeStruct((ys_rows, D), BF16),
        grid_spec=grid_spec,
        compiler_params=_cparams(("arbitrary",)),
        name="experts",
    )(*plan, xs, w_in, b_in, w_out, b_out)


def _combine_kernel(ys_ref, meta_ref, x1_ref, ada_ref, lng_ref, lnb_ref, o_ref):
    tt = x1_ref.shape[0]
    cap = ys_ref.shape[0]
    meta = meta_ref[...]
    col = lax.broadcasted_iota(jnp.int32, (tt, cap), 1).astype(jnp.int16)
    gmat = jnp.zeros((tt, cap), BF16)
    for k in range(TOP_K):
        pos_k = meta[:, k:k + 1].astype(jnp.int32).astype(jnp.int16)
        gate_k = meta[:, TOP_K + k:TOP_K + k + 1].astype(BF16)
        gmat = jnp.where(col == pos_k, gate_k, gmat)
    f = _dot(gmat, ys_ref[...])
    ada = ada_ref[0] if len(ada_ref.shape) == 3 else ada_ref[...]
    g2 = ada[:, 5 * D:6 * D]
    o_ref[...] = _ln(ALPHA * x1_ref[...] + (1.0 + g2) * f, lng_ref[...], lnb_ref[...])


def _combine_prompt(ys, meta, x1, ada_p, ln_g, ln_b, seq):
    n = x1.shape[0]
    cap = _tile_capacity(MOE_TILE)
    per_seq = seq // MOE_TILE
    return pl.pallas_call(
        _combine_kernel,
        out_shape=jax.ShapeDtypeStruct((n, D), F32),
        grid=(n // MOE_TILE,),
        in_specs=[pl.BlockSpec((cap, D), lambda i: (i, 0)),
                  pl.BlockSpec((MOE_TILE, LANES), lambda i: (i, 0)),
                  pl.BlockSpec((MOE_TILE, D), lambda i: (i, 0)),
                  pl.BlockSpec((1, 1, 6 * D), lambda i: (i // per_seq, 0, 0)),
                  pl.BlockSpec((1, D), lambda i: (0, 0)),
                  pl.BlockSpec((1, D), lambda i: (0, 0))],
        out_specs=pl.BlockSpec((MOE_TILE, D), lambda i: (i, 0)),
        compiler_params=_cparams(("arbitrary",)),
        name="combine_prompt",
    )(ys, meta, x1, ada_p, ln_g, ln_b)


def _combine_sample(ys, meta, x1, ada_s, ln_g, ln_b, tile):
    n = x1.shape[0]
    cap = _tile_capacity(MOE_TILE)
    return pl.pallas_call(
        _combine_kernel,
        out_shape=jax.ShapeDtypeStruct((n, D), F32),
        grid=(1,),
        in_specs=[pl.BlockSpec((cap, D), lambda i: (tile, 0)),
                  pl.BlockSpec((n, LANES), lambda i: (tile * (MOE_TILE // n), 0)),
                  pl.BlockSpec((n, D), lambda i: (0, 0)),
                  pl.BlockSpec((n, 6 * D), lambda i: (0, 0)),
                  pl.BlockSpec((1, D), lambda i: (0, 0)),
                  pl.BlockSpec((1, D), lambda i: (0, 0))],
        out_specs=pl.BlockSpec((n, D), lambda i: (0, 0)),
        compiler_params=_cparams(("arbitrary",)),
        name="combine_sample",
    )(ys, meta, x1, ada_s, ln_g, ln_b)


def _moe_and_norm(h2_p, h2_s, lg_p, lg_s, x1_p, x1_s, ada_p, ada_s, ln_g, ln_b,
                  layer, w_in, b_in, w_out, b_out, seq):
    n_p, n_s = h2_p.shape[0], h2_s.shape[0]
    n_valid = n_p + n_s
    nt = n_p // MOE_TILE + 1
    cap = _tile_capacity(MOE_TILE)
    cap_chunks = cap // ROW_CHUNK
    xs, meta, cnt = _route(h2_p, h2_s, lg_p, lg_s)
    total_chunks_max = (TOP_K * n_valid + nt * N_EXPERTS * (ROW_CHUNK - 1)) // ROW_CHUNK
    nb_max = -(-total_chunks_max // BLOCK_CHUNKS) + N_EXPERTS
    max_pad = N_EXPERTS * (BLOCK_CHUNKS - 1)
    nz_max = nt * cap_chunks - (TOP_K * n_valid) // ROW_CHUNK + max_pad
    plan = _plan(cnt[:, :, 0].reshape(-1), nt, cap_chunks, nb_max, max_pad, nz_max)
    ys = _experts(xs, plan, layer, w_in, b_in, w_out, b_out, nb_max, (nt * cap_chunks + max_pad) * ROW_CHUNK)
    x2_p = _combine_prompt(ys, meta, x1_p, ada_p, ln_g, ln_b, seq)
    x2_s = _combine_sample(ys, meta, x1_s, ada_s, ln_g, ln_b, n_p // MOE_TILE)
    return x2_p, x2_s


def _row(v):
    return v.reshape(1, -1)


def kernel(x_prompt, x_sample, state_conv_a, state_pool_b, state_conv_c, c_prompt, c_sample,
           w_ada, b_ada, ln1_g, ln1_b, ln2_g, ln2_b,
           a_w_in, a_b_in, a_w_dw, a_b_dw, a_ln_g, a_ln_b, a_w_out,
           b_w_grp, b_scale, c_w_in, c_w_conv, c_w_out,
           d_w_in, d_b_in, d_ln_g, d_ln_b, d_w_s, d_b_s, d_w_out,
           w_router, b_router, w_moe_in, b_moe_in, w_moe_out, b_moe_out):
    bp, seq, _ = x_prompt.shape
    bs = x_sample.shape[0]
    assert seq % TOK_TILE == 0 and seq % MOE_TILE == 0 and MOE_TILE % bs == 0
    assert x_sample.shape[1] == 1 and w_ada.shape[0] == DEPTH == 4

    ada = _ada(jnp.concatenate([c_prompt, c_sample], axis=0), w_ada, b_ada)
    xp = x_prompt.reshape(bp * seq, D)
    xs = x_sample.reshape(bs, D)
    b_moe_in4 = b_moe_in.reshape(DEPTH, N_EXPERTS, 1, 2 * D)
    b_moe_out4 = b_moe_out.reshape(DEPTH, N_EXPERTS, 1, D)
    states = {}

    for i in range(DEPTH):
        ada_p = ada[i, :bp].reshape(bp, 1, 6 * D)
        ada_s = ada[i, bp:]
        wr = jnp.pad(w_router[i], ((0, 0), (0, LANES - N_EXPERTS)))
        ln1 = [_row(ln1_g[i]), _row(ln1_b[i]), wr, _row(jnp.pad(b_router[i], (0, LANES - N_EXPERTS)))]
        if i == 0:
            wts = [a_w_in[0].astype(BF16), _row(a_b_in[0]), a_w_dw[0], _row(a_b_dw[0]), _row(a_ln_g[0]),
                   _row(a_ln_b[0]), a_w_out[0].astype(BF16)] + ln1
            x1_p, h2_p, lg_p, st =_mix_prompt(_mix_a_prompt_kernel, "mix_a_prompt", xp, ada_p, wts, A_HALO,
                                         [pltpu.VMEM((A_HALO + TOK_TILE, D), F32),
                                          pltpu.VMEM((8, A_HALO + TOK_TILE, LANES), F32),
                                          pltpu.VMEM((TOK_TILE, D), F32)], seq)
            states["a_p"] = st[:, A_HALO - (CONV_A_WIDTH - 1):][None]
            hist = jnp.transpose(state_conv_a[0], (1, 0, 2))
            x1_s, h2_s, lg_s, new =_mix_sample(_mix_a_sample_kernel, "mix_a_sample", xs, ada_s, [hist] + wts)
            states["a_s"] = jnp.concatenate([state_conv_a[0][:, 1:], new[:, None]], axis=1)[None]
        elif i == 1:
            wts = [b_w_grp[0], _row(b_scale[0])] + ln1
            x1_p, h2_p, lg_p, st =_mix_prompt(_mix_b_prompt_kernel, "mix_b_prompt", xp, ada_p, wts, B_HALO,
                                         [pltpu.VMEM((B_HALO + TOK_TILE, D), F32),
                                          pltpu.VMEM((B_HALO + TOK_TILE, POOL_GROUP), F32),
                                          pltpu.VMEM((B_HALO + TOK_TILE, POOL_GROUP), F32)], seq)
            states["b_p"] = st[:, B_HALO - POOL_HIST:][None]
            hist = jnp.transpose(state_pool_b[0], (1, 0, 2))
            x1_s, h2_s, lg_s, new =_mix_sample(_mix_b_sample_kernel, "mix_b_sample", xs, ada_s, [hist] + wts)
            states["b_s"] = jnp.concatenate([state_pool_b[0][:, 1:], new[:, None]], axis=1)[None]
        elif i == 2:
            wts = [c_w_in[0].astype(BF16), c_w_conv[0], c_w_out[0].astype(BF16)] + ln1
            x1_p, h2_p, lg_p, st =_mix_prompt(_mix_c_prompt_kernel, "mix_c_prompt", xp, ada_p, wts, C_HALO,
                                         [pltpu.VMEM((C_HALO + TOK_TILE, D), F32)], seq)
            states["c_p"] = st[:, C_HALO - (CONV_C_WIDTH - 1):][None]
            hist = jnp.transpose(state_conv_c[0], (1, 0, 2))
            x1_s, h2_s, lg_s, new =_mix_sample(_mix_c_sample_kernel, "mix_c_sample", xs, ada_s, [hist] + wts)
            states["c_s"] = jnp.concatenate([state_conv_c[0][:, 1:], new[:, None]], axis=1)[None]
        else:
            common = [d_w_in[0].astype(BF16), _row(d_b_in[0]), _row(d_ln_g[0]), _row(d_ln_b[0])]
            wts = common + [d_w_s[0], d_b_s[0].T, d_w_out[0].astype(BF16)] + ln1
            x1_p, h2_p, lg_p, st =_mix_prompt(_mix_d_prompt_kernel, "mix_d_prompt", xp, ada_p, wts, CHUNK, [], seq)
            states["d_p"] = st[None]
            w_s0 = jnp.repeat(d_w_s[0][:, 0, 0], SGU_GROUP).reshape(1, D)
            b_s0 = jnp.repeat(d_b_s[0][:, 0], SGU_GROUP).reshape(1, D)
            wts_s = common + [w_s0, b_s0, d_w_out[0].astype(BF16)] + ln1
            x1_s, h2_s, lg_s, new =_mix_sample(_mix_d_sample_kernel, "mix_d_sample", xs, ada_s, wts_s)
            states["d_s"] = new[:, None][None]

        xp, xs = _moe_and_norm(h2_p, h2_s, lg_p, lg_s, x1_p, x1_s, ada_p, ada_s, _row(ln2_g[i]), _row(ln2_b[i]),
                               i, w_moe_in, b_moe_in4, w_moe_out, b_moe_out4, seq)

    return (xp.reshape(bp, seq, D), xs.reshape(bs, 1, D),
            states["a_p"], states["a_s"], states["b_p"], states["b_s"],
            states["c_p"], states["c_s"], states["d_p"], states["d_s"])
```

```python
import functools

import jax
import jax.numpy as jnp
from jax import lax
from jax.experimental import pallas as pl
from jax.experimental.pallas import tpu as pltpu

F32 = jnp.float32
BF16 = jnp.bfloat16

D = 1024
DEPTH = 4
N_EXPERTS = 32
TOP_K = 4
CONV_A_WIDTH = 31
POOL_WINDOWS = (2, 4, 8, 16)
POOL_GROUP = D // 4
POOL_HIST = 15
CONV_C_WIDTH = 3
CHUNK = 128
N_SGU_GROUPS = 4
SGU_GROUP = D // N_SGU_GROUPS
SWIGLU_LIMIT = 7.0
SWIGLU_ALPHA = 1.702
ALPHA = (2 * DEPTH) ** 0.25
LN_EPS = 1e-5

LANES = 128
SUBLANES_BF16 = 16
VMEM_LIMIT = 56 * 1024 * 1024

TOK_TILE = 512
MOE_TILE = 512
ROW_CHUNK = SUBLANES_BF16
BLOCK_CHUNKS = 32
BLOCK_ROWS = BLOCK_CHUNKS * ROW_CHUNK
QUARTER_CHUNKS = BLOCK_CHUNKS // 4
QUARTER_ROWS = BLOCK_ROWS // 4


def _tile_capacity(tt):
    cap = TOP_K * tt + N_EXPERTS * (ROW_CHUNK - 1)
    return -(-cap // 256) * 256


def _cparams(sem=None):
    return pltpu.CompilerParams(dimension_semantics=sem, vmem_limit_bytes=VMEM_LIMIT)


def _dot(a, b):
    return jnp.dot(a, b, preferred_element_type=F32)


def _dot_split(a, b):
    hi = a.astype(BF16)
    lo = (a - hi.astype(F32)).astype(BF16)
    return _dot(hi, b) + _dot(lo, b)


def _ln(x, g, b):
    mu = jnp.mean(x, axis=-1, keepdims=True)
    xc = x - mu
    var = jnp.mean(xc * xc, axis=-1, keepdims=True)
    return xc * lax.rsqrt(var + LN_EPS) * g + b


def _split_ada(ada):
    return [ada[:, i * D:(i + 1) * D] for i in range(6)]


def _post_mixer(x, out, ada, ln_g, ln_b, wr_ref, br_ref, x1_ref, h2_ref, lg_ref):
    _, _, g1, sh2, sc2, _ = _split_ada(ada)
    x1 = _ln(ALPHA * x + (1.0 + g1) * out, ln_g, ln_b)
    x1_ref[...] = x1
    h2 = x1 * (1.0 + sc2) + sh2
    hi = h2.astype(BF16)
    h2_ref[...] = hi
    lo = (h2 - hi.astype(F32)).astype(BF16)
    wr = wr_ref[...]
    w_hi = wr.astype(BF16)
    w_lo = (wr - w_hi.astype(F32)).astype(BF16)
    both = _dot(hi, jnp.concatenate([w_hi, w_lo], axis=1))
    lg_ref[...] = both[:, :LANES] + both[:, LANES:] + _dot(lo, w_hi) + br_ref[...]


def _ada_kernel(c_ref, w_ref, b_ref, o_ref):
    c = c_ref[...]
    sc = (c * jax.nn.sigmoid(c)).astype(BF16)
    o_ref[0] = _dot(sc, w_ref[0].astype(BF16)) + b_ref[0]


def _ada(c_all, w_ada, b_ada):
    rows = c_all.shape[0]
    return pl.pallas_call(
        _ada_kernel,
        out_shape=jax.ShapeDtypeStruct((DEPTH, rows, 6 * D), F32),
        grid=(DEPTH, 6),
        in_specs=[
            pl.BlockSpec((rows, D), lambda i, j: (0, 0)),
            pl.BlockSpec((1, D, D), lambda i, j: (i, 0, j)),
            pl.BlockSpec((1, 1, D), lambda i, j: (i, 0, j)),
        ],
        out_specs=pl.BlockSpec((1, rows, D), lambda i, j: (i, 0, j)),
        compiler_params=_cparams(("arbitrary", "arbitrary")),
        name="ada",
    )(c_all, w_ada, b_ada.reshape(DEPTH, 1, 6 * D))


A_HALO = 32
B_HALO = 32
C_HALO = 8
CONV_ROWS = 128


def _mix_a_prompt_kernel(x_ref, ada_ref, w_in_ref, b_in_ref, w_dw_ref, b_dw_ref, lng_ref, lnb_ref,
                         w_out_ref, ln1g_ref, ln1b_ref, wr_ref, br_ref, x1_ref, h2_ref, lg_ref, st_ref,
                         u_scr, shift_scr, y_scr):
    t = pl.program_id(1)
    tt = x_ref.shape[0]
    x = x_ref[...]
    ada = ada_ref[0]
    sh1, sc1 = ada[:, 0:D], ada[:, D:2 * D]
    h = (x * (1.0 + sc1) + sh1).astype(BF16)
    z = _dot(h, w_in_ref[...]) + b_in_ref[...]
    u = z[:, :D] * jax.nn.sigmoid(z[:, D:])

    @pl.when(t == 0)
    def _():
        u_scr[0:A_HALO, :] = jnp.zeros((A_HALO, D), F32)

    u_scr[A_HALO:A_HALO + tt, :] = u
    first = A_HALO - (CONV_A_WIDTH - 1)
    for lb in range(D // LANES):
        lanes = slice(lb * LANES, (lb + 1) * LANES)
        for s in range(8):
            rows = tt + A_HALO - (0 if s == 0 else 8)
            shift_scr[s, 0:rows, :] = u_scr[s:s + rows, lanes]

        def row_block(rb, carry, lanes=lanes):
            r0 = rb * CONV_ROWS
            acc = jnp.broadcast_to(b_dw_ref[:, lanes], (CONV_ROWS, LANES))
            for k in range(CONV_A_WIDTH):
                q, s = divmod(first + k, 8)
                start = pl.multiple_of(r0 + 8 * q, 8)
                acc = acc + w_dw_ref[k:k + 1, lanes] * shift_scr[s, pl.ds(start, CONV_ROWS), :]
            y_scr[pl.ds(pl.multiple_of(r0, CONV_ROWS), CONV_ROWS), lanes] = acc
            return carry

        lax.fori_loop(0, tt // CONV_ROWS, row_block, 0)
    y = _ln(y_scr[...], lng_ref[...], lnb_ref[...])
    y = y * jax.nn.sigmoid(y)
    out = _dot(y.astype(BF16), w_out_ref[...])
    _post_mixer(x, out, ada, ln1g_ref[...], ln1b_ref[...], wr_ref, br_ref, x1_ref, h2_ref, lg_ref)
    tail = u_scr[tt:tt + A_HALO, :]
    u_scr[0:A_HALO, :] = tail

    @pl.when(t == pl.num_programs(1) - 1)
    def _():
        st_ref[0] = tail


def _mix_b_prompt_kernel(x_ref, ada_ref, w_grp_ref, scale_ref, ln1g_ref, ln1b_ref,
                         wr_ref, br_ref, x1_ref, h2_ref, lg_ref, st_ref, h_scr, pa_scr, pb_scr):
    t = pl.program_id(1)
    tt = x_ref.shape[0]
    x = x_ref[...]
    ada = ada_ref[0]
    sh1, sc1 = ada[:, 0:D], ada[:, D:2 * D]
    h = x * (1.0 + sc1) + sh1

    @pl.when(t == 0)
    def _():
        h_scr[0:B_HALO, :] = jnp.zeros((B_HALO, D), F32)

    h_scr[B_HALO:B_HALO + tt, :] = h
    pos = (t * tt + lax.broadcasted_iota(jnp.int32, (tt, POOL_GROUP), 0)).astype(F32)
    outs = []
    for gi, w in enumerate(POOL_WINDOWS):
        lo = gi * POOL_GROUP
        end = B_HALO + tt
        start, width, level = 8, 2, 0
        cur = h_scr[start:end, lo:lo + POOL_GROUP] + h_scr[start - 1:end - 1, lo:lo + POOL_GROUP]
        while width < w:
            buf = (pa_scr, pb_scr)[level % 2]
            buf[start:end, :] = cur
            cur = buf[start + 8:end, :] + buf[start + 8 - width:end - width, :]
            start, width, level = start + 8, 2 * width, level + 1
        s = cur[B_HALO - start:, :]
        cnt = jnp.minimum(float(w), pos + 1.0)
        pooled = s / cnt - h[:, lo:lo + POOL_GROUP]
        outs.append(_dot_split(pooled, w_grp_ref[gi].astype(BF16)))
    out = jnp.concatenate(outs, axis=-1) * scale_ref[...]
    _post_mixer(x, out, ada, ln1g_ref[...], ln1b_ref[...], wr_ref, br_ref, x1_ref, h2_ref, lg_ref)
    tail = h_scr[tt:tt + B_HALO, :]
    h_scr[0:B_HALO, :] = tail

    @pl.when(t == pl.num_programs(1) - 1)
    def _():
        st_ref[0] = tail


def _mix_c_prompt_kernel(x_ref, ada_ref, w_in_ref, w_conv_ref, w_out_ref, ln1g_ref, ln1b_ref,
                         wr_ref, br_ref, x1_ref, h2_ref, lg_ref, st_ref, v_scr):
    t = pl.program_id(1)
    tt = x_ref.shape[0]
    x = x_ref[...]
    ada = ada_ref[0]
    sh1, sc1 = ada[:, 0:D], ada[:, D:2 * D]
    h = (x * (1.0 + sc1) + sh1).astype(BF16)
    z = _dot(h, w_in_ref[...])
    bg = z[:, :D]
    v = z[:, D:2 * D] * z[:, 2 * D:]

    @pl.when(t == 0)
    def _():
        v_scr[0:C_HALO, :] = jnp.zeros((C_HALO, D), F32)

    v_scr[C_HALO:C_HALO + tt, :] = v
    first = C_HALO - (CONV_C_WIDTH - 1)
    y = w_conv_ref[CONV_C_WIDTH - 1:CONV_C_WIDTH, :] * v
    for k in range(CONV_C_WIDTH - 1):
        y = y + w_conv_ref[k:k + 1, :] * v_scr[first + k:first + k + tt, :]
    out = _dot((bg * y).astype(BF16), w_out_ref[...])
    _post_mixer(x, out, ada, ln1g_ref[...], ln1b_ref[...], wr_ref, br_ref, x1_ref, h2_ref, lg_ref)
    tail = v_scr[tt:tt + C_HALO, :]
    v_scr[0:C_HALO, :] = tail

    @pl.when(t == pl.num_programs(1) - 1)
    def _():
        st_ref[0] = tail


def _gelu_exact(x):
    return 0.5 * x * (1.0 + lax.erf(x * (2.0 ** -0.5)))


def _mix_d_prompt_kernel(x_ref, ada_ref, w_in_ref, b_in_ref, lng_ref, lnb_ref, w_s_ref, b_st_ref,
                         w_out_ref, ln1g_ref, ln1b_ref, wr_ref, br_ref, x1_ref, h2_ref, lg_ref, st_ref):
    t = pl.program_id(1)
    tt = x_ref.shape[0]
    x = x_ref[...]
    ada = ada_ref[0]
    sh1, sc1 = ada[:, 0:D], ada[:, D:2 * D]
    h = (x * (1.0 + sc1) + sh1).astype(BF16)
    z = _gelu_exact(_dot(h, w_in_ref[...]) + b_in_ref[...])
    u = z[:, :D]
    v = _ln(z[:, D:], lng_ref[...], lnb_ref[...])
    vb = v.astype(BF16)
    row = lax.broadcasted_iota(jnp.int32, (CHUNK, CHUNK), 0)
    col = lax.broadcasted_iota(jnp.int32, (CHUNK, CHUNK), 1)
    causal = col <= row
    chunks = []
    for c in range(tt // CHUNK):
        groups = []
        for g in range(N_SGU_GROUPS):
            w = jnp.where(causal, w_s_ref[g], 0.0).astype(BF16)
            m = _dot(w, vb[c * CHUNK:(c + 1) * CHUNK, g * SGU_GROUP:(g + 1) * SGU_GROUP])
            groups.append(m + b_st_ref[:, g:g + 1])
        chunks.append(jnp.concatenate(groups, axis=-1))
    mixed = jnp.concatenate(chunks, axis=0)
    out = _dot((u * mixed).astype(BF16), w_out_ref[...])
    _post_mixer(x, out, ada, ln1g_ref[...], ln1b_ref[...], wr_ref, br_ref, x1_ref, h2_ref, lg_ref)

    @pl.when(t == pl.num_programs(1) - 1)
    def _():
        st_ref[0] = v[tt - CHUNK:, :]


def _full(shape):
    nd = len(shape)
    return pl.BlockSpec(shape, lambda b, t: (0,) * nd)


def _mix_prompt(kernel_fn, name, x, ada_p, weights, state_rows, scratch, seq):
    n = x.shape[0]
    batch = n // seq
    nt = seq // TOK_TILE
    tok = pl.BlockSpec((TOK_TILE, D), lambda b, t: (b * nt + t, 0))
    lgt = pl.BlockSpec((TOK_TILE, LANES), lambda b, t: (b * nt + t, 0))
    return pl.pallas_call(
        kernel_fn,
        out_shape=(jax.ShapeDtypeStruct((n, D), F32), jax.ShapeDtypeStruct((n, D), BF16),
                   jax.ShapeDtypeStruct((n, LANES), F32), jax.ShapeDtypeStruct((batch, state_rows, D), F32)),
        grid=(batch, nt),
        in_specs=[tok, pl.BlockSpec((1, 1, 6 * D), lambda b, t: (b, 0, 0))] + [_full(w.shape) for w in weights],
        out_specs=(tok, tok, lgt, pl.BlockSpec((1, state_rows, D), lambda b, t: (b, 0, 0))),
        scratch_shapes=scratch,
        compiler_params=_cparams(("arbitrary", "arbitrary")),
        name=name,
    )(x, ada_p, *weights)


def _mix_a_sample_kernel(x_ref, ada_ref, hist_ref, w_in_ref, b_in_ref, w_dw_ref, b_dw_ref, lng_ref, lnb_ref,
                         w_out_ref, ln1g_ref, ln1b_ref, wr_ref, br_ref, x1_ref, h2_ref, lg_ref, new_ref):
    x = x_ref[...]
    ada = ada_ref[...]
    sh1, sc1 = ada[:, 0:D], ada[:, D:2 * D]
    h = (x * (1.0 + sc1) + sh1).astype(BF16)
    z = _dot(h, w_in_ref[...]) + b_in_ref[...]
    u = z[:, :D] * jax.nn.sigmoid(z[:, D:])
    y = b_dw_ref[...] + w_dw_ref[CONV_A_WIDTH - 1:CONV_A_WIDTH, :] * u
    for k in range(CONV_A_WIDTH - 1):
        y = y + w_dw_ref[k:k + 1, :] * hist_ref[k]
    y = _ln(y, lng_ref[...], lnb_ref[...])
    y = y * jax.nn.sigmoid(y)
    out = _dot(y.astype(BF16), w_out_ref[...])
    _post_mixer(x, out, ada, ln1g_ref[...], ln1b_ref[...], wr_ref, br_ref, x1_ref, h2_ref, lg_ref)
    new_ref[...] = u


def _mix_b_sample_kernel(x_ref, ada_ref, hist_ref, w_grp_ref, scale_ref, ln1g_ref, ln1b_ref,
                         wr_ref, br_ref, x1_ref, h2_ref, lg_ref, new_ref):
    x = x_ref[...]
    ada = ada_ref[...]
    sh1, sc1 = ada[:, 0:D], ada[:, D:2 * D]
    h = x * (1.0 + sc1) + sh1
    outs = []
    for gi, w in enumerate(POOL_WINDOWS):
        lo = gi * POOL_GROUP
        s = h[:, lo:lo + POOL_GROUP]
        for j in range(1, w):
            s = s + hist_ref[POOL_HIST - j][:, lo:lo + POOL_GROUP]
        pooled = s / float(w) - h[:, lo:lo + POOL_GROUP]
        outs.append(_dot_split(pooled, w_grp_ref[gi].astype(BF16)))
    out = jnp.concatenate(outs, axis=-1) * scale_ref[...]
    _post_mixer(x, out, ada, ln1g_ref[...], ln1b_ref[...], wr_ref, br_ref, x1_ref, h2_ref, lg_ref)
    new_ref[...] = h


def _mix_c_sample_kernel(x_ref, ada_ref, hist_ref, w_in_ref, w_conv_ref, w_out_ref, ln1g_ref, ln1b_ref,
                         wr_ref, br_ref, x1_ref, h2_ref, lg_ref, new_ref):
    x = x_ref[...]
    ada = ada_ref[...]
    sh1, sc1 = ada[:, 0:D], ada[:, D:2 * D]
    h = (x * (1.0 + sc1) + sh1).astype(BF16)
    z = _dot(h, w_in_ref[...])
    bg = z[:, :D]
    v = z[:, D:2 * D] * z[:, 2 * D:]
    y = w_conv_ref[CONV_C_WIDTH - 1:CONV_C_WIDTH, :] * v
    for k in range(CONV_C_WIDTH - 1):
        y = y + w_conv_ref[k:k + 1, :] * hist_ref[k]
    out = _dot((bg * y).astype(BF16), w_out_ref[...])
    _post_mixer(x, out, ada, ln1g_ref[...], ln1b_ref[...], wr_ref, br_ref, x1_ref, h2_ref, lg_ref)
    new_ref[...] = v


def _mix_d_sample_kernel(x_ref, ada_ref, w_in_ref, b_in_ref, lng_ref, lnb_ref, w_s0_ref, b_s0_ref,
                         w_out_ref, ln1g_ref, ln1b_ref, wr_ref, br_ref, x1_ref, h2_ref, lg_ref, new_ref):
    x = x_ref[...]
    ada = ada_ref[...]
    sh1, sc1 = ada[:, 0:D], ada[:, D:2 * D]
    h = (x * (1.0 + sc1) + sh1).astype(BF16)
    z = _gelu_exact(_dot(h, w_in_ref[...]) + b_in_ref[...])
    u = z[:, :D]
    v = _ln(z[:, D:], lng_ref[...], lnb_ref[...])
    mixed = w_s0_ref[...] * v + b_s0_ref[...]
    out = _dot((u * mixed).astype(BF16), w_out_ref[...])
    _post_mixer(x, out, ada, ln1g_ref[...], ln1b_ref[...], wr_ref, br_ref, x1_ref, h2_ref, lg_ref)
    new_ref[...] = v


def _mix_sample(kernel_fn, name, x, ada_s, arrays):
    n = x.shape[0]
    return pl.pallas_call(
        kernel_fn,
        out_shape=(jax.ShapeDtypeStruct((n, D), F32), jax.ShapeDtypeStruct((n, D), BF16),
                   jax.ShapeDtypeStruct((n, LANES), F32), jax.ShapeDtypeStruct((n, D), F32)),
        compiler_params=_cparams(),
        name=name,
    )(x, ada_s, *arrays)


def _route_kernel(hp_ref, hs_ref, lgp_ref, lgs_ref, xs_ref, meta_ref, cnt_ref, *, n_valid):
    i = pl.program_id(0)
    tt = hp_ref.shape[0]
    ns = hs_ref.shape[0]
    cap = xs_ref.shape[0]
    is_sample = i == pl.num_programs(0) - 1
    hs = jnp.concatenate([hs_ref[...], jnp.zeros((tt - ns, D), BF16)], axis=0)
    lgs = jnp.concatenate([lgs_ref[...], jnp.zeros((tt - ns, LANES), F32)], axis=0)
    hb = jnp.where(is_sample, hs, hp_ref[...])
    logits = jnp.where(is_sample, lgs, lgp_ref[...]).T[:N_EXPERTS, :]
    e_iota = lax.broadcasted_iota(jnp.int32, (N_EXPERTS, tt), 0)
    valid = (i * tt + lax.broadcasted_iota(jnp.int32, (1, tt), 1)) < n_valid
    work = logits
    sel, top = [], []
    for _ in range(TOP_K):
        m = jnp.max(work, axis=0, keepdims=True)
        idx = jnp.min(jnp.where(work == m, e_iota, N_EXPERTS), axis=0, keepdims=True)
        oh = e_iota == idx
        sel.append(oh)
        top.append(m)
        work = jnp.where(oh, -jnp.inf, work)
    ex = [jnp.exp(v - top[0]) for v in top]
    denom = ex[0] + ex[1] + ex[2] + ex[3]
    gates = [e / denom for e in ex]

    member = jnp.where(sel[0] | sel[1] | sel[2] | sel[3], 1.0, 0.0)
    member = jnp.where(valid, member, 0.0)
    r_i = lax.broadcasted_iota(jnp.int32, (tt, tt), 0)
    c_i = lax.broadcasted_iota(jnp.int32, (tt, tt), 1)
    before = jnp.where(r_i < c_i, 1.0, 0.0).astype(BF16)
    rank = _dot(member.astype(BF16), before)
    count = jnp.sum(member, axis=1, keepdims=True)
    chunks = jnp.floor((count + float(ROW_CHUNK - 1)) * (1.0 / ROW_CHUNK))
    chunks_b = jnp.broadcast_to(chunks, (N_EXPERTS, LANES))
    er = lax.broadcasted_iota(jnp.int32, (N_EXPERTS, N_EXPERTS), 0)
    ec = lax.broadcasted_iota(jnp.int32, (N_EXPERTS, N_EXPERTS), 1)
    lower = jnp.where(ec < er, 1.0, 0.0).astype(BF16)
    chunk_off = _dot(lower, chunks_b.astype(BF16))
    base = chunk_off[:, 0:1] * float(ROW_CHUNK)
    slot = base + rank
    pos = [jnp.where(valid, jnp.sum(jnp.where(s, slot, 0.0), axis=0, keepdims=True), -1.0) for s in sel]

    row = lax.broadcasted_iota(jnp.int32, (cap, tt), 0).astype(jnp.int16)
    pos16 = [p.astype(jnp.int32).astype(jnp.int16) for p in pos]
    one, zero = jnp.ones((), BF16), jnp.zeros((), BF16)
    onehot = jnp.where(row == pos16[0], one, zero)
    for k in range(1, TOP_K):
        onehot = jnp.where(row == pos16[k], one, onehot)
    xs_ref[...] = _dot(onehot, hb).astype(BF16)

    meta = jnp.concatenate(pos + gates + [jnp.zeros((LANES - 2 * TOP_K, tt), F32)], axis=0)
    meta_ref[...] = meta.T
    cnt_ref[0] = chunks_b.astype(jnp.int32)


def _route(h2_p, h2_s, lg_p, lg_s):
    n_p, n_s = h2_p.shape[0], h2_s.shape[0]
    ntp = n_p // MOE_TILE
    nt = ntp + 1
    cap = _tile_capacity(MOE_TILE)
    prompt_tile = lambda i: (jnp.minimum(i, ntp - 1), 0)
    return pl.pallas_call(
        functools.partial(_route_kernel, n_valid=n_p + n_s),
        out_shape=(jax.ShapeDtypeStruct((nt * cap, D), BF16),
                   jax.ShapeDtypeStruct((nt * MOE_TILE, LANES), F32),
                   jax.ShapeDtypeStruct((nt, N_EXPERTS, LANES), jnp.int32)),
        grid=(nt,),
        in_specs=[pl.BlockSpec((MOE_TILE, D), prompt_tile),
                  pl.BlockSpec((n_s, D), lambda i: (0, 0)),
                  pl.BlockSpec((MOE_TILE, LANES), prompt_tile),
                  pl.BlockSpec((n_s, LANES), lambda i: (0, 0))],
        out_specs=(pl.BlockSpec((cap, D), lambda i: (i, 0)),
                   pl.BlockSpec((MOE_TILE, LANES), lambda i: (i, 0)),
                   pl.BlockSpec((1, N_EXPERTS, LANES), lambda i: (i, 0, 0))),
        compiler_params=_cparams(("arbitrary",)),
        name="route_dispatch",
    )(h2_p, h2_s, lg_p, lg_s)


def _plan_kernel(cnt_ref, be_ref, nx_ref, half_ref, nb_ref, src_ref, dst_ref, z_ref, nz_ref, tile_off, blk0_s,
                 nblk_s, npad_s, *, nt, cap_chunks, nb_max, max_pad, nz_max):
    del nb_max, nz_max
    i32 = jnp.int32
    trash_base = nt * cap_chunks
    shift = BLOCK_CHUNKS.bit_length() - 1

    def index_grid(ref):
        r, c = ref.shape
        return lax.broadcasted_iota(i32, (r, c), 0) * c + lax.broadcasted_iota(i32, (r, c), 1)

    def in_run(idx, start, length):
        return jnp.logical_and(idx >= start, idx < start + length)

    for t in range(nt):
        tile_off[t] = 0

    d = index_grid(src_ref)

    def expert_body(e, carry):
        p0, blk0, pad0, src, dst = carry

        def tile_body(t, inner):
            p, src, dst = inner
            c = cnt_ref[t * N_EXPERTS + e]
            off = tile_off[t]
            tile_off[t] = off + c
            run = in_run(d, p, c)
            val = d + (t * cap_chunks + off - p)
            return p + c, jnp.where(run, val, src), jnp.where(run, val, dst)

        p1, src, dst = lax.fori_loop(0, nt, tile_body, (p0, src, dst))
        nblk = lax.shift_right_logical(p1 - p0 + (BLOCK_CHUNKS - 1), shift)
        p2 = p0 + nblk * BLOCK_CHUNKS
        pad = in_run(d, p1, p2 - p1)
        src = jnp.where(pad, 0, src)
        dst = jnp.where(pad, d + (trash_base + pad0 - p1), dst)
        blk0_s[e] = blk0
        nblk_s[e] = nblk
        npad_s[e] = p2 - p1
        return p2, blk0 + nblk, pad0 + (p2 - p1), src, dst

    zeros = jnp.zeros(src_ref.shape, i32)
    zero = jnp.int32(0)
    _, nb, n_pad, src, dst = lax.fori_loop(0, N_EXPERTS, expert_body, (zero, zero, zero, zeros, zeros))
    src_ref[...] = src
    dst_ref[...] = dst
    nb_ref[0] = nb

    bi = index_grid(be_ref)

    def block_body(k, carry):
        be, nx, half, nxt = carry
        e = N_EXPERTS - 1 - k
        b0, n = blk0_s[e], nblk_s[e]
        mine = in_run(bi, b0, n)
        last = jnp.logical_and(bi == b0 + n - 1, n > 0)
        empty_quarters = lax.shift_right_logical(npad_s[e], QUARTER_CHUNKS.bit_length() - 1)
        return (jnp.where(mine, e, be), jnp.where(mine, nxt, nx), jnp.where(last, empty_quarters, half),
                jnp.where(n > 0, e, nxt))

    init = (jnp.full(be_ref.shape, N_EXPERTS - 1, i32), jnp.full(be_ref.shape, -1, i32),
            jnp.zeros(be_ref.shape, i32), jnp.int32(-1))
    be, nx, half, _ = lax.fori_loop(0, N_EXPERTS, block_body, init)
    be_ref[...] = be
    nx_ref[...] = nx
    half_ref[...] = half

    zi = index_grid(z_ref)

    def tail_body(t, carry):
        z0, z = carry
        used = tile_off[t]
        n = cap_chunks - used
        return z0 + n, jnp.where(in_run(zi, z0, n), zi + (t * cap_chunks + used - z0), z)

    z1, z = lax.fori_loop(0, nt, tail_body, (zero, jnp.zeros(z_ref.shape, i32)))
    n_trash = max_pad - n_pad
    z_ref[...] = jnp.where(in_run(zi, z1, n_trash), zi + (trash_base + n_pad - z1), z)
    nz_ref[0] = z1 + n_trash


def _plan(cnt_flat, nt, cap_chunks, nb_max, max_pad, nz_max):
    smem = pl.BlockSpec(memory_space=pltpu.SMEM)
    vmem = pl.BlockSpec(memory_space=pltpu.VMEM)
    i32 = jnp.int32

    def table(n):
        return jax.ShapeDtypeStruct((-(-n // (8 * LANES)) * 8, LANES), i32)

    out = pl.pallas_call(
        functools.partial(_plan_kernel, nt=nt, cap_chunks=cap_chunks, nb_max=nb_max, max_pad=max_pad, nz_max=nz_max),
        out_shape=(table(nb_max), table(nb_max), table(nb_max), jax.ShapeDtypeStruct((1,), i32),
                   table(nb_max * BLOCK_CHUNKS), table(nb_max * BLOCK_CHUNKS),
                   table(nz_max), jax.ShapeDtypeStruct((1,), i32)),
        in_specs=[smem],
        out_specs=(vmem, vmem, vmem, smem, vmem, vmem, vmem, smem),
        scratch_shapes=[pltpu.SMEM((nt,), i32), pltpu.SMEM((N_EXPERTS,), i32), pltpu.SMEM((N_EXPERTS,), i32),
                        pltpu.SMEM((N_EXPERTS,), i32)],
        name="plan",
    )(cnt_flat)
    return tuple(o.reshape(-1) for o in out)


def _expert_kernel(be_ref, nx_ref, half_ref, nb_ref, src_ref, dst_ref, z_ref, nz_ref, xs_hbm, w_in_hbm, b_in_ref,
                   w_out_hbm, b_out_ref, ys_hbm, xbuf, ybuf, w_in_f32, w_out_f32, w_in_bf, w_out_bf, zbuf,
                   sem_in, sem_out, sem_z, sem_w, *, layer):
    b = pl.program_id(0)
    nb = nb_ref[0]
    nz = nz_ref[0]
    slot = lax.rem(b, 2)

    def weight_copies(e):
        return (pltpu.make_async_copy(w_in_hbm.at[layer, e], w_in_f32, sem_w.at[0]),
                pltpu.make_async_copy(w_out_hbm.at[layer, e], w_out_f32, sem_w.at[1]))

    def ffn(rows):
        e = be_ref[b]
        x = xbuf[slot, 0:rows, :]
        h = _dot(x, w_in_bf[...]) + b_in_ref[layer, e]
        g = jnp.minimum(h[:, :D], SWIGLU_LIMIT)
        u = jnp.clip(h[:, D:], -SWIGLU_LIMIT, SWIGLU_LIMIT)
        a = (u + 1.0) * (g * jax.nn.sigmoid(SWIGLU_ALPHA * g))
        y = _dot(a.astype(BF16), w_out_bf[...]) + b_out_ref[layer, e]
        ybuf[slot, 0:rows, :] = y.astype(BF16)

    def chunk_rows(c):
        return pl.ds(pl.multiple_of(c * ROW_CHUNK, ROW_CHUNK), ROW_CHUNK)

    def gather(blk, s):
        for j in range(BLOCK_CHUNKS):
            c = src_ref[blk * BLOCK_CHUNKS + j]
            pltpu.make_async_copy(xs_hbm.at[chunk_rows(c)], xbuf.at[s, pl.ds(j * ROW_CHUNK, ROW_CHUNK)],
                                  sem_in.at[s]).start()

    def gather_wait(s):
        pltpu.make_async_copy(xs_hbm.at[pl.ds(0, BLOCK_ROWS)], xbuf.at[s], sem_in.at[s]).wait()

    def scatter(blk, s):
        for j in range(BLOCK_CHUNKS):
            c = dst_ref[blk * BLOCK_CHUNKS + j]
            pltpu.make_async_copy(ybuf.at[s, pl.ds(j * ROW_CHUNK, ROW_CHUNK)], ys_hbm.at[chunk_rows(c)],
                                  sem_out.at[s]).start(priority=j % 2)

    def scatter_wait(s):
        pltpu.make_async_copy(ybuf.at[s], ys_hbm.at[pl.ds(0, BLOCK_ROWS)], sem_out.at[s]).wait()

    @pl.when(b == 0)
    def _():
        for c in weight_copies(be_ref[0]):
            c.start(priority=1)
        gather(0, 0)
        zbuf[...] = jnp.zeros(zbuf.shape, zbuf.dtype)
        ybuf[...] = jnp.zeros(ybuf.shape, ybuf.dtype)

        def zero_one(i, _):
            pltpu.make_async_copy(zbuf, ys_hbm.at[chunk_rows(z_ref[i])], sem_z).start()
            return 0

        lax.fori_loop(0, nz, zero_one, 0)

    @pl.when(b < nb)
    def _():
        new_expert = jnp.logical_or(b == 0, be_ref[b] != be_ref[jnp.maximum(b - 1, 0)])

        @pl.when(new_expert)
        def _():
            for c in weight_copies(be_ref[b]):
                c.wait()
            w_in_bf[...] = w_in_f32[...].astype(BF16)
            w_out_bf[...] = w_out_f32[...].astype(BF16)

            @pl.when(nx_ref[b] >= 0)
            def _():
                for c in weight_copies(nx_ref[b]):
                    c.start(priority=1)

        gather_wait(slot)

        @pl.when(b + 1 < nb)
        def _():
            gather(b + 1, 1 - slot)

        @pl.when(b >= 2)
        def _():
            scatter_wait(slot)

        for empty in range(4):
            @pl.when(half_ref[b] == empty)
            def _(empty=empty):
                ffn(BLOCK_ROWS - empty * QUARTER_ROWS)

        scatter(b, slot)

        @pl.when(b == 0)
        def _():
            def zero_wait(i, _):
                pltpu.make_async_copy(zbuf, ys_hbm.at[pl.ds(0, ROW_CHUNK)], sem_z).wait()
                return 0

            lax.fori_loop(0, nz, zero_wait, 0)

        @pl.when(b == nb - 1)
        def _():
            scatter_wait(slot)

            @pl.when(b >= 1)
            def _():
                scatter_wait(1 - slot)


def _experts(xs, plan, layer, w_in, b_in, w_out, b_out, nb_max, ys_rows):
    resident = lambda b, *_: (0, 0, 0, 0)
    grid_spec = pltpu.PrefetchScalarGridSpec(
        num_scalar_prefetch=8,
        grid=(nb_max,),
        in_specs=[pl.BlockSpec(memory_space=pl.ANY),
                  pl.BlockSpec(memory_space=pl.ANY),
                  pl.BlockSpec(b_in.shape, resident),
                  pl.BlockSpec(memory_space=pl.ANY),
                  pl.BlockSpec(b_out.shape, resident)],
        out_specs=pl.BlockSpec(memory_space=pl.ANY),
        scratch_shapes=[pltpu.VMEM((2, BLOCK_ROWS, D), BF16), pltpu.VMEM((2, BLOCK_ROWS, D), BF16),
                        pltpu.VMEM((D, 2 * D), F32), pltpu.VMEM((D, D), F32),
                        pltpu.VMEM((D, 2 * D), BF16), pltpu.VMEM((D, D), BF16), pltpu.VMEM((ROW_CHUNK, D), BF16),
                        pltpu.SemaphoreType.DMA((2,)), pltpu.SemaphoreType.DMA((2,)), pltpu.SemaphoreType.DMA(()),
                        pltpu.SemaphoreType.DMA((2,))],
    )
    return pl.pallas_call(
        functools.partial(_expert_kernel, layer=layer),
        out_shape=jax.ShapeDtypeStruct((ys_rows, D), BF16),
        grid_spec=grid_spec,
        compiler_params=_cparams(("arbitrary",)),
        name="experts",
    )(*plan, xs, w_in, b_in, w_out, b_out)


def _combine_kernel(ys_ref, meta_ref, x1_ref, ada_ref, lng_ref, lnb_ref, o_ref):
    tt = x1_ref.shape[0]
    cap = ys_ref.shape[0]
    meta = meta_ref[...]
    col = lax.broadcasted_iota(jnp.int32, (tt, cap), 1).astype(jnp.int16)
    gmat = jnp.zeros((tt, cap), BF16)
    for k in range(TOP_K):
        pos_k = meta[:, k:k + 1].astype(jnp.int32).astype(jnp.int16)
        gate_k = meta[:, TOP_K + k:TOP_K + k + 1].astype(BF16)
        gmat = jnp.where(col == pos_k, gate_k, gmat)
    f = _dot(gmat, ys_ref[...])
    ada = ada_ref[0] if len(ada_ref.shape) == 3 else ada_ref[...]
    g2 = ada[:, 5 * D:6 * D]
    o_ref[...] = _ln(ALPHA * x1_ref[...] + (1.0 + g2) * f, lng_ref[...], lnb_ref[...])


def _combine_prompt(ys, meta, x1, ada_p, ln_g, ln_b, seq):
    n = x1.shape[0]
    cap = _tile_capacity(MOE_TILE)
    per_seq = seq // MOE_TILE
    return pl.pallas_call(
        _combine_kernel,
        out_shape=jax.ShapeDtypeStruct((n, D), F32),
        grid=(n // MOE_TILE,),
        in_specs=[pl.BlockSpec((cap, D), lambda i: (i, 0)),
                  pl.BlockSpec((MOE_TILE, LANES), lambda i: (i, 0)),
                  pl.BlockSpec((MOE_TILE, D), lambda i: (i, 0)),
                  pl.BlockSpec((1, 1, 6 * D), lambda i: (i // per_seq, 0, 0)),
                  pl.BlockSpec((1, D), lambda i: (0, 0)),
                  pl.BlockSpec((1, D), lambda i: (0, 0))],
        out_specs=pl.BlockSpec((MOE_TILE, D), lambda i: (i, 0)),
        compiler_params=_cparams(("arbitrary",)),
        name="combine_prompt",
    )(ys, meta, x1, ada_p, ln_g, ln_b)


def _combine_sample(ys, meta, x1, ada_s, ln_g, ln_b, tile):
    n = x1.shape[0]
    cap = _tile_capacity(MOE_TILE)
    return pl.pallas_call(
        _combine_kernel,
        out_shape=jax.ShapeDtypeStruct((n, D), F32),
        grid=(1,),
        in_specs=[pl.BlockSpec((cap, D), lambda i: (tile, 0)),
                  pl.BlockSpec((n, LANES), lambda i: (tile * (MOE_TILE // n), 0)),
                  pl.BlockSpec((n, D), lambda i: (0, 0)),
                  pl.BlockSpec((n, 6 * D), lambda i: (0, 0)),
                  pl.BlockSpec((1, D), lambda i: (0, 0)),
                  pl.BlockSpec((1, D), lambda i: (0, 0))],
        out_specs=pl.BlockSpec((n, D), lambda i: (0, 0)),
        compiler_params=_cparams(("arbitrary",)),
        name="combine_sample",
    )(ys, meta, x1, ada_s, ln_g, ln_b)


def _moe_and_norm(h2_p, h2_s, lg_p, lg_s, x1_p, x1_s, ada_p, ada_s, ln_g, ln_b,
                  layer, w_in, b_in, w_out, b_out, seq):
    n_p, n_s = h2_p.shape[0], h2_s.shape[0]
    n_valid = n_p + n_s
    nt = n_p // MOE_TILE + 1
    cap = _tile_capacity(MOE_TILE)
    cap_chunks = cap // ROW_CHUNK
    xs, meta, cnt = _route(h2_p, h2_s, lg_p, lg_s)
    total_chunks_max = (TOP_K * n_valid + nt * N_EXPERTS * (ROW_CHUNK - 1)) // ROW_CHUNK
    nb_max = -(-total_chunks_max // BLOCK_CHUNKS) + N_EXPERTS
    max_pad = N_EXPERTS * (BLOCK_CHUNKS - 1)
    nz_max = nt * cap_chunks - (TOP_K * n_valid) // ROW_CHUNK + max_pad
    plan = _plan(cnt[:, :, 0].reshape(-1), nt, cap_chunks, nb_max, max_pad, nz_max)
    ys = _experts(xs, plan, layer, w_in, b_in, w_out, b_out, nb_max, (nt * cap_chunks + max_pad) * ROW_CHUNK)
    x2_p = _combine_prompt(ys, meta, x1_p, ada_p, ln_g, ln_b, seq)
    x2_s = _combine_sample(ys, meta, x1_s, ada_s, ln_g, ln_b, n_p // MOE_TILE)
    return x2_p, x2_s


def _row(v):
    return v.reshape(1, -1)


def kernel(x_prompt, x_sample, state_conv_a, state_pool_b, state_conv_c, c_prompt, c_sample,
           w_ada, b_ada, ln1_g, ln1_b, ln2_g, ln2_b,
           a_w_in, a_b_in, a_w_dw, a_b_dw, a_ln_g, a_ln_b, a_w_out,
           b_w_grp, b_scale, c_w_in, c_w_conv, c_w_out,
           d_w_in, d_b_in, d_ln_g, d_ln_b, d_w_s, d_b_s, d_w_out,
           w_router, b_router, w_moe_in, b_moe_in, w_moe_out, b_moe_out):
    bp, seq, _ = x_prompt.shape
    bs = x_sample.shape[0]
    assert seq % TOK_TILE == 0 and seq % MOE_TILE == 0 and MOE_TILE % bs == 0
    assert x_sample.shape[1] == 1 and w_ada.shape[0] == DEPTH == 4

    ada = _ada(jnp.concatenate([c_prompt, c_sample], axis=0), w_ada, b_ada)
    xp = x_prompt.reshape(bp * seq, D)
    xs = x_sample.reshape(bs, D)
    b_moe_in4 = b_moe_in.reshape(DEPTH, N_EXPERTS, 1, 2 * D)
    b_moe_out4 = b_moe_out.reshape(DEPTH, N_EXPERTS, 1, D)
    states = {}

    for i in range(DEPTH):
        ada_p = ada[i, :bp].reshape(bp, 1, 6 * D)
        ada_s = ada[i, bp:]
        wr = jnp.pad(w_router[i], ((0, 0), (0, LANES - N_EXPERTS)))
        ln1 = [_row(ln1_g[i]), _row(ln1_b[i]), wr, _row(jnp.pad(b_router[i], (0, LANES - N_EXPERTS)))]
        if i == 0:
            wts = [a_w_in[0].astype(BF16), _row(a_b_in[0]), a_w_dw[0], _row(a_b_dw[0]), _row(a_ln_g[0]),
                   _row(a_ln_b[0]), a_w_out[0].astype(BF16)] + ln1
            x1_p, h2_p, lg_p, st =_mix_prompt(_mix_a_prompt_kernel, "mix_a_prompt", xp, ada_p, wts, A_HALO,
                                         [pltpu.VMEM((A_HALO + TOK_TILE, D), F32),
                                          pltpu.VMEM((8, A_HALO + TOK_TILE, LANES), F32),
                                          pltpu.VMEM((TOK_TILE, D), F32)], seq)
            states["a_p"] = st[:, A_HALO - (CONV_A_WIDTH - 1):][None]
            hist = jnp.transpose(state_conv_a[0], (1, 0, 2))
            x1_s, h2_s, lg_s, new =_mix_sample(_mix_a_sample_kernel, "mix_a_sample", xs, ada_s, [hist] + wts)
            states["a_s"] = jnp.concatenate([state_conv_a[0][:, 1:], new[:, None]], axis=1)[None]
        elif i == 1:
            wts = [b_w_grp[0], _row(b_scale[0])] + ln1
            x1_p, h2_p, lg_p, st =_mix_prompt(_mix_b_prompt_kernel, "mix_b_prompt", xp, ada_p, wts, B_HALO,
                                         [pltpu.VMEM((B_HALO + TOK_TILE, D), F32),
                                          pltpu.VMEM((B_HALO + TOK_TILE, POOL_GROUP), F32),
                                          pltpu.VMEM((B_HALO + TOK_TILE, POOL_GROUP), F32)], seq)
            states["b_p"] = st[:, B_HALO - POOL_HIST:][None]
            hist = jnp.transpose(state_pool_b[0], (1, 0, 2))
            x1_s, h2_s, lg_s, new =_mix_sample(_mix_b_sample_kernel, "mix_b_sample", xs, ada_s, [hist] + wts)
            states["b_s"] = jnp.concatenate([state_pool_b[0][:, 1:], new[:, None]], axis=1)[None]
        elif i == 2:
            wts = [c_w_in[0].astype(BF16), c_w_conv[0], c_w_out[0].astype(BF16)] + ln1
            x1_p, h2_p, lg_p, st =_mix_prompt(_mix_c_prompt_kernel, "mix_c_prompt", xp, ada_p, wts, C_HALO,
                                         [pltpu.VMEM((C_HALO + TOK_TILE, D), F32)], seq)
            states["c_p"] = st[:, C_HALO - (CONV_C_WIDTH - 1):][None]
            hist = jnp.transpose(state_conv_c[0], (1, 0, 2))
            x1_s, h2_s, lg_s, new =_mix_sample(_mix_c_sample_kernel, "mix_c_sample", xs, ada_s, [hist] + wts)
            states["c_s"] = jnp.concatenate([state_conv_c[0][:, 1:], new[:, None]], axis=1)[None]
        else:
            common = [d_w_in[0].astype(BF16), _row(d_b_in[0]), _row(d_ln_g[0]), _row(d_ln_b[0])]
            wts = common + [d_w_s[0], d_b_s[0].T, d_w_out[0].astype(BF16)] + ln1
            x1_p, h2_p, lg_p, st =_mix_prompt(_mix_d_prompt_kernel, "mix_d_prompt", xp, ada_p, wts, CHUNK, [], seq)
            states["d_p"] = st[None]
            w_s0 = jnp.repeat(d_w_s[0][:, 0, 0], SGU_GROUP).reshape(1, D)
            b_s0 = jnp.repeat(d_b_s[0][:, 0], SGU_GROUP).reshape(1, D)
            wts_s = common + [w_s0, b_s0, d_w_out[0].astype(BF16)] + ln1
            x1_s, h2_s, lg_s, new =_mix_sample(_mix_d_sample_kernel, "mix_d_sample", xs, ada_s, wts_s)
            states["d_s"] = new[:, None][None]

        xp, xs = _moe_and_norm(h2_p, h2_s, lg_p, lg_s, x1_p, x1_s, ada_p, ada_s, _row(ln2_g[i]), _row(ln2_b[i]),
                               i, w_moe_in, b_moe_in4, w_moe_out, b_moe_out4, seq)

    return (xp.reshape(bp, seq, D), xs.reshape(bs, 1, D),
            states["a_p"], states["a_s"], states["b_p"], states["b_s"],
            states["c_p"], states["c_s"], states["d_p"], states["d_s"])
```
